```python
import math
import jax
import jax.numpy as jnp
from jax import lax
import numpy as np

D_MODEL = 1024
BATCH = 16
SEQ = 2048
DEPTH = 2
DEC_BATCH = 8
DEC_SEQ = 16
PAST_LEN = 2048

CHUNK = 64
Q_BLOCK = 128

MLA_HEADS = 8
MLA_NOPE = 64
MLA_ROPE = 32
MLA_V = 64
MLA_Q_RANK = 384
MLA_KV_RANK = 256
MLA_DIM = MLA_HEADS * MLA_V
ROPE_THETA = 10000.0

RWKV_HEADS = 8
RWKV_HEAD = 64
RWKV_DIM = RWKV_HEADS * RWKV_HEAD
RWKV_W_RANK = 64
RWKV_A_RANK = 64
RWKV_G_RANK = 128
RWKV_SIZES = (RWKV_DIM, RWKV_DIM, RWKV_DIM, RWKV_W_RANK, RWKV_A_RANK, RWKV_G_RANK)
RWKV_PROJ = sum(RWKV_SIZES)
RWKV_LN_EPS = 64e-5

SSM_DIM = 512
SSM_GROUP = 16
SSM_GROUPS = SSM_DIM // SSM_GROUP
SSM_STATE = 64
DT_MIN = 0.001
DT_MAX = 0.1

IN_SIZES = (MLA_Q_RANK, MLA_KV_RANK + MLA_ROPE, RWKV_PROJ, SSM_DIM, D_MODEL, D_MODEL, D_MODEL)
IN_DIM = sum(IN_SIZES)

N_EXPERTS = 32
TOP_K = 4
D_FF = 512
SWIGLU_LIMIT = 7.0
SWIGLU_ALPHA = 1.702

DN_ALPHA = (2 * DEPTH) ** 0.25
DN_BETA = (8 * DEPTH) ** -0.25
LN_EPS = 1e-5
RMS_EPS = 1e-6
NEG_INF = -1e30

kernel_name = 'hybrid_mla_rwkv7_s5_moe_stream_step'


def split_last(x, sizes):
    return jnp.split(x, np.cumsum(sizes)[:-1].tolist(), axis=-1)


def layer_norm(x, g, b):
    xf = x.astype(jnp.float32)
    mu = jnp.mean(xf, -1, keepdims=True)
    var = jnp.mean(jnp.square(xf - mu), -1, keepdims=True)
    return ((xf - mu) * lax.rsqrt(var + LN_EPS)).astype(x.dtype) * g + b


def rms_norm(x, g):
    xf = x.astype(jnp.float32)
    return (xf * lax.rsqrt(jnp.mean(xf * xf, -1, keepdims=True) + RMS_EPS)).astype(x.dtype) * g


def rope(x, pos):
    half = x.shape[-1] // 2
    inv = ROPE_THETA ** (-jnp.arange(half, dtype=jnp.float32) / half)
    ang = pos.astype(jnp.float32)[:, None] * inv
    shp = (pos.shape[0],) + (1,) * (x.ndim - 3) + (half,)
    cos = jnp.cos(ang).reshape(shp)
    sin = jnp.sin(ang).reshape(shp)
    x1 = x[..., :half].astype(jnp.float32)
    x2 = x[..., half:].astype(jnp.float32)
    return jnp.concatenate([x1 * cos - x2 * sin, x2 * cos + x1 * sin], -1).astype(x.dtype)


def chunk_causal_attention(q_nope, q_pe, k_nope, k_pe, v, q_pos, k_pos):
    B, Sq, H, _ = q_nope.shape
    scale = (MLA_NOPE + MLA_ROPE) ** -0.5
    k_chunk = k_pos // CHUNK

    def block(args):
        qn, qp, qpos = args
        s = (jnp.einsum('bqhd,bkhd->bhqk', qn, k_nope)
             + jnp.einsum('bqhr,bkr->bhqk', qp, k_pe)).astype(jnp.float32) * scale
        visible = k_chunk[None, :] <= (qpos // CHUNK)[:, None]
        p = jax.nn.softmax(jnp.where(visible, s, NEG_INF), axis=-1)
        return jnp.einsum('bhqk,bkhd->bqhd', p.astype(v.dtype), v)

    if Sq <= Q_BLOCK:
        return block((q_nope, q_pe, q_pos))
    nb = Sq // Q_BLOCK
    to_blocks = lambda t: jnp.moveaxis(t.reshape((B, nb, Q_BLOCK) + t.shape[2:]), 1, 0)
    out = lax.map(block, (to_blocks(q_nope), to_blocks(q_pe), q_pos.reshape(nb, Q_BLOCK)))
    return jnp.moveaxis(out, 0, 1).reshape(B, Sq, H, MLA_V)


def mla_mix(q_a, kv_a, pos, ckv_past, kpe_past, q_a_norm, w_q_b, kv_a_norm, w_kv_b):
    B, S, _ = q_a.shape
    q = (rms_norm(q_a, q_a_norm) @ w_q_b).reshape(B, S, MLA_HEADS, MLA_NOPE + MLA_ROPE)
    q_nope = q[..., :MLA_NOPE]
    q_pe = rope(q[..., MLA_NOPE:], pos)
    ckv = rms_norm(kv_a[..., :MLA_KV_RANK], kv_a_norm)
    kpe = rope(kv_a[..., MLA_KV_RANK:], pos)
    if ckv_past is None:
        ckv_all, kpe_all, k_pos = ckv, kpe, pos
    else:
        ckv_all = jnp.concatenate([ckv_past.astype(ckv.dtype), ckv], axis=1)
        kpe_all = jnp.concatenate([kpe_past.astype(kpe.dtype), kpe], axis=1)
        k_pos = jnp.arange(ckv_all.shape[1])
    kv = (ckv_all @ w_kv_b).reshape(B, ckv_all.shape[1], MLA_HEADS, MLA_NOPE + MLA_V)
    k_nope, v = kv[..., :MLA_NOPE], kv[..., MLA_NOPE:]
    o = chunk_causal_attention(q_nope, q_pe, k_nope, kpe_all, v, pos, k_pos)
    return o.reshape(B, S, MLA_DIM), ckv, kpe


def rwkv7_mix(p, shift_prev, wkv_prev, mu, w0, w_up, a0, a_up, g_up, k_k, k_a, r_k, lnx_g, lnx_b):
    B, S, _ = p.shape
    f32 = jnp.float32
    prev = jnp.concatenate([shift_prev[:, None, :].astype(p.dtype), p[:, :-1]], axis=1)
    ps = p + (prev - p) * mu
    r, k, v, wd, ad, gd = split_last(ps, RWKV_SIZES)
    w = -jax.nn.softplus(-(w0 + jnp.tanh(wd) @ w_up)) - 0.5
    decay = jnp.exp(-jnp.exp(w.astype(f32)))
    a = jax.nn.sigmoid(a0 + ad @ a_up)
    g = jax.nn.sigmoid(gd) @ g_up
    heads = lambda t: t.reshape(B, S, RWKV_HEADS, RWKV_HEAD).astype(f32)
    kk = heads(k * k_k)
    kk = kk * lax.rsqrt(jnp.sum(kk * kk, -1, keepdims=True) + 1e-12)
    k = k * (1.0 + (a - 1.0) * k_a)
    rh, kh, vh, ah, wh = heads(r), heads(k), heads(v), heads(a), heads(decay)
    tm = lambda t: jnp.moveaxis(t, 1, 0)

    def step(st, inp):
        r_t, w_t, k_t, v_t, kk_t, a_t = inp
        sa = jnp.einsum('bhvk,bhk->bhv', st, -kk_t)
        st = (st * w_t[:, :, None, :] + sa[..., None] * (kk_t * a_t)[:, :, None, :]
              + v_t[..., None] * k_t[:, :, None, :])
        return st, jnp.einsum('bhvk,bhk->bhv', st, r_t)

    wkv_new, o = lax.scan(step, wkv_prev.astype(f32), (tm(rh), tm(wh), tm(kh), tm(vh), tm(kk), tm(ah)))
    o = jnp.moveaxis(o, 0, 1)
    mean = jnp.mean(o, -1, keepdims=True)
    var = jnp.mean(jnp.square(o - mean), -1, keepdims=True)
    o = ((o - mean) * lax.rsqrt(var + RWKV_LN_EPS)).reshape(B, S, RWKV_DIM) * lnx_g + lnx_b
    bonus = jnp.sum(rh * kh * r_k, -1, keepdims=True) * vh
    o = o + bonus.reshape(B, S, RWKV_DIM)
    return (o * g).astype(p.dtype), p[:, -1], wkv_new.astype(p.dtype)


def _complex_affine_combine(e1, e2):
    a1r, a1i, b1r, b1i = e1
    a2r, a2i, b2r, b2i = e2
    return (a2r * a1r - a2i * a1i, a2r * a1i + a2i * a1r,
            a2r * b1r - a2i * b1i + b2r, a2r * b1i + a2i * b1r + b2i)


def s5_mix(u, h0_re, h0_im, a_re, a_im, b_re, b_im, c_re, c_im, d, log_dt, glu_w, glu_b):
    B, S, _ = u.shape
    f32 = jnp.float32
    ug = u.reshape(B, S, SSM_GROUPS, SSM_GROUP).astype(f32)
    lr, li = a_re.astype(f32), a_im.astype(f32)
    dt = jnp.exp(log_dt.astype(f32))[:, None]
    mag = jnp.exp(lr * dt)
    ab_re, ab_im = mag * jnp.cos(li * dt), mag * jnp.sin(li * dt)
    den = lr * lr + li * li
    f_re = ((ab_re - 1.0) * lr + ab_im * li) / den
    f_im = (ab_im * lr - (ab_re - 1.0) * li) / den
    br, bi = b_re.astype(f32), b_im.astype(f32)
    bb_re = f_re[..., None] * br - f_im[..., None] * bi
    bb_im = f_re[..., None] * bi + f_im[..., None] * br
    bu_re = jnp.einsum('bsgi,gni->bsgn', ug, bb_re)
    bu_im = jnp.einsum('bsgi,gni->bsgn', ug, bb_im)
    h0r, h0i = h0_re.astype(f32), h0_im.astype(f32)
    bu_re = bu_re.at[:, 0].add(ab_re * h0r - ab_im * h0i)
    bu_im = bu_im.at[:, 0].add(ab_re * h0i + ab_im * h0r)
    a_seq_re = jnp.broadcast_to(ab_re, (1, S) + ab_re.shape)
    a_seq_im = jnp.broadcast_to(ab_im, (1, S) + ab_im.shape)
    _, _, h_re, h_im = lax.associative_scan(_complex_affine_combine, (a_seq_re, a_seq_im, bu_re, bu_im), axis=1)
    y = (jnp.einsum('gin,bsgn->bsgi', c_re.astype(f32), h_re)
         - jnp.einsum('gin,bsgn->bsgi', c_im.astype(f32), h_im) + d.astype(f32) * ug)
    y = jax.nn.gelu(y.reshape(B, S, SSM_DIM)).astype(u.dtype)
    z_a, z_b = split_last(y @ glu_w + glu_b, (SSM_DIM, SSM_DIM))
    return z_a * jax.nn.sigmoid(z_b), h_re[:, -1].astype(u.dtype), h_im[:, -1].astype(u.dtype)


def moe_ffn(x, router_w, router_b, w_gu, b_gu, w_down, b_down):
    B, S, D = x.shape
    t = x.reshape(B * S, D)
    logits = (t @ router_w + router_b).astype(jnp.float32)
    top_val, top_idx = lax.top_k(logits, TOP_K)
    gates = jax.nn.softmax(top_val, axis=-1)
    comb = jnp.sum(jax.nn.one_hot(top_idx, N_EXPERTS, dtype=jnp.float32) * gates[..., None], axis=1)
    out = jnp.zeros((B * S, D), jnp.float32)
    for e in range(N_EXPERTS):
        h = t @ w_gu[e] + b_gu[e]
        h_glu = jnp.minimum(h[:, 0::2], SWIGLU_LIMIT)
        h_lin = jnp.clip(h[:, 1::2], -SWIGLU_LIMIT, SWIGLU_LIMIT)
        o = (h_glu * jax.nn.sigmoid(SWIGLU_ALPHA * h_glu) * (h_lin + 1.0)) @ w_down[e] + b_down[e]
        out = out + comb[:, e:e + 1] * o.astype(jnp.float32)
    return out.astype(x.dtype).reshape(B, S, D)


def trunk_layer(x, l, P, past):
    B, S, _ = x.shape
    if past is None:
        ckv_p = kpe_p = None
        shift_p = jnp.zeros((B, RWKV_PROJ), x.dtype)
        wkv_p = jnp.zeros((B, RWKV_HEADS, RWKV_HEAD, RWKV_HEAD), jnp.float32)
        sre_p = jnp.zeros((B, SSM_GROUPS, SSM_STATE), jnp.float32)
        sim_p = jnp.zeros((B, SSM_GROUPS, SSM_STATE), jnp.float32)
        start = 0
    else:
        ckv_p, kpe_p, shift_p, wkv_p, sre_p, sim_p = past
        start = ckv_p.shape[1]
    pos = start + jnp.arange(S)
    h = x @ P['w_in'][l]
    q_a, kv_a, rw, su, g_a, g_b, g_c = split_last(h, IN_SIZES)
    ya, ckv_n, kpe_n = mla_mix(q_a, kv_a, pos, ckv_p, kpe_p, P['mla_q_a_norm'][l], P['mla_w_q_b'][l],
                               P['mla_kv_a_norm'][l], P['mla_w_kv_b'][l])
    yb, shift_n, wkv_n = rwkv7_mix(rw, shift_p, wkv_p, P['rwkv_mu'][l], P['rwkv_w0'][l], P['rwkv_w_up'][l],
                                   P['rwkv_a0'][l], P['rwkv_a_up'][l], P['rwkv_g_up'][l], P['rwkv_k_k'][l],
                                   P['rwkv_k_a'][l], P['rwkv_r_k'][l], P['rwkv_lnx_g'][l], P['rwkv_lnx_b'][l])
    yc, sre_n, sim_n = s5_mix(su, sre_p, sim_p, P['ssm_a_re'][l], P['ssm_a_im'][l], P['ssm_b_re'][l],
                              P['ssm_b_im'][l], P['ssm_c_re'][l], P['ssm_c_im'][l], P['ssm_d'][l],
                              P['ssm_log_dt'][l], P['ssm_glu_w'][l], P['ssm_glu_b'][l])
    merged = (jax.nn.sigmoid(g_a) * (ya @ P['w_br_a'][l]) + jax.nn.sigmoid(g_b) * (yb @ P['w_br_b'][l])
              + jax.nn.sigmoid(g_c) * (yc @ P['w_br_c'][l]))
    x = layer_norm(DN_ALPHA * x + merged @ P['w_out'][l], P['ln1_g'][l], P['ln1_b'][l])
    ffn = moe_ffn(x, P['router_w'][l], P['router_b'][l], P['exp_w_gu'][l], P['exp_b_gu'][l],
                  P['exp_w_down'][l], P['exp_b_down'][l])
    x = layer_norm(DN_ALPHA * x + ffn, P['ln2_g'][l], P['ln2_b'][l])
    return x, (ckv_n, kpe_n, shift_n, wkv_n, sre_n, sim_n)


def run_trunk(x, P, caches):
    new = []
    for l in range(DEPTH):
        past = None if caches is None else tuple(c[l] for c in caches)
        x, st = trunk_layer(x, l, P, past)
        new.append(st)
    ckv = jnp.stack([s[0] for s in new])
    kpe = jnp.stack([s[1] for s in new])
    shift = jnp.stack([s[2] for s in new])
    wkv = jnp.stack([s[3] for s in new])
    sre = jnp.stack([s[4] for s in new])
    sim = jnp.stack([s[5] for s in new])
    return x, ckv, kpe, shift, wkv, sre, sim


def setup_inputs(seed: int = 0) -> dict:
    key = jax.random.key(seed)
    ks = iter(jax.random.split(key, 64))
    f32 = jnp.float32

    def nrm(shape, scale=1.0):
        return jax.random.normal(next(ks), shape, f32) * scale

    def unif(shape, lo, hi):
        return jax.random.uniform(next(ks), shape, f32, minval=lo, maxval=hi)

    L = DEPTH
    n_idx = jnp.arange(SSM_STATE, dtype=f32)
    return {
        'x_prompt': nrm((BATCH, SEQ, D_MODEL)),
        'x_sample': nrm((DEC_BATCH, DEC_SEQ, D_MODEL)),
        'cache_mla_ckv': nrm((L, DEC_BATCH, PAST_LEN, MLA_KV_RANK)),
        'cache_mla_kpe': nrm((L, DEC_BATCH, PAST_LEN, MLA_ROPE)),
        'state_rwkv_shift': nrm((L, DEC_BATCH, RWKV_PROJ)),
        'state_rwkv_wkv': nrm((L, DEC_BATCH, RWKV_HEADS, RWKV_HEAD, RWKV_HEAD), 0.1),
        'state_ssm_re': nrm((L, DEC_BATCH, SSM_GROUPS, SSM_STATE), 0.1),
        'state_ssm_im': nrm((L, DEC_BATCH, SSM_GROUPS, SSM_STATE), 0.1),
        'w_in': nrm((L, D_MODEL, IN_DIM), D_MODEL ** -0.5),
        'mla_q_a_norm': 1.0 + nrm((L, MLA_Q_RANK), 0.01),
        'mla_w_q_b': nrm((L, MLA_Q_RANK, MLA_HEADS * (MLA_NOPE + MLA_ROPE)), MLA_Q_RANK ** -0.5),
        'mla_kv_a_norm': 1.0 + nrm((L, MLA_KV_RANK), 0.01),
        'mla_w_kv_b': nrm((L, MLA_KV_RANK, MLA_HEADS * (MLA_NOPE + MLA_V)), MLA_KV_RANK ** -0.5),
        'rwkv_mu': unif((L, RWKV_PROJ), 0.0, 1.0),
        'rwkv_w0': unif((L, RWKV_DIM), -6.0, 0.0),
        'rwkv_w_up': nrm((L, RWKV_W_RANK, RWKV_DIM), 0.1),
        'rwkv_a0': nrm((L, RWKV_DIM), 0.1),
        'rwkv_a_up': nrm((L, RWKV_A_RANK, RWKV_DIM), 0.1),
        'rwkv_g_up': nrm((L, RWKV_G_RANK, RWKV_DIM), RWKV_G_RANK ** -0.5),
        'rwkv_k_k': 0.85 + nrm((L, RWKV_DIM), 0.01),
        'rwkv_k_a': 1.0 + nrm((L, RWKV_DIM), 0.01),
        'rwkv_r_k': nrm((L, RWKV_HEADS, RWKV_HEAD), 0.1),
        'rwkv_lnx_g': 1.0 + nrm((L, RWKV_DIM), 0.01),
        'rwkv_lnx_b': nrm((L, RWKV_DIM), 0.01),
        'ssm_a_re': -0.5 + nrm((L, SSM_GROUPS, SSM_STATE), 0.01),
        'ssm_a_im': math.pi * n_idx + nrm((L, SSM_GROUPS, SSM_STATE), 0.01),
        'ssm_b_re': nrm((L, SSM_GROUPS, SSM_STATE, SSM_GROUP), (2 * SSM_GROUP) ** -0.5),
        'ssm_b_im': nrm((L, SSM_GROUPS, SSM_STATE, SSM_GROUP), (2 * SSM_GROUP) ** -0.5),
        'ssm_c_re': nrm((L, SSM_GROUPS, SSM_GROUP, SSM_STATE), SSM_STATE ** -0.5),
        'ssm_c_im': nrm((L, SSM_GROUPS, SSM_GROUP, SSM_STATE), SSM_STATE ** -0.5),
        'ssm_d': nrm((L, SSM_GROUPS, SSM_GROUP)),
        'ssm_log_dt': unif((L, SSM_GROUPS), math.log(DT_MIN), math.log(DT_MAX)),
        'ssm_glu_w': nrm((L, SSM_DIM, 2 * SSM_DIM), SSM_DIM ** -0.5),
        'ssm_glu_b': nrm((L, 2 * SSM_DIM), 0.01),
        'w_br_a': nrm((L, MLA_DIM, D_MODEL), MLA_DIM ** -0.5),
        'w_br_b': nrm((L, RWKV_DIM, D_MODEL), RWKV_DIM ** -0.5),
        'w_br_c': nrm((L, SSM_DIM, D_MODEL), SSM_DIM ** -0.5),
        'w_out': nrm((L, D_MODEL, D_MODEL), D_MODEL ** -0.5 * DN_BETA),
        'ln1_g': 1.0 + nrm((L, D_MODEL), 0.01),
        'ln1_b': nrm((L, D_MODEL), 0.01),
        'router_w': nrm((L, D_MODEL, N_EXPERTS), D_MODEL ** -0.5),
        'router_b': nrm((L, N_EXPERTS), 0.01),
        'exp_w_gu': nrm((L, N_EXPERTS, D_MODEL, 2 * D_FF), D_MODEL ** -0.5),
        'exp_b_gu': nrm((L, N_EXPERTS, 2 * D_FF), 0.01),
        'exp_w_down': nrm((L, N_EXPERTS, D_FF, D_MODEL), D_FF ** -0.5 * DN_BETA),
        'exp_b_down': nrm((L, N_EXPERTS, D_MODEL), 0.01),
        'ln2_g': 1.0 + nrm((L, D_MODEL), 0.01),
        'ln2_b': nrm((L, D_MODEL), 0.01),
    }


def reference(x_prompt, x_sample, cache_mla_ckv, cache_mla_kpe, state_rwkv_shift, state_rwkv_wkv,
              state_ssm_re, state_ssm_im, w_in, mla_q_a_norm, mla_w_q_b, mla_kv_a_norm, mla_w_kv_b,
              rwkv_mu, rwkv_w0, rwkv_w_up, rwkv_a0, rwkv_a_up, rwkv_g_up, rwkv_k_k, rwkv_k_a, rwkv_r_k,
              rwkv_lnx_g, rwkv_lnx_b, ssm_a_re, ssm_a_im, ssm_b_re, ssm_b_im, ssm_c_re, ssm_c_im, ssm_d,
              ssm_log_dt, ssm_glu_w, ssm_glu_b, w_br_a, w_br_b, w_br_c, w_out, ln1_g, ln1_b,
              router_w, router_b, exp_w_gu, exp_b_gu, exp_w_down, exp_b_down, ln2_g, ln2_b):
    P = dict(w_in=w_in, mla_q_a_norm=mla_q_a_norm, mla_w_q_b=mla_w_q_b, mla_kv_a_norm=mla_kv_a_norm,
             mla_w_kv_b=mla_w_kv_b, rwkv_mu=rwkv_mu, rwkv_w0=rwkv_w0, rwkv_w_up=rwkv_w_up, rwkv_a0=rwkv_a0,
             rwkv_a_up=rwkv_a_up, rwkv_g_up=rwkv_g_up, rwkv_k_k=rwkv_k_k, rwkv_k_a=rwkv_k_a, rwkv_r_k=rwkv_r_k,
             rwkv_lnx_g=rwkv_lnx_g, rwkv_lnx_b=rwkv_lnx_b, ssm_a_re=ssm_a_re, ssm_a_im=ssm_a_im,
             ssm_b_re=ssm_b_re, ssm_b_im=ssm_b_im, ssm_c_re=ssm_c_re, ssm_c_im=ssm_c_im, ssm_d=ssm_d,
             ssm_log_dt=ssm_log_dt, ssm_glu_w=ssm_glu_w, ssm_glu_b=ssm_glu_b, w_br_a=w_br_a, w_br_b=w_br_b,
             w_br_c=w_br_c, w_out=w_out, ln1_g=ln1_g, ln1_b=ln1_b, router_w=router_w, router_b=router_b,
             exp_w_gu=exp_w_gu, exp_b_gu=exp_b_gu, exp_w_down=exp_w_down, exp_b_down=exp_b_down,
             ln2_g=ln2_g, ln2_b=ln2_b)
    y_prompt, ckv_p, kpe_p, shift_p, wkv_p, sre_p, sim_p = run_trunk(x_prompt, P, None)
    caches = (cache_mla_ckv, cache_mla_kpe, state_rwkv_shift, state_rwkv_wkv, state_ssm_re, state_ssm_im)
    y_sample, ckv_s, kpe_s, shift_s, wkv_s, sre_s, sim_s = run_trunk(x_sample, P, caches)
    return (y_prompt, y_sample, ckv_p, kpe_p, shift_p, wkv_p, sre_p, sim_p,
            ckv_s, kpe_s, shift_s, wkv_s, sre_s, sim_s)
```

```python
import functools
import math

import numpy as np
import jax
import jax.numpy as jnp
from jax import lax
from jax.experimental import pallas as pl
from jax.experimental.pallas import tpu as pltpu

F32 = jnp.float32
BF16 = jnp.bfloat16

D_MODEL = 1024
CHUNK = 64
MLA_HEADS = 8
MLA_NOPE = 64
MLA_ROPE = 32
MLA_V = 64
MLA_Q_RANK = 384
MLA_KV_RANK = 256
ROPE_THETA = 10000.0
HEAD_PAD = 128
RWKV_HEADS = 8
RWKV_HEAD = 64
RWKV_DIM = RWKV_HEADS * RWKV_HEAD
RWKV_PROJ = 3 * RWKV_DIM + 64 + 64 + 128
RWKV_LN_EPS = 64e-5
SSM_DIM = 512
SSM_GROUP = 16
SSM_GROUPS = 32
SSM_STATE = 64
SSM_CHUNKS = 4
SSM_CH = SSM_GROUPS * SSM_STATE
IN_SIZES = (MLA_Q_RANK, MLA_KV_RANK + MLA_ROPE, RWKV_PROJ, SSM_DIM, D_MODEL, D_MODEL, D_MODEL)
N_EXPERTS = 32
TOP_K = 4
D_FF = 512
SWIGLU_LIMIT = 7.0
SWIGLU_ALPHA = 1.702
DEPTH = 2
DN_ALPHA = (2 * DEPTH) ** 0.25
LN_EPS = 1e-5
RMS_EPS = 1e-6
NEG_INF = -1e30
ATT_SCALE = (MLA_NOPE + MLA_ROPE) ** -0.5
LANES = 128
VMEM_LIMIT = 48 * 1024 * 1024


def _cparams(*sem):
    return pltpu.CompilerParams(dimension_semantics=sem, vmem_limit_bytes=VMEM_LIMIT)


def _dot(a, b):
    return jnp.dot(a, b, preferred_element_type=F32)


def _sigmoid(x):
    return 1.0 / (1.0 + jnp.exp(-x))


def _layer_norm(x, g, b):
    mu = jnp.mean(x, -1, keepdims=True)
    xc = x - mu
    var = jnp.mean(xc * xc, -1, keepdims=True)
    return xc * lax.rsqrt(var + LN_EPS) * g + b


def _full(shape):
    n = len(shape)
    return pl.BlockSpec(shape, lambda *_: (0,) * n)


def _mm_kernel(x_ref, w_ref, o_ref):
    o_ref[...] = _dot(x_ref[...].astype(BF16), w_ref[...])


def _matmul(x, w, nb, s, *, time_major_out=False, tm=512, tn=1024):
    K = x.shape[1]
    N = w.shape[1]
    tm = min(tm, s)
    tn = min(tn, N)
    while N % tn:
        tn -= LANES
    nt = s // tm
    if time_major_out:
        assert tn == N
        out_shape = jax.ShapeDtypeStruct((s, nb * N), F32)
        out_spec = pl.BlockSpec((tm, N), lambda b, i, j: (i, b))
    else:
        out_shape = jax.ShapeDtypeStruct((nb * s, N), F32)
        out_spec = pl.BlockSpec((tm, tn), lambda b, i, j: (b * nt + i, j))
    return pl.pallas_call(
        _mm_kernel,
        grid=(nb, nt, N // tn),
        in_specs=[pl.BlockSpec((tm, K), lambda b, i, j: (b * nt + i, 0)),
                  pl.BlockSpec((K, tn), lambda b, i, j: (0, j))],
        out_specs=out_spec,
        out_shape=out_shape,
        compiler_params=_cparams("parallel", "parallel", "arbitrary"),
        name="matmul",
    )(x, w)


def _mla_in_kernel(x_ref, w_ref, qg_ref, kvg_ref, wqb_ref, tc_ref, ts_ref, q_ref, ckv_ref, kpe_ref):
    h = _dot(x_ref[...].astype(BF16), w_ref[...])
    tc = tc_ref[...]
    ts = ts_ref[...]
    qa = h[:, :MLA_Q_RANK]
    qn = qa * lax.rsqrt(jnp.mean(qa * qa, -1, keepdims=True) + RMS_EPS) * qg_ref[...]
    q2 = _dot(qn.astype(BF16), wqb_ref[...])
    hp = MLA_HEADS * HEAD_PAD
    for hd in range(MLA_HEADS):
        lo = hd * HEAD_PAD
        q0 = q2[:, lo:lo + HEAD_PAD]
        q1 = q2[:, hp + lo:hp + lo + HEAD_PAD]
        q_ref[:, lo:lo + HEAD_PAD] = ((q0 * tc + q1 * ts) * ATT_SCALE).astype(BF16)
    c0 = MLA_Q_RANK
    ckv = h[:, c0:c0 + MLA_KV_RANK]
    ckv_ref[...] = ckv * lax.rsqrt(jnp.mean(ckv * ckv, -1, keepdims=True) + RMS_EPS) * kvg_ref[...]
    c1 = c0 + MLA_KV_RANK
    kpe_ref[...] = h[:, c1:c1 + LANES] * tc + h[:, c1 + LANES:c1 + 2 * LANES] * ts


def _mla_in(x, lw, tc, ts, *, tm=512):
    T = x.shape[0]
    tm = min(tm, T)
    hp = MLA_HEADS * HEAD_PAD
    nw = lw["w_mla"].shape[1]
    return pl.pallas_call(
        _mla_in_kernel,
        grid=(T // tm,),
        in_specs=[pl.BlockSpec((tm, D_MODEL), lambda i: (i, 0)),
                  _full((D_MODEL, nw)), _full((1, MLA_Q_RANK)), _full((1, MLA_KV_RANK)),
                  _full((MLA_Q_RANK, 2 * hp)),
                  pl.BlockSpec((tm, LANES), lambda i: (i, 0)),
                  pl.BlockSpec((tm, LANES), lambda i: (i, 0))],
        out_specs=[pl.BlockSpec((tm, hp), lambda i: (i, 0)),
                   pl.BlockSpec((tm, MLA_KV_RANK), lambda i: (i, 0)),
                   pl.BlockSpec((tm, LANES), lambda i: (i, 0))],
        out_shape=[jax.ShapeDtypeStruct((T, hp), BF16),
                   jax.ShapeDtypeStruct((T, MLA_KV_RANK), F32),
                   jax.ShapeDtypeStruct((T, LANES), F32)],
        compiler_params=_cparams("parallel"),
        name="mla_in",
    )(x, lw["w_mla"], lw["q_norm"], lw["kv_norm"], lw["w_qb"], tc, ts)


def _kv_expand_kernel(ckv_ref, kpe_ref, wk_ref, wv_ref, k_ref, v_ref):
    c = ckv_ref[...].astype(BF16)
    k = _dot(c, wk_ref[...])
    kpe = kpe_ref[...]
    for hd in range(MLA_HEADS):
        lo = hd * HEAD_PAD
        k_ref[:, lo:lo + HEAD_PAD] = (k[:, lo:lo + HEAD_PAD] + kpe).astype(BF16)
    v_ref[...] = _dot(c, wv_ref[...]).astype(BF16)


def _kv_expand(ckv, kpe, lw, *, tm=512):
    T = ckv.shape[0]
    tm = min(tm, T)
    while T % tm:
        tm //= 2
    hp = MLA_HEADS * HEAD_PAD
    hv = MLA_HEADS * MLA_V
    return pl.pallas_call(
        _kv_expand_kernel,
        grid=(T // tm,),
        in_specs=[pl.BlockSpec((tm, MLA_KV_RANK), lambda i: (i, 0)),
                  pl.BlockSpec((tm, LANES), lambda i: (i, 0)),
                  _full((MLA_KV_RANK, hp)), _full((MLA_KV_RANK, hv))],
        out_specs=[pl.BlockSpec((tm, hp), lambda i: (i, 0)),
                   pl.BlockSpec((tm, hv), lambda i: (i, 0))],
        out_shape=[jax.ShapeDtypeStruct((T, hp), BF16), jax.ShapeDtypeStruct((T, hv), BF16)],
        compiler_params=_cparams("parallel"),
        name="kv_expand",
    )(ckv, kpe, lw["w_k"], lw["w_v"])


def _attn_kernel(q_ref, k_ref, v_ref, o_ref, *, tq, tk, q_start, sk):
    i = pl.program_id(2)
    q_lo = q_start + i * tq
    k_end = jnp.minimum(((q_lo + tq - 1) // CHUNK + 1) * CHUNK, sk)
    nkb = (k_end + tk - 1) // tk
    q_chunk = (q_lo + lax.broadcasted_iota(jnp.int32, (tq, tk), 0)) // CHUNK
    k_iota = lax.broadcasted_iota(jnp.int32, (tq, tk), 1)
    outs = []
    for hh in range(2):
        q = q_ref[:, hh * HEAD_PAD:(hh + 1) * HEAD_PAD]

        def body(kb, carry, q=q, hh=hh):
            m, l, acc = carry
            r0 = pl.multiple_of(kb * tk, tk)
            k = k_ref[pl.ds(r0, tk), hh * HEAD_PAD:(hh + 1) * HEAD_PAD]
            s = lax.dot_general(q, k, (((1,), (1,)), ((), ())), preferred_element_type=F32)
            k_pos = r0 + k_iota
            visible = (k_pos // CHUNK <= q_chunk) & (k_pos < sk)
            s = jnp.where(visible, s, NEG_INF)
            m_new = jnp.maximum(m, jnp.max(s, -1, keepdims=True))
            alpha = jnp.exp(m - m_new)
            p = jnp.exp(s - m_new)
            l_new = alpha * l + jnp.sum(p, -1, keepdims=True)
            acc_new = alpha * acc + _dot(p.astype(BF16), v_ref[pl.ds(r0, tk), :])
            return m_new, l_new, acc_new

        init = (jnp.full((tq, 1), NEG_INF, F32), jnp.zeros((tq, 1), F32), jnp.zeros((tq, LANES), F32))
        _, l, acc = lax.fori_loop(0, nkb, body, init)
        outs.append(acc / l)
    lane = lax.broadcasted_iota(jnp.int32, (tq, LANES), 1)
    o_ref[...] = jnp.where(lane < MLA_V, outs[0], outs[1])


def _attention(q, k, v, nb, sq, skp, sk, q_start, *, tq=256, tk=256):
    tq = min(tq, sq)
    nq = sq // tq
    kern = functools.partial(_attn_kernel, tq=tq, tk=tk, q_start=q_start, sk=sk)
    return pl.pallas_call(
        kern,
        grid=(nb, MLA_HEADS // 2, nq),
        in_specs=[pl.BlockSpec((tq, 2 * HEAD_PAD), lambda b, j, i: (b * nq + i, j)),
                  pl.BlockSpec((skp, 2 * HEAD_PAD), lambda b, j, i: (b, j)),
                  pl.BlockSpec((skp, 2 * MLA_V), lambda b, j, i: (b, j))],
        out_specs=pl.BlockSpec((tq, 2 * MLA_V), lambda b, j, i: (b * nq + i, j)),
        out_shape=jax.ShapeDtypeStruct((nb * sq, MLA_HEADS * MLA_V), F32),
        compiler_params=_cparams("parallel", "parallel", "arbitrary"),
        name="attention",
    )(q, k, v)


def _head_sum(z, ones_bd):
    hi = z.astype(BF16)
    lo = (z - hi.astype(F32)).astype(BF16)
    return _dot(hi, ones_bd) + _dot(lo, ones_bd)


def _rwkv_pre_kernel(p_ref, prev_ref, shift_ref, mu_ref, w0_ref, wup_ref, a0_ref, aup_ref, gup_ref,
                     kk_ref, ka_ref, rk_ref, ones_ref,
                     r_out, w_out, k_out, v_out, kk_out, a_out, g_out, bonus_out, *, tiles_per_seq):
    i = pl.program_id(0)
    p = p_ref[...]
    tm = p.shape[0]
    first = (i % tiles_per_seq) == 0
    prev_row = jnp.where(first, shift_ref[0], prev_ref[7:8, :])
    row = lax.broadcasted_iota(jnp.int32, p.shape, 0)
    prev = jnp.where(row == 0, prev_row, pltpu.roll(p, 1, 0))
    ps = p + (prev - p) * mu_ref[...]
    d = RWKV_DIM
    r = ps[:, 0:d]
    k = ps[:, d:2 * d]
    v = ps[:, 2 * d:3 * d]
    wa = ps[:, 3 * d:3 * d + LANES]
    gd = ps[:, 3 * d + LANES:3 * d + 2 * LANES]
    ones_bd = ones_ref[...]
    wlin = w0_ref[...] + _dot(jnp.tanh(wa).astype(BF16), wup_ref[...])
    z = -wlin
    w = -(jnp.maximum(z, 0.0) + jnp.log1p(jnp.exp(-jnp.abs(z)))) - 0.5
    w_out[...] = jnp.exp(-jnp.exp(w))
    a = _sigmoid(a0_ref[...] + _dot(wa.astype(BF16), aup_ref[...]))
    g_out[...] = _dot(_sigmoid(gd).astype(BF16), gup_ref[...])
    kk = k * kk_ref[...]
    kk_out[...] = kk * lax.rsqrt(_head_sum(kk * kk, ones_bd) + 1e-12)
    kn = k * (1.0 + (a - 1.0) * ka_ref[...])
    r_out[...] = r
    k_out[...] = kn
    v_out[...] = v
    a_out[...] = a
    bonus_out[...] = _head_sum(r * kn * rk_ref[...], ones_bd) * v


def _rwkv_pre(rw, shift_prev, lw, nb, s, *, tm=256):
    T = nb * s
    tm = min(tm, s)
    tps = s // tm
    d = RWKV_DIM
    vec = lambda n: _full((1, n))
    out = jax.ShapeDtypeStruct((T, d), F32)
    tile = pl.BlockSpec((tm, d), lambda i: (i, 0))
    kern = functools.partial(_rwkv_pre_kernel, tiles_per_seq=tps)
    return pl.pallas_call(
        kern,
        grid=(T // tm,),
        in_specs=[pl.BlockSpec((tm, RWKV_PROJ), lambda i: (i, 0)),
                  pl.BlockSpec((8, RWKV_PROJ), lambda i: (jnp.maximum(i * (tm // 8) - 1, 0), 0)),
                  pl.BlockSpec((1, 1, RWKV_PROJ), lambda i: (i // tps, 0, 0)),
                  vec(RWKV_PROJ), vec(d), _full((LANES, d)), vec(d), _full((LANES, d)),
                  _full((LANES, d)), vec(d), vec(d), vec(d), _full((d, d))],
        out_specs=[tile] * 8,
        out_shape=[out] * 8,
        compiler_params=_cparams("parallel"),
        name="rwkv_pre",
    )(rw, rw, shift_prev.reshape(nb, 1, RWKV_PROJ), lw["mu"], lw["w0"], lw["w_up"], lw["a0"], lw["a_up"],
      lw["g_up"], lw["k_k"], lw["k_a"], lw["r_k"], lw["ones_bd"])


def _rwkv_scan_kernel(r_ref, w_ref, k_ref, v_ref, kk_ref, a_ref, s0_ref, o_ref, sT_ref, st_ref, *, tt):
    n = RWKV_HEAD

    @pl.when(pl.program_id(0) == 0)
    def _():
        st_ref[...] = s0_ref[...]

    def step(t, carry):
        acc = st_ref[0] * kk_ref[t, 0:1, :]
        for q in range(1, n):
            acc = acc + st_ref[q] * kk_ref[t, q:q + 1, :]
        sa = -acc
        vt = v_ref[t]
        o = None
        for q in range(n):
            kkq = kk_ref[t, q:q + 1, :]
            s_new = (st_ref[q] * w_ref[t, q:q + 1, :] + sa * (kkq * a_ref[t, q:q + 1, :])
                     + vt * k_ref[t, q:q + 1, :])
            st_ref[q] = s_new
            term = s_new * r_ref[t, q:q + 1, :]
            o = term if o is None else o + term
        o_ref[t] = o
        return carry

    lax.fori_loop(0, tt, step, 0)

    @pl.when(pl.program_id(0) == pl.num_programs(0) - 1)
    def _():
        sT_ref[...] = st_ref[...]


def _rwkv_scan(seqs, s0, s, *, tt=16):
    L = s0.shape[-1]
    n = RWKV_HEAD
    tt = min(tt, s)
    blk = pl.BlockSpec((tt, n, L), lambda i: (i, 0, 0))
    kern = functools.partial(_rwkv_scan_kernel, tt=tt)
    return pl.pallas_call(
        kern,
        grid=(s // tt,),
        in_specs=[blk] * 6 + [_full((n, n, L))],
        out_specs=[blk, _full((n, n, L))],
        out_shape=[jax.ShapeDtypeStruct((s, n, L), F32), jax.ShapeDtypeStruct((n, n, L), F32)],
        scratch_shapes=[pltpu.VMEM((n, n, L), F32)],
        compiler_params=_cparams("arbitrary"),
        name="rwkv_scan",
    )(*seqs, s0)


def _s5_params_kernel(are_ref, aim_ref, ldt_ref, bre_ref, bim_ref, abre_ref, abim_ref, bbre_ref, bbim_ref):
    lr = are_ref[...]
    li = aim_ref[...]
    dt = jnp.exp(ldt_ref[...])
    mag = jnp.exp(lr * dt)
    ab_re = mag * jnp.cos(li * dt)
    ab_im = mag * jnp.sin(li * dt)
    den = lr * lr + li * li
    f_re = ((ab_re - 1.0) * lr + ab_im * li) / den
    f_im = (ab_im * lr - (ab_re - 1.0) * li) / den
    abre_ref[...] = ab_re
    abim_ref[...] = ab_im
    for i in range(SSM_GROUP):
        br = bre_ref[i]
        bi = bim_ref[i]
        bbre_ref[i] = f_re * br - f_im * bi
        bbim_ref[i] = f_re * bi + f_im * br


def _s5_params(a_re, a_im, log_dt, b_re, b_im):
    g, n = a_re.shape
    gn = jax.ShapeDtypeStruct((g, n), F32)
    ign = jax.ShapeDtypeStruct((SSM_GROUP, g, n), F32)
    return pl.pallas_call(
        _s5_params_kernel,
        out_shape=[gn, gn, ign, ign],
        name="s5_params",
    )(a_re, a_im, log_dt.reshape(g, 1), jnp.transpose(b_re, (2, 0, 1)), jnp.transpose(b_im, (2, 0, 1)))


def _gelu_tanh(x):
    return 0.5 * x * (1.0 + jnp.tanh(math.sqrt(2.0 / math.pi) * (x + 0.044715 * (x * x * x))))


def _s5_kernel(u_ref, bre_ref, bim_ref, cre_ref, cim_ref, d_ref, are_ref, aim_ref, h0re_ref, h0im_ref,
               gw_ref, gb_ref, y_ref, hTre_ref, hTim_ref, hre_s, him_s, sre_s, sim_s, yy_s, *, tt, nb):
    cw = SSM_CH // SSM_CHUNKS

    @pl.when(pl.program_id(0) == 0)
    def _():
        hre_s[...] = h0re_ref[...]
        him_s[...] = h0im_ref[...]

    for c in range(SSM_CHUNKS):
        uc = u_ref[:, c * LANES:(c + 1) * LANES]
        ub = uc.astype(BF16)
        sre_s[...] = _dot(ub, bre_ref[c])
        sim_s[...] = _dot(ub, bim_ref[c])
        a_re = jnp.broadcast_to(are_ref[:, c * cw:(c + 1) * cw], (nb, cw))
        a_im = jnp.broadcast_to(aim_ref[:, c * cw:(c + 1) * cw], (nb, cw))

        def step(t, carry, a_re=a_re, a_im=a_im):
            hr, hi = carry
            rows = pl.ds(pl.multiple_of(t * nb, nb), nb)
            nr = a_re * hr - a_im * hi + sre_s[rows, :]
            ni = a_re * hi + a_im * hr + sim_s[rows, :]
            sre_s[rows, :] = nr
            sim_s[rows, :] = ni
            return nr, ni

        hr, hi = lax.fori_loop(0, tt, step, (hre_s[:, c * cw:(c + 1) * cw], him_s[:, c * cw:(c + 1) * cw]))
        hre_s[:, c * cw:(c + 1) * cw] = hr
        him_s[:, c * cw:(c + 1) * cw] = hi
        yc = _dot(sre_s[...].astype(BF16), cre_ref[c]) - _dot(sim_s[...].astype(BF16), cim_ref[c])
        yy_s[:, c * LANES:(c + 1) * LANES] = yc + d_ref[:, c * LANES:(c + 1) * LANES] * uc

    y = _gelu_tanh(yy_s[...])
    z = _dot(y.astype(BF16), gw_ref[...]) + gb_ref[...]
    y_ref[...] = z[:, :SSM_DIM] * _sigmoid(z[:, SSM_DIM:])

    @pl.when(pl.program_id(0) == pl.num_programs(0) - 1)
    def _():
        hTre_ref[...] = hre_s[...]
        hTim_ref[...] = him_s[...]


def _s5(u_tm, h0_re, h0_im, lw, nb, s, *, rows=512):
    tt = max(min(rows // nb, s), 1)
    R = tt * nb
    cw = SSM_CH // SSM_CHUNKS
    kern = functools.partial(_s5_kernel, tt=tt, nb=nb)
    return pl.pallas_call(
        kern,
        grid=(s // tt,),
        in_specs=[pl.BlockSpec((R, SSM_DIM), lambda i: (i, 0)),
                  _full((SSM_CHUNKS, LANES, cw)), _full((SSM_CHUNKS, LANES, cw)),
                  _full((SSM_CHUNKS, cw, LANES)), _full((SSM_CHUNKS, cw, LANES)),
                  _full((1, SSM_DIM)), _full((1, SSM_CH)), _full((1, SSM_CH)),
                  _full((nb, SSM_CH)), _full((nb, SSM_CH)),
                  _full((SSM_DIM, 2 * SSM_DIM)), _full((1, 2 * SSM_DIM))],
        out_specs=[pl.BlockSpec((R, SSM_DIM), lambda i: (i, 0)), _full((nb, SSM_CH)), _full((nb, SSM_CH))],
        out_shape=[jax.ShapeDtypeStruct((s * nb, SSM_DIM), F32),
                   jax.ShapeDtypeStruct((nb, SSM_CH), F32), jax.ShapeDtypeStruct((nb, SSM_CH), F32)],
        scratch_shapes=[pltpu.VMEM((nb, SSM_CH), F32), pltpu.VMEM((nb, SSM_CH), F32),
                        pltpu.VMEM((R, cw), F32), pltpu.VMEM((R, cw), F32), pltpu.VMEM((R, SSM_DIM), F32)],
        compiler_params=_cparams("arbitrary"),
        name="s5",
    )(u_tm, lw["s5_bre"], lw["s5_bim"], lw["s5_cre"], lw["s5_cim"], lw["s5_d"], lw["s5_are"], lw["s5_aim"],
      h0_re, h0_im, lw["glu_w"], lw["glu_b"])


def _merge_kernel(x_ref, ya_ref, o_ref, bonus_ref, g_ref, yc_ref, gates_ref, lng_ref, lnb_ref, ones_ref,
                  wa_ref, wb_ref, wc_ref, wo_ref, ln1g_ref, ln1b_ref, out_ref):
    ones_bd = ones_ref[...]
    o = o_ref[...]
    inv_n = 1.0 / RWKV_HEAD
    mean = _head_sum(o, ones_bd) * inv_n
    oc = o - mean
    var = _head_sum(oc * oc, ones_bd) * inv_n
    yb = (oc * lax.rsqrt(var + RWKV_LN_EPS) * lng_ref[...] + lnb_ref[...] + bonus_ref[...]) * g_ref[...]
    d = D_MODEL
    ga = _sigmoid(gates_ref[:, 0:d])
    gb = _sigmoid(gates_ref[:, d:2 * d])
    gc = _sigmoid(gates_ref[:, 2 * d:3 * d])
    merged = (ga * _dot(ya_ref[...].astype(BF16), wa_ref[...]) + gb * _dot(yb.astype(BF16), wb_ref[...])
              + gc * _dot(yc_ref[...].astype(BF16), wc_ref[...]))
    y = DN_ALPHA * x_ref[...] + _dot(merged.astype(BF16), wo_ref[...])
    out_ref[...] = _layer_norm(y, ln1g_ref[...], ln1b_ref[...])


def _merge(x, ya, o, bonus, g, yc_tm, gates, lw, nb, s, *, tm=256):
    tm = min(tm, s)
    nt = s // tm
    d = D_MODEL
    h = RWKV_DIM
    row = lambda w: pl.BlockSpec((tm, w), lambda b, i: (b * nt + i, 0))
    vec = lambda n: _full((1, n))
    return pl.pallas_call(
        _merge_kernel,
        grid=(nb, nt),
        in_specs=[row(d), row(h), row(h), row(h), row(h),
                  pl.BlockSpec((tm, SSM_DIM), lambda b, i: (i, b)),
                  row(3 * d), vec(h), vec(h), _full((h, h)),
                  _full((h, d)), _full((h, d)), _full((h, d)), _full((d, d)), vec(d), vec(d)],
        out_specs=row(d),
        out_shape=jax.ShapeDtypeStruct((nb * s, d), F32),
        compiler_params=_cparams("parallel", "parallel"),
        name="merge",
    )(x, ya, o, bonus, g, yc_tm, gates, lw["lnx_g"], lw["lnx_b"], lw["ones_bd"],
      lw["w_br_a"], lw["w_br_b"], lw["w_br_c"], lw["w_out"], lw["ln1_g"], lw["ln1_b"])


def _router_kernel(x_ref, whi_ref, wlo_ref, b_ref, comb_ref):
    x = x_ref[...]
    xh = x.astype(BF16)
    xl = (x - xh.astype(F32)).astype(BF16)
    logits = _dot(xh, whi_ref[...]) + _dot(xl, whi_ref[...]) + _dot(xh, wlo_ref[...]) + b_ref[...]
    lane = lax.broadcasted_iota(jnp.int32, logits.shape, 1).astype(F32)
    vals, sels = [], []
    for _ in range(TOP_K):
        m = jnp.max(logits, -1, keepdims=True)
        idx = jnp.min(jnp.where(logits == m, lane, float(LANES)), -1, keepdims=True)
        sel = lane == idx
        vals.append(m)
        sels.append(sel)
        logits = jnp.where(sel, -3e38, logits)
    es = [jnp.exp(v - vals[0]) for v in vals]
    den = es[0] + es[1] + es[2] + es[3]
    comb = jnp.zeros_like(logits)
    for e, sel in zip(es, sels):
        comb = comb + jnp.where(sel, e / den, 0.0)
    comb_ref[...] = comb


def _router(x, lw, *, tm=512):
    T = x.shape[0]
    tm = min(tm, T)
    return pl.pallas_call(
        _router_kernel,
        grid=(T // tm,),
        in_specs=[pl.BlockSpec((tm, D_MODEL), lambda i: (i, 0)),
                  _full((D_MODEL, LANES)), _full((D_MODEL, LANES)), _full((1, LANES))],
        out_specs=pl.BlockSpec((tm, LANES), lambda i: (i, 0)),
        out_shape=jax.ShapeDtypeStruct((T, LANES), F32),
        compiler_params=_cparams("parallel"),
        name="router",
    )(x, lw["router_hi"], lw["router_lo"], lw["router_b"])


def _moe_kernel(x_ref, comb_ref, wgu_ref, bgu_ref, wd_ref, bd_ref, ln2g_ref, ln2b_ref, out_ref, xb_s, acc_s):
    e = pl.program_id(1)

    @pl.when(e == 0)
    def _():
        xb_s[...] = x_ref[...].astype(BF16)
        acc_s[...] = jnp.zeros_like(acc_s)

    h = _dot(xb_s[...], wgu_ref[0]) + bgu_ref[0]
    hg = jnp.minimum(h[:, :D_FF], SWIGLU_LIMIT)
    hl = jnp.clip(h[:, D_FF:], -SWIGLU_LIMIT, SWIGLU_LIMIT)
    act = hg * _sigmoid(SWIGLU_ALPHA * hg) * (hl + 1.0)
    o = _dot(act.astype(BF16), wd_ref[0]) + bd_ref[0]
    comb = comb_ref[...]
    lane = lax.broadcasted_iota(jnp.int32, comb.shape, 1)
    ce = jnp.sum(jnp.where(lane == e, comb, 0.0), -1, keepdims=True)
    acc_s[...] += ce * o

    @pl.when(e == pl.num_programs(1) - 1)
    def _():
        out_ref[...] = _layer_norm(DN_ALPHA * x_ref[...] + acc_s[...], ln2g_ref[...], ln2b_ref[...])


def _moe(x, comb, lw, *, tm=1024):
    T = x.shape[0]
    tm = min(tm, T)
    d = D_MODEL
    return pl.pallas_call(
        _moe_kernel,
        grid=(T // tm, N_EXPERTS),
        in_specs=[pl.BlockSpec((tm, d), lambda i, e: (i, 0)),
                  pl.BlockSpec((tm, LANES), lambda i, e: (i, 0)),
                  pl.BlockSpec((1, d, 2 * D_FF), lambda i, e: (e, 0, 0)),
                  pl.BlockSpec((1, 1, 2 * D_FF), lambda i, e: (e, 0, 0)),
                  pl.BlockSpec((1, D_FF, d), lambda i, e: (e, 0, 0)),
                  pl.BlockSpec((1, 1, d), lambda i, e: (e, 0, 0)),
                  _full((1, d)), _full((1, d))],
        out_specs=pl.BlockSpec((tm, d), lambda i, e: (i, 0)),
        out_shape=jax.ShapeDtypeStruct((T, d), F32),
        scratch_shapes=[pltpu.VMEM((tm, d), BF16), pltpu.VMEM((tm, d), F32)],
        compiler_params=_cparams("parallel", "arbitrary"),
        name="moe",
    )(x, comb, lw["w_gu"], lw["b_gu"], lw["w_down"], lw["b_down"], lw["ln2_g"], lw["ln2_b"])


def _ones_block_diag():
    idx = np.arange(RWKV_DIM) // RWKV_HEAD
    return jnp.asarray((idx[:, None] == idx[None, :]).astype(np.float32), dtype=BF16)


def _block_diag(x):
    C, G, r, c = x.shape
    eye = jnp.eye(G, dtype=x.dtype)
    return jnp.einsum("cgij,gh->cgihj", x, eye).reshape(C, G * r, G * c)


def _prep_layer(P, l):
    f = lambda name: P[name][l]
    lw = {}
    w_in = f("w_in")
    offs = np.cumsum((0,) + IN_SIZES)
    cols = lambda j: w_in[:, offs[j]:offs[j + 1]]
    w_q, w_kv = cols(0), cols(1)
    w_ckv, w_kpe = w_kv[:, :MLA_KV_RANK], w_kv[:, MLA_KV_RANK:]
    half = MLA_ROPE // 2
    zpad = jnp.zeros((D_MODEL, LANES - MLA_ROPE), F32)
    w_kpe_rot = jnp.concatenate([-w_kpe[:, half:], w_kpe[:, :half]], 1)
    lw["w_mla"] = jnp.concatenate([w_q, w_ckv, w_kpe, zpad, w_kpe_rot, zpad], 1).astype(BF16)
    lw["w_rw"] = cols(2).astype(BF16)
    lw["w_su"] = cols(3).astype(BF16)
    lw["w_gates"] = jnp.concatenate([cols(4), cols(5), cols(6)], 1).astype(BF16)
    lw["q_norm"] = f("mla_q_a_norm").reshape(1, -1)
    lw["kv_norm"] = f("mla_kv_a_norm").reshape(1, -1)
    wqb = f("mla_w_q_b").reshape(MLA_Q_RANK, MLA_HEADS, MLA_NOPE + MLA_ROPE)
    nope, x1, x2 = wqb[..., :MLA_NOPE], wqb[..., MLA_NOPE:MLA_NOPE + half], wqb[..., MLA_NOPE + half:]
    z32 = jnp.zeros_like(wqb[..., :HEAD_PAD - MLA_NOPE - MLA_ROPE])
    plain = jnp.concatenate([x1, x2, nope, z32], -1).reshape(MLA_Q_RANK, -1)
    rot = jnp.concatenate([-x2, x1, jnp.zeros_like(nope), z32], -1).reshape(MLA_Q_RANK, -1)
    lw["w_qb"] = jnp.concatenate([plain, rot], 1).astype(BF16)
    wkvb = f("mla_w_kv_b").reshape(MLA_KV_RANK, MLA_HEADS, MLA_NOPE + MLA_V)
    k_nope, v = wkvb[..., :MLA_NOPE], wkvb[..., MLA_NOPE:]
    zk = jnp.zeros_like(k_nope[..., :MLA_ROPE])
    lw["w_k"] = jnp.concatenate([zk, k_nope, zk], -1).reshape(MLA_KV_RANK, -1).astype(BF16)
    lw["w_v"] = v.reshape(MLA_KV_RANK, -1).astype(BF16)
    row = lambda name: f(name).reshape(1, -1)
    lw["mu"] = row("rwkv_mu")
    lw["w0"] = row("rwkv_w0")
    z64 = jnp.zeros((64, RWKV_DIM), F32)
    lw["w_up"] = jnp.concatenate([f("rwkv_w_up"), z64], 0).astype(BF16)
    lw["a_up"] = jnp.concatenate([z64, f("rwkv_a_up")], 0).astype(BF16)
    lw["a0"] = row("rwkv_a0")
    lw["g_up"] = f("rwkv_g_up").astype(BF16)
    lw["k_k"] = row("rwkv_k_k")
    lw["k_a"] = row("rwkv_k_a")
    lw["r_k"] = row("rwkv_r_k")
    lw["lnx_g"] = row("rwkv_lnx_g")
    lw["lnx_b"] = row("rwkv_lnx_b")
    lw["ones_bd"] = _ones_block_diag()
    ab_re, ab_im, bb_re, bb_im = _s5_params(f("ssm_a_re"), f("ssm_a_im"), f("ssm_log_dt"),
                                            f("ssm_b_re"), f("ssm_b_im"))
    gpc = SSM_GROUPS // SSM_CHUNKS
    chunked = lambda t: t.reshape(SSM_CHUNKS, gpc, t.shape[1], t.shape[2])
    lw["s5_bre"] = _block_diag(chunked(jnp.transpose(bb_re, (1, 0, 2)))).astype(BF16)
    lw["s5_bim"] = _block_diag(chunked(jnp.transpose(bb_im, (1, 0, 2)))).astype(BF16)
    lw["s5_cre"] = _block_diag(chunked(jnp.transpose(f("ssm_c_re"), (0, 2, 1)))).astype(BF16)
    lw["s5_cim"] = _block_diag(chunked(jnp.transpose(f("ssm_c_im"), (0, 2, 1)))).astype(BF16)
    lw["s5_d"] = row("ssm_d")
    lw["s5_are"] = ab_re.reshape(1, -1)
    lw["s5_aim"] = ab_im.reshape(1, -1)
    lw["glu_w"] = f("ssm_glu_w").astype(BF16)
    lw["glu_b"] = row("ssm_glu_b")
    for name in ("w_br_a", "w_br_b", "w_br_c", "w_out"):
        lw[name] = f(name).astype(BF16)
    for name in ("ln1_g", "ln1_b", "ln2_g", "ln2_b"):
        lw[name] = row(name)
    rw_ = jnp.pad(f("router_w"), ((0, 0), (0, LANES - N_EXPERTS)))
    hi = rw_.astype(BF16)
    lw["router_hi"] = hi
    lw["router_lo"] = (rw_ - hi.astype(F32)).astype(BF16)
    lw["router_b"] = jnp.pad(f("router_b"), (0, LANES - N_EXPERTS), constant_values=NEG_INF).reshape(1, -1)
    wgu = f("exp_w_gu")
    lw["w_gu"] = jnp.concatenate([wgu[..., 0::2], wgu[..., 1::2]], -1).astype(BF16)
    bgu = f("exp_b_gu")
    lw["b_gu"] = jnp.concatenate([bgu[..., 0::2], bgu[..., 1::2]], -1).reshape(N_EXPERTS, 1, 2 * D_FF)
    lw["w_down"] = f("exp_w_down").astype(BF16)
    lw["b_down"] = f("exp_b_down").reshape(N_EXPERTS, 1, D_MODEL)
    return lw


def _rope_tables(pos):
    half = MLA_ROPE // 2
    inv = ROPE_THETA ** (-jnp.arange(half, dtype=F32) / half)
    ang = pos.astype(F32)[:, None] * inv
    cos, sin = jnp.cos(ang), jnp.sin(ang)
    n = pos.shape[0]
    tc = jnp.concatenate([cos, cos, jnp.ones((n, MLA_NOPE), F32),
                          jnp.zeros((n, HEAD_PAD - MLA_NOPE - MLA_ROPE), F32)], 1)
    ts = jnp.concatenate([sin, sin, jnp.zeros((n, HEAD_PAD - MLA_ROPE), F32)], 1)
    return tc, ts


def _to_lanes(t, nb, s, lanes):
    t = t.reshape(nb, s, RWKV_HEADS, RWKV_HEAD).transpose(1, 3, 0, 2).reshape(s, RWKV_HEAD, nb * RWKV_HEADS)
    return jnp.pad(t, ((0, 0), (0, 0), (0, lanes - nb * RWKV_HEADS)))


def _layer(x, lw, past, nb, s, tables, att_tk=256):
    T = nb * s
    tc, ts = tables
    if past is None:
        start = 0
        shift_p = jnp.zeros((nb, RWKV_PROJ), F32)
        wkv_p = jnp.zeros((nb, RWKV_HEADS, RWKV_HEAD, RWKV_HEAD), F32)
        sre_p = jnp.zeros((nb, SSM_CH), F32)
        sim_p = jnp.zeros((nb, SSM_CH), F32)
    else:
        ckv_p, kpe_p, shift_p, wkv_p, sre_p, sim_p = past
        start = ckv_p.shape[1]
        sre_p = sre_p.reshape(nb, SSM_CH)
        sim_p = sim_p.reshape(nb, SSM_CH)

    q, ckv, kpe128 = _mla_in(x, lw, tc, ts)
    if past is None:
        sk = s
        ckv_all, kpe_all = ckv, kpe128
    else:
        sk = start + s
        ckv_all = jnp.concatenate([ckv_p, ckv.reshape(nb, s, -1)], 1)
        kpe_new = kpe128.reshape(nb, s, LANES)
        kpe_all = jnp.concatenate([jnp.pad(kpe_p, ((0, 0), (0, 0), (0, LANES - MLA_ROPE))), kpe_new], 1)
    skp = -(-sk // att_tk) * att_tk
    if skp != sk:
        ckv_all = jnp.pad(ckv_all.reshape(nb, sk, -1), ((0, 0), (0, skp - sk), (0, 0)))
        kpe_all = jnp.pad(kpe_all.reshape(nb, sk, -1), ((0, 0), (0, skp - sk), (0, 0)))
    k_pad, v_all = _kv_expand(ckv_all.reshape(nb * skp, -1), kpe_all.reshape(nb * skp, -1), lw)
    ya = _attention(q, k_pad, v_all, nb, s, skp, sk, start, tk=att_tk)

    rw = _matmul(x, lw["w_rw"], nb, s, tn=896)
    r, wdec, kn, v, kk, a, g, bonus = _rwkv_pre(rw, shift_p, lw, nb, s)
    lanes = -(-nb * RWKV_HEADS // LANES) * LANES
    seqs = [_to_lanes(t, nb, s, lanes) for t in (r, wdec, kn, v, kk, a)]
    s0 = jnp.transpose(wkv_p, (3, 2, 0, 1)).reshape(RWKV_HEAD, RWKV_HEAD, nb * RWKV_HEADS)
    s0 = jnp.pad(s0, ((0, 0), (0, 0), (0, lanes - nb * RWKV_HEADS)))
    o_l, sT = _rwkv_scan(seqs, s0, s)
    o = (o_l[:, :, :nb * RWKV_HEADS].reshape(s, RWKV_HEAD, nb, RWKV_HEADS)
         .transpose(2, 0, 3, 1).reshape(T, RWKV_DIM))
    wkv_n = (sT[:, :, :nb * RWKV_HEADS].reshape(RWKV_HEAD, RWKV_HEAD, nb, RWKV_HEADS).transpose(2, 3, 1, 0))
    shift_n = rw.reshape(nb, s, RWKV_PROJ)[:, -1]

    su_tm = _matmul(x, lw["w_su"], nb, s, time_major_out=True).reshape(s * nb, SSM_DIM)
    yc_tm, sre_n, sim_n = _s5(su_tm, sre_p, sim_p, lw, nb, s)
    yc_tm = yc_tm.reshape(s, nb * SSM_DIM)

    gates = _matmul(x, lw["w_gates"], nb, s)
    x1 = _merge(x, ya, o, bonus, g, yc_tm, gates, lw, nb, s)
    comb = _router(x1, lw)
    x2 = _moe(x1, comb, lw)
    new = (ckv.reshape(nb, s, MLA_KV_RANK), kpe128[:, :MLA_ROPE].reshape(nb, s, MLA_ROPE), shift_n, wkv_n,
           sre_n.reshape(nb, SSM_GROUPS, SSM_STATE), sim_n.reshape(nb, SSM_GROUPS, SSM_STATE))
    return x2, new


def _trunk(x3, weights, caches):
    nb, s, d = x3.shape
    start = 0 if caches is None else caches[0].shape[2]
    tc, ts = _rope_tables(start + jnp.arange(s))
    tables = (jnp.tile(tc, (nb, 1)), jnp.tile(ts, (nb, 1)))
    x = x3.reshape(nb * s, d)
    new = []
    for l in range(len(weights)):
        past = None if caches is None else tuple(c[l] for c in caches)
        x, st = _layer(x, weights[l], past, nb, s, tables)
        new.append(st)
    return (x.reshape(nb, s, d),) + tuple(jnp.stack([st[j] for st in new]) for j in range(6))


def kernel(x_prompt, x_sample, cache_mla_ckv, cache_mla_kpe, state_rwkv_shift, state_rwkv_wkv, state_ssm_re, state_ssm_im, w_in, mla_q_a_norm, mla_w_q_b, mla_kv_a_norm, mla_w_kv_b, rwkv_mu, rwkv_w0, rwkv_w_up, rwkv_a0, rwkv_a_up, rwkv_g_up, rwkv_k_k, rwkv_k_a, rwkv_r_k, rwkv_lnx_g, rwkv_lnx_b, ssm_a_re, ssm_a_im, ssm_b_re, ssm_b_im, ssm_c_re, ssm_c_im, ssm_d, ssm_log_dt, ssm_glu_w, ssm_glu_b, w_br_a, w_br_b, w_br_c, w_out, ln1_g, ln1_b, router_w, router_b, exp_w_gu, exp_b_gu, exp_w_down, exp_b_down, ln2_g, ln2_b):
    P = dict(w_in=w_in, mla_q_a_norm=mla_q_a_norm, mla_w_q_b=mla_w_q_b, mla_kv_a_norm=mla_kv_a_norm,
             mla_w_kv_b=mla_w_kv_b, rwkv_mu=rwkv_mu, rwkv_w0=rwkv_w0, rwkv_w_up=rwkv_w_up, rwkv_a0=rwkv_a0,
             rwkv_a_up=rwkv_a_up, rwkv_g_up=rwkv_g_up, rwkv_k_k=rwkv_k_k, rwkv_k_a=rwkv_k_a, rwkv_r_k=rwkv_r_k,
             rwkv_lnx_g=rwkv_lnx_g, rwkv_lnx_b=rwkv_lnx_b, ssm_a_re=ssm_a_re, ssm_a_im=ssm_a_im,
             ssm_b_re=ssm_b_re, ssm_b_im=ssm_b_im, ssm_c_re=ssm_c_re, ssm_c_im=ssm_c_im, ssm_d=ssm_d,
             ssm_log_dt=ssm_log_dt, ssm_glu_w=ssm_glu_w, ssm_glu_b=ssm_glu_b, w_br_a=w_br_a, w_br_b=w_br_b,
             w_br_c=w_br_c, w_out=w_out, ln1_g=ln1_g, ln1_b=ln1_b, router_w=router_w, router_b=router_b,
             exp_w_gu=exp_w_gu, exp_b_gu=exp_b_gu, exp_w_down=exp_w_down, exp_b_down=exp_b_down,
             ln2_g=ln2_g, ln2_b=ln2_b)
    depth = w_in.shape[0]
    weights = [_prep_layer(P, l) for l in range(depth)]
    outs_p = _trunk(x_prompt, weights, None)
    caches = (cache_mla_ckv, cache_mla_kpe, state_rwkv_shift, state_rwkv_wkv, state_ssm_re, state_ssm_im)
    outs_s = _trunk(x_sample, weights, caches)
    return (outs_p[0], outs_s[0]) + outs_p[1:] + outs_s[1:]
```

```python
import functools
import math

import numpy as np
import jax
import jax.numpy as jnp
from jax import lax
from jax.experimental import pallas as pl
from jax.experimental.pallas import tpu as pltpu

F32 = jnp.float32
BF16 = jnp.bfloat16

D_MODEL = 1024
CHUNK = 64
MLA_HEADS = 8
MLA_NOPE = 64
MLA_ROPE = 32
MLA_V = 64
MLA_Q_RANK = 384
MLA_KV_RANK = 256
ROPE_THETA = 10000.0
HEAD_PAD = 128
RWKV_HEADS = 8
RWKV_HEAD = 64
RWKV_DIM = RWKV_HEADS * RWKV_HEAD
RWKV_PROJ = 3 * RWKV_DIM + 64 + 64 + 128
RWKV_LN_EPS = 64e-5
SSM_DIM = 512
SSM_GROUP = 16
SSM_GROUPS = 32
SSM_STATE = 64
SSM_CHUNKS = 4
SSM_CH = SSM_GROUPS * SSM_STATE
IN_SIZES = (MLA_Q_RANK, MLA_KV_RANK + MLA_ROPE, RWKV_PROJ, SSM_DIM, D_MODEL, D_MODEL, D_MODEL)
N_EXPERTS = 32
TOP_K = 4
D_FF = 512
SWIGLU_LIMIT = 7.0
SWIGLU_ALPHA = 1.702
DEPTH = 2
DN_ALPHA = (2 * DEPTH) ** 0.25
LN_EPS = 1e-5
RMS_EPS = 1e-6
NEG_INF = -1e30
ATT_SCALE = (MLA_NOPE + MLA_ROPE) ** -0.5
LANES = 128
VMEM_LIMIT = 48 * 1024 * 1024


def _cparams(*sem):
    return pltpu.CompilerParams(dimension_semantics=sem, vmem_limit_bytes=VMEM_LIMIT)


def _dot(a, b):
    return jnp.dot(a, b, preferred_element_type=F32)


def _sigmoid(x):
    return 1.0 / (1.0 + jnp.exp(-x))


def _layer_norm(x, g, b):
    mu = jnp.mean(x, -1, keepdims=True)
    xc = x - mu
    var = jnp.mean(xc * xc, -1, keepdims=True)
    return xc * lax.rsqrt(var + LN_EPS) * g + b


def _full(shape):
    n = len(shape)
    return pl.BlockSpec(shape, lambda *_: (0,) * n)


def _mm_kernel(x_ref, w_ref, o_ref):
    o_ref[...] = _dot(x_ref[...].astype(BF16), w_ref[...]).astype(o_ref.dtype)


def _matmul(x, w, nb, s, *, time_major_out=False, tm=512, tn=1024, out_dtype=F32):
    K = x.shape[1]
    N = w.shape[1]
    tm = min(tm, s)
    tn = min(tn, N)
    while N % tn:
        tn -= LANES
    nt = s // tm
    if time_major_out:
        assert tn == N
        out_shape = jax.ShapeDtypeStruct((s, nb * N), out_dtype)
        out_spec = pl.BlockSpec((tm, N), lambda b, i, j: (i, b))
    else:
        out_shape = jax.ShapeDtypeStruct((nb * s, N), out_dtype)
        out_spec = pl.BlockSpec((tm, tn), lambda b, i, j: (b * nt + i, j))
    return pl.pallas_call(
        _mm_kernel,
        grid=(nb, nt, N // tn),
        in_specs=[pl.BlockSpec((tm, K), lambda b, i, j: (b * nt + i, 0)),
                  pl.BlockSpec((K, tn), lambda b, i, j: (0, j))],
        out_specs=out_spec,
        out_shape=out_shape,
        compiler_params=_cparams("parallel", "parallel", "arbitrary"),
        name="matmul",
    )(x, w)


def _mla_in_kernel(x_ref, w_ref, qg_ref, kvg_ref, wqb_ref, tc_ref, ts_ref, q_ref, ckv_ref, kpe_ref):
    h = _dot(x_ref[...].astype(BF16), w_ref[...])
    tc = tc_ref[...]
    ts = ts_ref[...]
    qa = h[:, :MLA_Q_RANK]
    qn = qa * lax.rsqrt(jnp.mean(qa * qa, -1, keepdims=True) + RMS_EPS) * qg_ref[...]
    q2 = _dot(qn.astype(BF16), wqb_ref[...])
    hp = MLA_HEADS * HEAD_PAD
    for hd in range(MLA_HEADS):
        lo = hd * HEAD_PAD
        q0 = q2[:, lo:lo + HEAD_PAD]
        q1 = q2[:, hp + lo:hp + lo + HEAD_PAD]
        q_ref[:, lo:lo + HEAD_PAD] = ((q0 * tc + q1 * ts) * ATT_SCALE).astype(BF16)
    c0 = MLA_Q_RANK
    ckv = h[:, c0:c0 + MLA_KV_RANK]
    ckv_ref[...] = ckv * lax.rsqrt(jnp.mean(ckv * ckv, -1, keepdims=True) + RMS_EPS) * kvg_ref[...]
    c1 = c0 + MLA_KV_RANK
    kpe_ref[...] = h[:, c1:c1 + LANES] * tc + h[:, c1 + LANES:c1 + 2 * LANES] * ts


def _mla_in(x, lw, tc, ts, *, tm=512):
    T = x.shape[0]
    tm = min(tm, T)
    hp = MLA_HEADS * HEAD_PAD
    nw = lw["w_mla"].shape[1]
    return pl.pallas_call(
        _mla_in_kernel,
        grid=(T // tm,),
        in_specs=[pl.BlockSpec((tm, D_MODEL), lambda i: (i, 0)),
                  _full((D_MODEL, nw)), _full((1, MLA_Q_RANK)), _full((1, MLA_KV_RANK)),
                  _full((MLA_Q_RANK, 2 * hp)),
                  pl.BlockSpec((tm, LANES), lambda i: (i, 0)),
                  pl.BlockSpec((tm, LANES), lambda i: (i, 0))],
        out_specs=[pl.BlockSpec((tm, hp), lambda i: (i, 0)),
                   pl.BlockSpec((tm, MLA_KV_RANK), lambda i: (i, 0)),
                   pl.BlockSpec((tm, LANES), lambda i: (i, 0))],
        out_shape=[jax.ShapeDtypeStruct((T, hp), BF16),
                   jax.ShapeDtypeStruct((T, MLA_KV_RANK), F32),
                   jax.ShapeDtypeStruct((T, LANES), F32)],
        compiler_params=_cparams("parallel"),
        name="mla_in",
    )(x, lw["w_mla"], lw["q_norm"], lw["kv_norm"], lw["w_qb"], tc, ts)


def _kv_expand_kernel(ckv_ref, kpe_ref, wk_ref, wv_ref, k_ref, v_ref):
    c = ckv_ref[...].astype(BF16)
    k = _dot(c, wk_ref[...])
    kpe = kpe_ref[...]
    for hd in range(MLA_HEADS):
        lo = hd * HEAD_PAD
        k_ref[:, lo:lo + HEAD_PAD] = (k[:, lo:lo + HEAD_PAD] + kpe).astype(BF16)
    v_ref[...] = _dot(c, wv_ref[...]).astype(BF16)


def _kv_expand(ckv, kpe, lw, *, tm=512):
    T = ckv.shape[0]
    tm = min(tm, T)
    while T % tm:
        tm //= 2
    hp = MLA_HEADS * HEAD_PAD
    hv = MLA_HEADS * MLA_V
    return pl.pallas_call(
        _kv_expand_kernel,
        grid=(T // tm,),
        in_specs=[pl.BlockSpec((tm, MLA_KV_RANK), lambda i: (i, 0)),
                  pl.BlockSpec((tm, LANES), lambda i: (i, 0)),
                  _full((MLA_KV_RANK, hp)), _full((MLA_KV_RANK, hv))],
        out_specs=[pl.BlockSpec((tm, hp), lambda i: (i, 0)),
                   pl.BlockSpec((tm, hv), lambda i: (i, 0))],
        out_shape=[jax.ShapeDtypeStruct((T, hp), BF16), jax.ShapeDtypeStruct((T, hv), BF16)],
        compiler_params=_cparams("parallel"),
        name="kv_expand",
    )(ckv, kpe, lw["w_k"], lw["w_v"])


def _attn_kernel(q_ref, k_ref, v_ref, o_ref, *, tq, nq, q_start, sk):
    lane = lax.broadcasted_iota(jnp.int32, (tq, LANES), 1)
    nt = (((1,), (1,)), ((), ()))
    for i in range(nq):
        q_lo = q_start + i * tq
        k_end = min(((q_lo + tq - 1) // CHUNK + 1) * CHUNK, sk)
        kw = -(-k_end // LANES) * LANES
        c0 = min((q_lo // CHUNK + 1) * CHUNK, sk) // LANES * LANES
        tail = kw - c0
        if tail:
            q_chunk = (q_lo + lax.broadcasted_iota(jnp.int32, (tq, tail), 0)) // CHUNK
            k_pos = c0 + lax.broadcasted_iota(jnp.int32, (tq, tail), 1)
            visible = k_pos // CHUNK <= q_chunk
            if kw > sk:
                visible = visible & (k_pos < sk)
        outs = []
        for hh in range(2):
            hs = slice(hh * HEAD_PAD, (hh + 1) * HEAD_PAD)
            q = q_ref[i * tq:(i + 1) * tq, hs]
            parts = []
            if c0:
                parts.append(lax.dot_general(q, k_ref[0:c0, hs], nt, preferred_element_type=F32))
            if tail:
                s_t = lax.dot_general(q, k_ref[c0:kw, hs], nt, preferred_element_type=F32)
                parts.append(jnp.where(visible, s_t, NEG_INF))
            m = jnp.max(parts[0], -1, keepdims=True)
            for s_ in parts[1:]:
                m = jnp.maximum(m, jnp.max(s_, -1, keepdims=True))
            l = None
            acc = None
            lo = 0
            for s_ in parts:
                p = jnp.exp(s_ - m)
                ps = jnp.sum(p, -1, keepdims=True)
                pv = _dot(p.astype(BF16), v_ref[lo:lo + s_.shape[1], :])
                l = ps if l is None else l + ps
                acc = pv if acc is None else acc + pv
                lo += s_.shape[1]
            outs.append(acc / l)
        o_ref[i * tq:(i + 1) * tq, :] = jnp.where(lane < MLA_V, outs[0], outs[1])


def _attention(q, k, v, nb, sq, skp, sk, q_start, *, tq=256):
    tq = min(tq, sq)
    kern = functools.partial(_attn_kernel, tq=tq, nq=sq // tq, q_start=q_start, sk=sk)
    return pl.pallas_call(
        kern,
        grid=(nb, MLA_HEADS // 2),
        in_specs=[pl.BlockSpec((sq, 2 * HEAD_PAD), lambda b, j: (b, j)),
                  pl.BlockSpec((skp, 2 * HEAD_PAD), lambda b, j: (b, j)),
                  pl.BlockSpec((skp, 2 * MLA_V), lambda b, j: (b, j))],
        out_specs=pl.BlockSpec((sq, 2 * MLA_V), lambda b, j: (b, j)),
        out_shape=jax.ShapeDtypeStruct((nb * sq, MLA_HEADS * MLA_V), F32),
        compiler_params=_cparams("parallel", "parallel"),
        name="attention",
    )(q, k, v)


def _head_sum(z, ones_bd):
    hi = z.astype(BF16)
    lo = (z - hi.astype(F32)).astype(BF16)
    return _dot(hi, ones_bd) + _dot(lo, ones_bd)


def _rwkv_pre_kernel(p_ref, prev_ref, shift_ref, mu_ref, w0_ref, wup_ref, a0_ref, aup_ref, gup_ref,
                     kk_ref, ka_ref, rk_ref, ones_ref,
                     r_out, w_out, k_out, v_out, kk_out, a_out, g_out, bonus_out, *, tiles_per_seq):
    i = pl.program_id(0)
    p = p_ref[...]
    tm = p.shape[0]
    first = (i % tiles_per_seq) == 0
    prev_row = jnp.where(first, shift_ref[0], prev_ref[7:8, :])
    row = lax.broadcasted_iota(jnp.int32, p.shape, 0)
    prev = jnp.where(row == 0, prev_row, pltpu.roll(p, 1, 0))
    ps = p + (prev - p) * mu_ref[...]
    d = RWKV_DIM
    r = ps[:, 0:d]
    k = ps[:, d:2 * d]
    v = ps[:, 2 * d:3 * d]
    wa = ps[:, 3 * d:3 * d + LANES]
    gd = ps[:, 3 * d + LANES:3 * d + 2 * LANES]
    ones_bd = ones_ref[...]
    wlin = w0_ref[...] + _dot(jnp.tanh(wa).astype(BF16), wup_ref[...])
    z = -wlin
    w = -(jnp.maximum(z, 0.0) + jnp.log1p(jnp.exp(-jnp.abs(z)))) - 0.5
    w_out[...] = jnp.exp(-jnp.exp(w))
    a = _sigmoid(a0_ref[...] + _dot(wa.astype(BF16), aup_ref[...]))
    g_out[...] = _dot(_sigmoid(gd).astype(BF16), gup_ref[...])
    kk = k * kk_ref[...]
    kk_out[...] = kk * lax.rsqrt(_head_sum(kk * kk, ones_bd) + 1e-12)
    kn = k * (1.0 + (a - 1.0) * ka_ref[...])
    r_out[...] = r
    k_out[...] = kn
    v_out[...] = v
    a_out[...] = a
    bonus_out[...] = _head_sum(r * kn * rk_ref[...], ones_bd) * v


def _rwkv_pre(rw, shift_prev, lw, nb, s, *, tm=256):
    T = nb * s
    tm = min(tm, s)
    tps = s // tm
    d = RWKV_DIM
    vec = lambda n: _full((1, n))
    out = jax.ShapeDtypeStruct((T, d), F32)
    tile = pl.BlockSpec((tm, d), lambda i: (i, 0))
    kern = functools.partial(_rwkv_pre_kernel, tiles_per_seq=tps)
    return pl.pallas_call(
        kern,
        grid=(T // tm,),
        in_specs=[pl.BlockSpec((tm, RWKV_PROJ), lambda i: (i, 0)),
                  pl.BlockSpec((8, RWKV_PROJ), lambda i: (jnp.maximum(i * (tm // 8) - 1, 0), 0)),
                  pl.BlockSpec((1, 1, RWKV_PROJ), lambda i: (i // tps, 0, 0)),
                  vec(RWKV_PROJ), vec(d), _full((LANES, d)), vec(d), _full((LANES, d)),
                  _full((LANES, d)), vec(d), vec(d), vec(d), _full((d, d))],
        out_specs=[tile] * 8,
        out_shape=[out] * 8,
        compiler_params=_cparams("parallel"),
        name="rwkv_pre",
    )(rw, rw, shift_prev.reshape(nb, 1, RWKV_PROJ), lw["mu"], lw["w0"], lw["w_up"], lw["a0"], lw["a_up"],
      lw["g_up"], lw["k_k"], lw["k_a"], lw["r_k"], lw["ones_bd"])


def _rwkv_scan_kernel(r_ref, w_ref, k_ref, v_ref, kk_ref, a_ref, s0_ref, o_ref, sT_ref, st_ref, *, tt):
    n = RWKV_HEAD

    @pl.when(pl.program_id(0) == 0)
    def _():
        st_ref[...] = s0_ref[...]

    def step(t, carry):
        acc = st_ref[0] * kk_ref[t, 0:1, :]
        for q in range(1, n):
            acc = acc + st_ref[q] * kk_ref[t, q:q + 1, :]
        sa = -acc
        vt = v_ref[t]
        o = None
        for q in range(n):
            kkq = kk_ref[t, q:q + 1, :]
            s_new = (st_ref[q] * w_ref[t, q:q + 1, :] + sa * (kkq * a_ref[t, q:q + 1, :])
                     + vt * k_ref[t, q:q + 1, :])
            st_ref[q] = s_new
            term = s_new * r_ref[t, q:q + 1, :]
            o = term if o is None else o + term
        o_ref[t] = o
        return carry

    lax.fori_loop(0, tt, step, 0)

    @pl.when(pl.program_id(0) == pl.num_programs(0) - 1)
    def _():
        sT_ref[...] = st_ref[...]


def _rwkv_scan(seqs, s0, s, *, tt=16):
    L = s0.shape[-1]
    n = RWKV_HEAD
    tt = min(tt, s)
    blk = pl.BlockSpec((tt, n, L), lambda i: (i, 0, 0))
    kern = functools.partial(_rwkv_scan_kernel, tt=tt)
    return pl.pallas_call(
        kern,
        grid=(s // tt,),
        in_specs=[blk] * 6 + [_full((n, n, L))],
        out_specs=[blk, _full((n, n, L))],
        out_shape=[jax.ShapeDtypeStruct((s, n, L), F32), jax.ShapeDtypeStruct((n, n, L), F32)],
        scratch_shapes=[pltpu.VMEM((n, n, L), F32)],
        compiler_params=_cparams("arbitrary"),
        name="rwkv_scan",
    )(*seqs, s0)


def _s5_params_kernel(are_ref, aim_ref, ldt_ref, bre_ref, bim_ref, abre_ref, abim_ref, bbre_ref, bbim_ref):
    lr = are_ref[...]
    li = aim_ref[...]
    dt = jnp.exp(ldt_ref[...])
    mag = jnp.exp(lr * dt)
    ab_re = mag * jnp.cos(li * dt)
    ab_im = mag * jnp.sin(li * dt)
    den = lr * lr + li * li
    f_re = ((ab_re - 1.0) * lr + ab_im * li) / den
    f_im = (ab_im * lr - (ab_re - 1.0) * li) / den
    abre_ref[...] = ab_re
    abim_ref[...] = ab_im
    for i in range(SSM_GROUP):
        br = bre_ref[i]
        bi = bim_ref[i]
        bbre_ref[i] = f_re * br - f_im * bi
        bbim_ref[i] = f_re * bi + f_im * br


def _s5_params(a_re, a_im, log_dt, b_re, b_im):
    g, n = a_re.shape
    gn = jax.ShapeDtypeStruct((g, n), F32)
    ign = jax.ShapeDtypeStruct((SSM_GROUP, g, n), F32)
    return pl.pallas_call(
        _s5_params_kernel,
        out_shape=[gn, gn, ign, ign],
        name="s5_params",
    )(a_re, a_im, log_dt.reshape(g, 1), jnp.transpose(b_re, (2, 0, 1)), jnp.transpose(b_im, (2, 0, 1)))


def _gelu_tanh(x):
    return 0.5 * x * (1.0 + jnp.tanh(math.sqrt(2.0 / math.pi) * (x + 0.044715 * (x * x * x))))


def _s5_kernel(u_ref, bre_ref, bim_ref, cre_ref, cim_ref, d_ref, are_ref, aim_ref, h0re_ref, h0im_ref,
               gw_ref, gb_ref, y_ref, hTre_ref, hTim_ref, hre_s, him_s, sre_s, sim_s, yy_s, *, tt, nb):
    cw = SSM_CH // SSM_CHUNKS

    @pl.when(pl.program_id(0) == 0)
    def _():
        hre_s[...] = h0re_ref[...]
        him_s[...] = h0im_ref[...]

    for c in range(SSM_CHUNKS):
        uc = u_ref[:, c * LANES:(c + 1) * LANES]
        ub = uc.astype(BF16)
        sre_s[...] = _dot(ub, bre_ref[c])
        sim_s[...] = _dot(ub, bim_ref[c])
        a_re = jnp.broadcast_to(are_ref[:, c * cw:(c + 1) * cw], (nb, cw))
        a_im = jnp.broadcast_to(aim_ref[:, c * cw:(c + 1) * cw], (nb, cw))

        def step(t, carry, a_re=a_re, a_im=a_im):
            hr, hi = carry
            rows = pl.ds(pl.multiple_of(t * nb, nb), nb)
            nr = a_re * hr - a_im * hi + sre_s[rows, :]
            ni = a_re * hi + a_im * hr + sim_s[rows, :]
            sre_s[rows, :] = nr
            sim_s[rows, :] = ni
            return nr, ni

        hr, hi = lax.fori_loop(0, tt, step, (hre_s[:, c * cw:(c + 1) * cw], him_s[:, c * cw:(c + 1) * cw]))
        hre_s[:, c * cw:(c + 1) * cw] = hr
        him_s[:, c * cw:(c + 1) * cw] = hi
        yc = _dot(sre_s[...].astype(BF16), cre_ref[c]) - _dot(sim_s[...].astype(BF16), cim_ref[c])
        yy_s[:, c * LANES:(c + 1) * LANES] = yc + d_ref[:, c * LANES:(c + 1) * LANES] * uc

    y = _gelu_tanh(yy_s[...])
    z = _dot(y.astype(BF16), gw_ref[...]) + gb_ref[...]
    y_ref[...] = z[:, :SSM_DIM] * _sigmoid(z[:, SSM_DIM:])

    @pl.when(pl.program_id(0) == pl.num_programs(0) - 1)
    def _():
        hTre_ref[...] = hre_s[...]
        hTim_ref[...] = him_s[...]


def _s5(u_tm, h0_re, h0_im, lw, nb, s, *, rows=512):
    tt = max(min(rows // nb, s), 1)
    R = tt * nb
    cw = SSM_CH // SSM_CHUNKS
    kern = functools.partial(_s5_kernel, tt=tt, nb=nb)
    return pl.pallas_call(
        kern,
        grid=(s // tt,),
        in_specs=[pl.BlockSpec((R, SSM_DIM), lambda i: (i, 0)),
                  _full((SSM_CHUNKS, LANES, cw)), _full((SSM_CHUNKS, LANES, cw)),
                  _full((SSM_CHUNKS, cw, LANES)), _full((SSM_CHUNKS, cw, LANES)),
                  _full((1, SSM_DIM)), _full((1, SSM_CH)), _full((1, SSM_CH)),
                  _full((nb, SSM_CH)), _full((nb, SSM_CH)),
                  _full((SSM_DIM, 2 * SSM_DIM)), _full((1, 2 * SSM_DIM))],
        out_specs=[pl.BlockSpec((R, SSM_DIM), lambda i: (i, 0)), _full((nb, SSM_CH)), _full((nb, SSM_CH))],
        out_shape=[jax.ShapeDtypeStruct((s * nb, SSM_DIM), F32),
                   jax.ShapeDtypeStruct((nb, SSM_CH), F32), jax.ShapeDtypeStruct((nb, SSM_CH), F32)],
        scratch_shapes=[pltpu.VMEM((nb, SSM_CH), F32), pltpu.VMEM((nb, SSM_CH), F32),
                        pltpu.VMEM((R, cw), F32), pltpu.VMEM((R, cw), F32), pltpu.VMEM((R, SSM_DIM), F32)],
        compiler_params=_cparams("arbitrary"),
        name="s5",
    )(u_tm, lw["s5_bre"], lw["s5_bim"], lw["s5_cre"], lw["s5_cim"], lw["s5_d"], lw["s5_are"], lw["s5_aim"],
      h0_re, h0_im, lw["glu_w"], lw["glu_b"])


def _merge_kernel(x_ref, ya_ref, o_ref, bonus_ref, g_ref, yc_ref, gates_ref, lng_ref, lnb_ref, ones_ref,
                  wa_ref, wb_ref, wc_ref, wo_ref, ln1g_ref, ln1b_ref, out_ref):
    ones_bd = ones_ref[...]
    o = o_ref[...]
    inv_n = 1.0 / RWKV_HEAD
    mean = _head_sum(o, ones_bd) * inv_n
    oc = o - mean
    var = _head_sum(oc * oc, ones_bd) * inv_n
    yb = (oc * lax.rsqrt(var + RWKV_LN_EPS) * lng_ref[...] + lnb_ref[...] + bonus_ref[...]) * g_ref[...]
    d = D_MODEL
    ga = _sigmoid(gates_ref[:, 0:d])
    gb = _sigmoid(gates_ref[:, d:2 * d])
    gc = _sigmoid(gates_ref[:, 2 * d:3 * d])
    merged = (ga * _dot(ya_ref[...].astype(BF16), wa_ref[...]) + gb * _dot(yb.astype(BF16), wb_ref[...])
              + gc * _dot(yc_ref[...].astype(BF16), wc_ref[...]))
    y = DN_ALPHA * x_ref[...] + _dot(merged.astype(BF16), wo_ref[...])
    out_ref[...] = _layer_norm(y, ln1g_ref[...], ln1b_ref[...])


def _merge(x, ya, o, bonus, g, yc_tm, gates, lw, nb, s, *, tm=256):
    tm = min(tm, s)
    nt = s // tm
    d = D_MODEL
    h = RWKV_DIM
    row = lambda w: pl.BlockSpec((tm, w), lambda b, i: (b * nt + i, 0))
    vec = lambda n: _full((1, n))
    return pl.pallas_call(
        _merge_kernel,
        grid=(nb, nt),
        in_specs=[row(d), row(h), row(h), row(h), row(h),
                  pl.BlockSpec((tm, SSM_DIM), lambda b, i: (i, b)),
                  row(3 * d), vec(h), vec(h), _full((h, h)),
                  _full((h, d)), _full((h, d)), _full((h, d)), _full((d, d)), vec(d), vec(d)],
        out_specs=row(d),
        out_shape=jax.ShapeDtypeStruct((nb * s, d), F32),
        compiler_params=_cparams("parallel", "parallel"),
        name="merge",
    )(x, ya, o, bonus, g, yc_tm, gates, lw["lnx_g"], lw["lnx_b"], lw["ones_bd"],
      lw["w_br_a"], lw["w_br_b"], lw["w_br_c"], lw["w_out"], lw["ln1_g"], lw["ln1_b"])


INFO_GATE, INFO_EID, INFO_RANK = 0, 4, 8


def _router_kernel(x_ref, whi_ref, wlo_ref, b_ref, tri_ref, info_ref, cnt_ref, base_s):
    @pl.when(pl.program_id(0) == 0)
    def _():
        base_s[...] = jnp.zeros_like(base_s)

    x = x_ref[...]
    xh = x.astype(BF16)
    xl = (x - xh.astype(F32)).astype(BF16)
    logits = _dot(xh, whi_ref[...]) + _dot(xl, whi_ref[...]) + _dot(xh, wlo_ref[...]) + b_ref[...]
    lane_i = lax.broadcasted_iota(jnp.int32, logits.shape, 1)
    lane = lane_i.astype(F32)
    vals, sels, ids = [], [], []
    for _ in range(TOP_K):
        m = jnp.max(logits, -1, keepdims=True)
        idx = jnp.min(jnp.where(logits == m, lane, float(LANES)), -1, keepdims=True)
        sel = lane == idx
        vals.append(m)
        sels.append(sel)
        ids.append(idx)
        logits = jnp.where(sel, -3e38, logits)
    es = [jnp.exp(v - vals[0]) for v in vals]
    den = es[0] + es[1] + es[2] + es[3]
    chosen = jnp.zeros_like(logits)
    for sel in sels:
        chosen = chosen + jnp.where(sel, 1.0, 0.0)
    before = _dot(tri_ref[...], chosen.astype(BF16)) + base_s[...]
    info = jnp.zeros_like(logits)
    for k in range(TOP_K):
        rank = jnp.sum(jnp.where(sels[k], before, 0.0), -1, keepdims=True)
        info = (info + jnp.where(lane_i == INFO_GATE + k, es[k] / den, 0.0)
                + jnp.where(lane_i == INFO_EID + k, ids[k], 0.0)
                + jnp.where(lane_i == INFO_RANK + k, rank, 0.0))
    info_ref[...] = info
    base_s[...] += jnp.sum(chosen, 0, keepdims=True)
    cnt_ref[...] = base_s[...]


def _router(x, lw, *, tm=512):
    T = x.shape[0]
    tm = min(tm, T)
    tri = jnp.asarray(np.tril(np.ones((tm, tm), np.float32), -1), dtype=BF16)
    return pl.pallas_call(
        _router_kernel,
        grid=(T // tm,),
        in_specs=[pl.BlockSpec((tm, D_MODEL), lambda i: (i, 0)),
                  _full((D_MODEL, LANES)), _full((D_MODEL, LANES)), _full((1, LANES)), _full((tm, tm))],
        out_specs=[pl.BlockSpec((tm, LANES), lambda i: (i, 0)), _full((1, LANES))],
        out_shape=[jax.ShapeDtypeStruct((T, LANES), F32), jax.ShapeDtypeStruct((1, LANES), F32)],
        scratch_shapes=[pltpu.VMEM((1, LANES), F32)],
        compiler_params=_cparams("arbitrary"),
        name="router",
    )(x, lw["router_hi"], lw["router_lo"], lw["router_b"], tri)


def _moe_schedule(info, cnt, tm_e, nt):
    eid = info[:, INFO_EID:INFO_EID + TOP_K].astype(jnp.int32)
    rank = info[:, INFO_RANK:INFO_RANK + TOP_K].astype(jnp.int32)
    counts = cnt[0, :N_EXPERTS].astype(jnp.int32)
    gsz = (counts + tm_e - 1) // tm_e * tm_e
    gend = jnp.cumsum(gsz)
    goff = gend - gsz
    onehot = eid[..., None] == jnp.arange(N_EXPERTS, dtype=jnp.int32)
    pos = jnp.sum(jnp.where(onehot, goff, 0), -1) + rank
    tile_start = jnp.arange(nt, dtype=jnp.int32) * tm_e
    tile_e = jnp.minimum(jnp.sum((gend[None, :] <= tile_start[:, None]).astype(jnp.int32), -1), N_EXPERTS - 1)
    n_valid = (gend[-1] // tm_e).reshape(1)
    return pos.reshape(-1).astype(jnp.int32), tile_e.astype(jnp.int32), n_valid.astype(jnp.int32)


def _dispatch_kernel(pos_ref, x_ref, xs_in_ref, xs_ref, sem, *, tm):
    del xs_in_ref

    def issue(t, carry):
        for k in range(TOP_K):
            p = pos_ref[t * TOP_K + k]
            pltpu.make_async_copy(x_ref.at[pl.ds(t, 1)], xs_ref.at[pl.ds(p, 1)], sem).start()
        return carry

    lax.fori_loop(0, tm, issue, 0)
    for k in range(TOP_K):
        pltpu.make_async_copy(x_ref, xs_ref.at[pl.ds(0, tm)], sem).wait()


def _dispatch(x, pos, rows, *, tm=256):
    T, d = x.shape
    tm = min(tm, T)
    kern = functools.partial(_dispatch_kernel, tm=tm)
    return pl.pallas_call(
        kern,
        grid=(T // tm,),
        in_specs=[pl.BlockSpec((tm * TOP_K,), lambda i: (i,), memory_space=pltpu.SMEM),
                  pl.BlockSpec((tm, d), lambda i: (i, 0)),
                  pl.BlockSpec(memory_space=pl.ANY)],
        out_specs=pl.BlockSpec(memory_space=pl.ANY),
        out_shape=jax.ShapeDtypeStruct((rows, d), F32),
        scratch_shapes=[pltpu.SemaphoreType.DMA],
        input_output_aliases={2: 0},
        compiler_params=_cparams("arbitrary"),
        name="moe_dispatch",
    )(pos, x, jnp.zeros((rows, d), F32))


def _expert_kernel(te_ref, nv_ref, xs_ref, wgu_ref, bgu_ref, wd_ref, bd_ref, ys_ref):
    del te_ref

    @pl.when(pl.program_id(0) < nv_ref[0])
    def _():
        h = _dot(xs_ref[...].astype(BF16), wgu_ref[0]) + bgu_ref[0]
        hg = jnp.minimum(h[:, :D_FF], SWIGLU_LIMIT)
        hl = jnp.clip(h[:, D_FF:], -SWIGLU_LIMIT, SWIGLU_LIMIT)
        act = hg * _sigmoid(SWIGLU_ALPHA * hg) * (hl + 1.0)
        ys_ref[...] = _dot(act.astype(BF16), wd_ref[0]) + bd_ref[0]

    @pl.when(pl.program_id(0) >= nv_ref[0])
    def _():
        ys_ref[...] = jnp.zeros_like(ys_ref)


def _experts(xs, tile_e, n_valid, lw, tm_e):
    rows, d = xs.shape
    nt = rows // tm_e
    tile = lambda j, te, nv: (jnp.minimum(j, nv[0] - 1), 0)
    wsel = lambda j, te, nv: (te[jnp.minimum(j, nv[0] - 1)], 0, 0)
    return pl.pallas_call(
        _expert_kernel,
        grid_spec=pltpu.PrefetchScalarGridSpec(
            num_scalar_prefetch=2,
            grid=(nt,),
            in_specs=[pl.BlockSpec((tm_e, d), tile),
                      pl.BlockSpec((1, d, 2 * D_FF), wsel), pl.BlockSpec((1, 1, 2 * D_FF), wsel),
                      pl.BlockSpec((1, D_FF, d), wsel), pl.BlockSpec((1, 1, d), wsel)],
            out_specs=pl.BlockSpec((tm_e, d), lambda j, te, nv: (j, 0))),
        out_shape=jax.ShapeDtypeStruct((rows, d), F32),
        compiler_params=_cparams("arbitrary"),
        name="moe_experts",
    )(tile_e, n_valid, xs, lw["w_gu"], lw["b_gu"], lw["w_down"], lw["b_down"])


def _combine_kernel(pos_ref, x_ref, info_ref, ys_ref, ln2g_ref, ln2b_ref, out_ref, buf, sem, *, tm):
    def issue(t, carry):
        for k in range(TOP_K):
            p = pos_ref[t * TOP_K + k]
            pltpu.make_async_copy(ys_ref.at[pl.ds(p, 1)], buf.at[k, pl.ds(t, 1)], sem).start()
        return carry

    lax.fori_loop(0, tm, issue, 0)
    for k in range(TOP_K):
        pltpu.make_async_copy(ys_ref.at[pl.ds(0, tm)], buf.at[k], sem).wait()
    info = info_ref[...]
    lane = lax.broadcasted_iota(jnp.int32, info.shape, 1)
    acc = DN_ALPHA * x_ref[...]
    for k in range(TOP_K):
        gate = jnp.sum(jnp.where(lane == INFO_GATE + k, info, 0.0), -1, keepdims=True)
        acc = acc + gate * buf[k]
    out_ref[...] = _layer_norm(acc, ln2g_ref[...], ln2b_ref[...])


def _combine(x, info, pos, ys, lw, *, tm=256):
    T, d = x.shape
    tm = min(tm, T)
    kern = functools.partial(_combine_kernel, tm=tm)
    return pl.pallas_call(
        kern,
        grid=(T // tm,),
        in_specs=[pl.BlockSpec((tm * TOP_K,), lambda i: (i,), memory_space=pltpu.SMEM),
                  pl.BlockSpec((tm, d), lambda i: (i, 0)),
                  pl.BlockSpec((tm, LANES), lambda i: (i, 0)),
                  pl.BlockSpec(memory_space=pl.ANY),
                  _full((1, d)), _full((1, d))],
        out_specs=pl.BlockSpec((tm, d), lambda i: (i, 0)),
        out_shape=jax.ShapeDtypeStruct((T, d), F32),
        scratch_shapes=[pltpu.VMEM((TOP_K, tm, d), F32), pltpu.SemaphoreType.DMA],
        compiler_params=_cparams("arbitrary"),
        name="moe_combine",
    )(pos, x, info, ys, lw["ln2_g"], lw["ln2_b"])


def _moe(x, lw):
    T = x.shape[0]
    tm_e = 512 if T >= 4096 else 128
    nt = T * TOP_K // tm_e + N_EXPERTS
    info, cnt = _router(x, lw)
    pos, tile_e, n_valid = _moe_schedule(info, cnt, tm_e, nt)
    xs = _dispatch(x, pos, nt * tm_e)
    ys = _experts(xs, tile_e, n_valid, lw, tm_e)
    return _combine(x, info, pos, ys, lw)


def _ones_block_diag():
    idx = np.arange(RWKV_DIM) // RWKV_HEAD
    return jnp.asarray((idx[:, None] == idx[None, :]).astype(np.float32), dtype=BF16)


def _deinterleave_matrix():
    n = 2 * D_FF
    src = np.concatenate([np.arange(0, n, 2), np.arange(1, n, 2)])
    return jnp.asarray((np.arange(n)[:, None] == src[None, :]).astype(np.float32), dtype=BF16)


def _block_diag(x):
    C, G, r, c = x.shape
    eye = jnp.eye(G, dtype=x.dtype)
    return jnp.einsum("cgij,gh->cgihj", x, eye).reshape(C, G * r, G * c)


def _prep_layer(P, l):
    f = lambda name: P[name][l]
    lw = {}
    w_in = f("w_in")
    offs = np.cumsum((0,) + IN_SIZES)
    cols = lambda j: w_in[:, offs[j]:offs[j + 1]]
    w_q, w_kv = cols(0), cols(1)
    w_ckv, w_kpe = w_kv[:, :MLA_KV_RANK], w_kv[:, MLA_KV_RANK:]
    half = MLA_ROPE // 2
    zpad = jnp.zeros((D_MODEL, LANES - MLA_ROPE), F32)
    w_kpe_rot = jnp.concatenate([-w_kpe[:, half:], w_kpe[:, :half]], 1)
    lw["w_mla"] = jnp.concatenate([w_q, w_ckv, w_kpe, zpad, w_kpe_rot, zpad], 1).astype(BF16)
    lw["w_rw"] = cols(2).astype(BF16)
    lw["w_su"] = cols(3).astype(BF16)
    lw["w_gates"] = jnp.concatenate([cols(4), cols(5), cols(6)], 1).astype(BF16)
    lw["q_norm"] = f("mla_q_a_norm").reshape(1, -1)
    lw["kv_norm"] = f("mla_kv_a_norm").reshape(1, -1)
    wqb = f("mla_w_q_b").reshape(MLA_Q_RANK, MLA_HEADS, MLA_NOPE + MLA_ROPE)
    nope, x1, x2 = wqb[..., :MLA_NOPE], wqb[..., MLA_NOPE:MLA_NOPE + half], wqb[..., MLA_NOPE + half:]
    z32 = jnp.zeros_like(wqb[..., :HEAD_PAD - MLA_NOPE - MLA_ROPE])
    plain = jnp.concatenate([x1, x2, nope, z32], -1).reshape(MLA_Q_RANK, -1)
    rot = jnp.concatenate([-x2, x1, jnp.zeros_like(nope), z32], -1).reshape(MLA_Q_RANK, -1)
    lw["w_qb"] = jnp.concatenate([plain, rot], 1).astype(BF16)
    wkvb = f("mla_w_kv_b").reshape(MLA_KV_RANK, MLA_HEADS, MLA_NOPE + MLA_V)
    k_nope, v = wkvb[..., :MLA_NOPE], wkvb[..., MLA_NOPE:]
    zk = jnp.zeros_like(k_nope[..., :MLA_ROPE])
    lw["w_k"] = jnp.concatenate([zk, k_nope, zk], -1).reshape(MLA_KV_RANK, -1).astype(BF16)
    lw["w_v"] = v.reshape(MLA_KV_RANK, -1).astype(BF16)
    row = lambda name: f(name).reshape(1, -1)
    lw["mu"] = row("rwkv_mu")
    lw["w0"] = row("rwkv_w0")
    z64 = jnp.zeros((64, RWKV_DIM), F32)
    lw["w_up"] = jnp.concatenate([f("rwkv_w_up"), z64], 0).astype(BF16)
    lw["a_up"] = jnp.concatenate([z64, f("rwkv_a_up")], 0).astype(BF16)
    lw["a0"] = row("rwkv_a0")
    lw["g_up"] = f("rwkv_g_up").astype(BF16)
    lw["k_k"] = row("rwkv_k_k")
    lw["k_a"] = row("rwkv_k_a")
    lw["r_k"] = row("rwkv_r_k")
    lw["lnx_g"] = row("rwkv_lnx_g")
    lw["lnx_b"] = row("rwkv_lnx_b")
    lw["ones_bd"] = _ones_block_diag()
    ab_re, ab_im, bb_re, bb_im = _s5_params(f("ssm_a_re"), f("ssm_a_im"), f("ssm_log_dt"),
                                            f("ssm_b_re"), f("ssm_b_im"))
    gpc = SSM_GROUPS // SSM_CHUNKS
    chunked = lambda t: t.reshape(SSM_CHUNKS, gpc, t.shape[1], t.shape[2])
    lw["s5_bre"] = _block_diag(chunked(jnp.transpose(bb_re, (1, 0, 2)))).astype(BF16)
    lw["s5_bim"] = _block_diag(chunked(jnp.transpose(bb_im, (1, 0, 2)))).astype(BF16)
    lw["s5_cre"] = _block_diag(chunked(jnp.transpose(f("ssm_c_re"), (0, 2, 1)))).astype(BF16)
    lw["s5_cim"] = _block_diag(chunked(jnp.transpose(f("ssm_c_im"), (0, 2, 1)))).astype(BF16)
    lw["s5_d"] = row("ssm_d")
    lw["s5_are"] = ab_re.reshape(1, -1)
    lw["s5_aim"] = ab_im.reshape(1, -1)
    lw["glu_w"] = f("ssm_glu_w").astype(BF16)
    lw["glu_b"] = row("ssm_glu_b")
    for name in ("w_br_a", "w_br_b", "w_br_c", "w_out"):
        lw[name] = f(name).astype(BF16)
    for name in ("ln1_g", "ln1_b", "ln2_g", "ln2_b"):
        lw[name] = row(name)
    rw_ = jnp.pad(f("router_w"), ((0, 0), (0, LANES - N_EXPERTS)))
    hi = rw_.astype(BF16)
    lw["router_hi"] = hi
    lw["router_lo"] = (rw_ - hi.astype(F32)).astype(BF16)
    lw["router_b"] = jnp.pad(f("router_b"), (0, LANES - N_EXPERTS), constant_values=NEG_INF).reshape(1, -1)
    wgu = f("exp_w_gu").reshape(N_EXPERTS * D_MODEL, 2 * D_FF)
    wgu = _matmul(wgu, _deinterleave_matrix(), 1, wgu.shape[0], out_dtype=BF16)
    lw["w_gu"] = wgu.reshape(N_EXPERTS, D_MODEL, 2 * D_FF)
    bgu = f("exp_b_gu")
    lw["b_gu"] = jnp.concatenate([bgu[..., 0::2], bgu[..., 1::2]], -1).reshape(N_EXPERTS, 1, 2 * D_FF)
    lw["w_down"] = f("exp_w_down").astype(BF16)
    lw["b_down"] = f("exp_b_down").reshape(N_EXPERTS, 1, D_MODEL)
    return lw


def _rope_tables(pos):
    half = MLA_ROPE // 2
    inv = ROPE_THETA ** (-jnp.arange(half, dtype=F32) / half)
    ang = pos.astype(F32)[:, None] * inv
    cos, sin = jnp.cos(ang), jnp.sin(ang)
    n = pos.shape[0]
    tc = jnp.concatenate([cos, cos, jnp.ones((n, MLA_NOPE), F32),
                          jnp.zeros((n, HEAD_PAD - MLA_NOPE - MLA_ROPE), F32)], 1)
    ts = jnp.concatenate([sin, sin, jnp.zeros((n, HEAD_PAD - MLA_ROPE), F32)], 1)
    return tc, ts


def _to_lanes(t, nb, s, lanes):
    t = t.reshape(nb, s, RWKV_HEADS, RWKV_HEAD).transpose(1, 3, 0, 2).reshape(s, RWKV_HEAD, nb * RWKV_HEADS)
    return jnp.pad(t, ((0, 0), (0, 0), (0, lanes - nb * RWKV_HEADS)))


def _layer(x, lw, past, nb, s, tables, att_tk=LANES):
    T = nb * s
    tc, ts = tables
    if past is None:
        start = 0
        shift_p = jnp.zeros((nb, RWKV_PROJ), F32)
        wkv_p = jnp.zeros((nb, RWKV_HEADS, RWKV_HEAD, RWKV_HEAD), F32)
        sre_p = jnp.zeros((nb, SSM_CH), F32)
        sim_p = jnp.zeros((nb, SSM_CH), F32)
    else:
        ckv_p, kpe_p, shift_p, wkv_p, sre_p, sim_p = past
        start = ckv_p.shape[1]
        sre_p = sre_p.reshape(nb, SSM_CH)
        sim_p = sim_p.reshape(nb, SSM_CH)

    q, ckv, kpe128 = _mla_in(x, lw, tc, ts)
    if past is None:
        sk = s
        ckv_all, kpe_all = ckv, kpe128
    else:
        sk = start + s
        ckv_all = jnp.concatenate([ckv_p, ckv.reshape(nb, s, -1)], 1)
        kpe_new = kpe128.reshape(nb, s, LANES)
        kpe_all = jnp.concatenate([jnp.pad(kpe_p, ((0, 0), (0, 0), (0, LANES - MLA_ROPE))), kpe_new], 1)
    skp = -(-sk // att_tk) * att_tk
    if skp != sk:
        ckv_all = jnp.pad(ckv_all.reshape(nb, sk, -1), ((0, 0), (0, skp - sk), (0, 0)))
        kpe_all = jnp.pad(kpe_all.reshape(nb, sk, -1), ((0, 0), (0, skp - sk), (0, 0)))
    k_pad, v_all = _kv_expand(ckv_all.reshape(nb * skp, -1), kpe_all.reshape(nb * skp, -1), lw)
    ya = _attention(q, k_pad, v_all, nb, s, skp, sk, start)

    rw = _matmul(x, lw["w_rw"], nb, s, tn=896)
    r, wdec, kn, v, kk, a, g, bonus = _rwkv_pre(rw, shift_p, lw, nb, s)
    lanes = -(-nb * RWKV_HEADS // LANES) * LANES
    seqs = [_to_lanes(t, nb, s, lanes) for t in (r, wdec, kn, v, kk, a)]
    s0 = jnp.transpose(wkv_p, (3, 2, 0, 1)).reshape(RWKV_HEAD, RWKV_HEAD, nb * RWKV_HEADS)
    s0 = jnp.pad(s0, ((0, 0), (0, 0), (0, lanes - nb * RWKV_HEADS)))
    o_l, sT = _rwkv_scan(seqs, s0, s)
    o = (o_l[:, :, :nb * RWKV_HEADS].reshape(s, RWKV_HEAD, nb, RWKV_HEADS)
         .transpose(2, 0, 3, 1).reshape(T, RWKV_DIM))
    wkv_n = (sT[:, :, :nb * RWKV_HEADS].reshape(RWKV_HEAD, RWKV_HEAD, nb, RWKV_HEADS).transpose(2, 3, 1, 0))
    shift_n = rw.reshape(nb, s, RWKV_PROJ)[:, -1]

    su_tm = _matmul(x, lw["w_su"], nb, s, time_major_out=True).reshape(s * nb, SSM_DIM)
    yc_tm, sre_n, sim_n = _s5(su_tm, sre_p, sim_p, lw, nb, s)
    yc_tm = yc_tm.reshape(s, nb * SSM_DIM)

    gates = _matmul(x, lw["w_gates"], nb, s)
    x1 = _merge(x, ya, o, bonus, g, yc_tm, gates, lw, nb, s)
    x2 = _moe(x1, lw)
    new = (ckv.reshape(nb, s, MLA_KV_RANK), kpe128[:, :MLA_ROPE].reshape(nb, s, MLA_ROPE), shift_n, wkv_n,
           sre_n.reshape(nb, SSM_GROUPS, SSM_STATE), sim_n.reshape(nb, SSM_GROUPS, SSM_STATE))
    return x2, new


def _trunk(x3, weights, caches):
    nb, s, d = x3.shape
    start = 0 if caches is None else caches[0].shape[2]
    tc, ts = _rope_tables(start + jnp.arange(s))
    tables = (jnp.tile(tc, (nb, 1)), jnp.tile(ts, (nb, 1)))
    x = x3.reshape(nb * s, d)
    new = []
    for l in range(len(weights)):
        past = None if caches is None else tuple(c[l] for c in caches)
        x, st = _layer(x, weights[l], past, nb, s, tables)
        new.append(st)
    return (x.reshape(nb, s, d),) + tuple(jnp.stack([st[j] for st in new]) for j in range(6))


def kernel(x_prompt, x_sample, cache_mla_ckv, cache_mla_kpe, state_rwkv_shift, state_rwkv_wkv, state_ssm_re, state_ssm_im, w_in, mla_q_a_norm, mla_w_q_b, mla_kv_a_norm, mla_w_kv_b, rwkv_mu, rwkv_w0, rwkv_w_up, rwkv_a0, rwkv_a_up, rwkv_g_up, rwkv_k_k, rwkv_k_a, rwkv_r_k, rwkv_lnx_g, rwkv_lnx_b, ssm_a_re, ssm_a_im, ssm_b_re, ssm_b_im, ssm_c_re, ssm_c_im, ssm_d, ssm_log_dt, ssm_glu_w, ssm_glu_b, w_br_a, w_br_b, w_br_c, w_out, ln1_g, ln1_b, router_w, router_b, exp_w_gu, exp_b_gu, exp_w_down, exp_b_down, ln2_g, ln2_b):
    P = dict(w_in=w_in, mla_q_a_norm=mla_q_a_norm, mla_w_q_b=mla_w_q_b, mla_kv_a_norm=mla_kv_a_norm,
             mla_w_kv_b=mla_w_kv_b, rwkv_mu=rwkv_mu, rwkv_w0=rwkv_w0, rwkv_w_up=rwkv_w_up, rwkv_a0=rwkv_a0,
             rwkv_a_up=rwkv_a_up, rwkv_g_up=rwkv_g_up, rwkv_k_k=rwkv_k_k, rwkv_k_a=rwkv_k_a, rwkv_r_k=rwkv_r_k,
             rwkv_lnx_g=rwkv_lnx_g, rwkv_lnx_b=rwkv_lnx_b, ssm_a_re=ssm_a_re, ssm_a_im=ssm_a_im,
             ssm_b_re=ssm_b_re, ssm_b_im=ssm_b_im, ssm_c_re=ssm_c_re, ssm_c_im=ssm_c_im, ssm_d=ssm_d,
             ssm_log_dt=ssm_log_dt, ssm_glu_w=ssm_glu_w, ssm_glu_b=ssm_glu_b, w_br_a=w_br_a, w_br_b=w_br_b,
             w_br_c=w_br_c, w_out=w_out, ln1_g=ln1_g, ln1_b=ln1_b, router_w=router_w, router_b=router_b,
             exp_w_gu=exp_w_gu, exp_b_gu=exp_b_gu, exp_w_down=exp_w_down, exp_b_down=exp_b_down,
             ln2_g=ln2_g, ln2_b=ln2_b)
    depth = w_in.shape[0]
    weights = [_prep_layer(P, l) for l in range(depth)]
    outs_p = _trunk(x_prompt, weights, None)
    caches = (cache_mla_ckv, cache_mla_kpe, state_rwkv_shift, state_rwkv_wkv, state_ssm_re, state_ssm_im)
    outs_s = _trunk(x_sample, weights, caches)
    return (outs_p[0], outs_s[0]) + outs_p[1:] + outs_s[1:]
```

```python
import functools
import math

import numpy as np
import jax
import jax.numpy as jnp
from jax import lax
from jax.experimental import pallas as pl
from jax.experimental.pallas import tpu as pltpu

F32 = jnp.float32
BF16 = jnp.bfloat16

D_MODEL = 1024
CHUNK = 64
MLA_HEADS = 8
MLA_NOPE = 64
MLA_ROPE = 32
MLA_V = 64
MLA_Q_RANK = 384
MLA_KV_RANK = 256
ROPE_THETA = 10000.0
HEAD_PAD = 128
RWKV_HEADS = 8
RWKV_HEAD = 64
RWKV_DIM = RWKV_HEADS * RWKV_HEAD
RWKV_PROJ = 3 * RWKV_DIM + 64 + 64 + 128
RWKV_LN_EPS = 64e-5
SSM_DIM = 512
SSM_GROUP = 16
SSM_GROUPS = 32
SSM_STATE = 64
SSM_CHUNKS = 4
SSM_CH = SSM_GROUPS * SSM_STATE
IN_SIZES = (MLA_Q_RANK, MLA_KV_RANK + MLA_ROPE, RWKV_PROJ, SSM_DIM, D_MODEL, D_MODEL, D_MODEL)
N_EXPERTS = 32
TOP_K = 4
D_FF = 512
SWIGLU_LIMIT = 7.0
SWIGLU_ALPHA = 1.702
DEPTH = 2
DN_ALPHA = (2 * DEPTH) ** 0.25
LN_EPS = 1e-5
RMS_EPS = 1e-6
NEG_INF = -1e30
ATT_SCALE = (MLA_NOPE + MLA_ROPE) ** -0.5
LANES = 128
VMEM_LIMIT = 48 * 1024 * 1024


def _cparams(*sem):
    return pltpu.CompilerParams(dimension_semantics=sem, vmem_limit_bytes=VMEM_LIMIT)


def _dot(a, b):
    return jnp.dot(a, b, preferred_element_type=F32)


def _sigmoid(x):
    return 1.0 / (1.0 + jnp.exp(-x))


def _layer_norm(x, g, b):
    mu = jnp.mean(x, -1, keepdims=True)
    xc = x - mu
    var = jnp.mean(xc * xc, -1, keepdims=True)
    return xc * lax.rsqrt(var + LN_EPS) * g + b


def _full(shape):
    n = len(shape)
    return pl.BlockSpec(shape, lambda *_: (0,) * n)


def _mm_kernel(x_ref, w_ref, o_ref):
    o_ref[...] = _dot(x_ref[...].astype(BF16), w_ref[...]).astype(o_ref.dtype)


def _matmul(x, w, nb, s, *, time_major_out=False, tm=512, tn=1024, out_dtype=F32):
    K = x.shape[1]
    N = w.shape[1]
    tm = min(tm, s)
    tn = min(tn, N)
    while N % tn:
        tn -= LANES
    nt = s // tm
    if time_major_out:
        assert tn == N
        out_shape = jax.ShapeDtypeStruct((s, nb * N), out_dtype)
        out_spec = pl.BlockSpec((tm, N), lambda b, i, j: (i, b))
    else:
        out_shape = jax.ShapeDtypeStruct((nb * s, N), out_dtype)
        out_spec = pl.BlockSpec((tm, tn), lambda b, i, j: (b * nt + i, j))
    return pl.pallas_call(
        _mm_kernel,
        grid=(nb, nt, N // tn),
        in_specs=[pl.BlockSpec((tm, K), lambda b, i, j: (b * nt + i, 0)),
                  pl.BlockSpec((K, tn), lambda b, i, j: (0, j))],
        out_specs=out_spec,
        out_shape=out_shape,
        compiler_params=_cparams("parallel", "parallel", "arbitrary"),
        name="matmul",
    )(x, w)


def _mla_in_kernel(x_ref, w_ref, qg_ref, kvg_ref, wqb_ref, tc_ref, ts_ref, q_ref, ckv_ref, kpe_ref):
    h = _dot(x_ref[...].astype(BF16), w_ref[...])
    tc = tc_ref[...]
    ts = ts_ref[...]
    qa = h[:, :MLA_Q_RANK]
    qn = qa * lax.rsqrt(jnp.mean(qa * qa, -1, keepdims=True) + RMS_EPS) * qg_ref[...]
    q2 = _dot(qn.astype(BF16), wqb_ref[...])
    hp = MLA_HEADS * HEAD_PAD
    for hd in range(MLA_HEADS):
        lo = hd * HEAD_PAD
        q0 = q2[:, lo:lo + HEAD_PAD]
        q1 = q2[:, hp + lo:hp + lo + HEAD_PAD]
        q_ref[:, lo:lo + HEAD_PAD] = ((q0 * tc + q1 * ts) * ATT_SCALE).astype(BF16)
    c0 = MLA_Q_RANK
    ckv = h[:, c0:c0 + MLA_KV_RANK]
    ckv_ref[...] = ckv * lax.rsqrt(jnp.mean(ckv * ckv, -1, keepdims=True) + RMS_EPS) * kvg_ref[...]
    c1 = c0 + MLA_KV_RANK
    kpe_ref[...] = h[:, c1:c1 + LANES] * tc + h[:, c1 + LANES:c1 + 2 * LANES] * ts


def _mla_in(x, lw, tc, ts, *, tm=512):
    T = x.shape[0]
    tm = min(tm, T)
    hp = MLA_HEADS * HEAD_PAD
    nw = lw["w_mla"].shape[1]
    return pl.pallas_call(
        _mla_in_kernel,
        grid=(T // tm,),
        in_specs=[pl.BlockSpec((tm, D_MODEL), lambda i: (i, 0)),
                  _full((D_MODEL, nw)), _full((1, MLA_Q_RANK)), _full((1, MLA_KV_RANK)),
                  _full((MLA_Q_RANK, 2 * hp)),
                  pl.BlockSpec((tm, LANES), lambda i: (i, 0)),
                  pl.BlockSpec((tm, LANES), lambda i: (i, 0))],
        out_specs=[pl.BlockSpec((tm, hp), lambda i: (i, 0)),
                   pl.BlockSpec((tm, MLA_KV_RANK), lambda i: (i, 0)),
                   pl.BlockSpec((tm, LANES), lambda i: (i, 0))],
        out_shape=[jax.ShapeDtypeStruct((T, hp), BF16),
                   jax.ShapeDtypeStruct((T, MLA_KV_RANK), F32),
                   jax.ShapeDtypeStruct((T, LANES), F32)],
        compiler_params=_cparams("parallel"),
        name="mla_in",
    )(x, lw["w_mla"], lw["q_norm"], lw["kv_norm"], lw["w_qb"], tc, ts)


def _kv_expand_kernel(ckv_ref, kpe_ref, wk_ref, wv_ref, k_ref, v_ref):
    c = ckv_ref[...].astype(BF16)
    k = _dot(c, wk_ref[...])
    kpe = kpe_ref[...]
    for hd in range(MLA_HEADS):
        lo = hd * HEAD_PAD
        k_ref[:, lo:lo + HEAD_PAD] = (k[:, lo:lo + HEAD_PAD] + kpe).astype(BF16)
    v_ref[...] = _dot(c, wv_ref[...]).astype(BF16)


def _kv_expand(ckv, kpe, lw, *, tm=512):
    T = ckv.shape[0]
    tm = min(tm, T)
    while T % tm:
        tm //= 2
    hp = MLA_HEADS * HEAD_PAD
    hv = MLA_HEADS * MLA_V
    return pl.pallas_call(
        _kv_expand_kernel,
        grid=(T // tm,),
        in_specs=[pl.BlockSpec((tm, MLA_KV_RANK), lambda i: (i, 0)),
                  pl.BlockSpec((tm, LANES), lambda i: (i, 0)),
                  _full((MLA_KV_RANK, hp)), _full((MLA_KV_RANK, hv))],
        out_specs=[pl.BlockSpec((tm, hp), lambda i: (i, 0)),
                   pl.BlockSpec((tm, hv), lambda i: (i, 0))],
        out_shape=[jax.ShapeDtypeStruct((T, hp), BF16), jax.ShapeDtypeStruct((T, hv), BF16)],
        compiler_params=_cparams("parallel"),
        name="kv_expand",
    )(ckv, kpe, lw["w_k"], lw["w_v"])


def _attn_kernel(q_ref, k_ref, v_ref, o_ref, *, tq, nq, q_start, sk):
    lane = lax.broadcasted_iota(jnp.int32, (tq, LANES), 1)
    nt = (((1,), (1,)), ((), ()))
    for i in range(nq):
        q_lo = q_start + i * tq
        k_end = min(((q_lo + tq - 1) // CHUNK + 1) * CHUNK, sk)
        kw = -(-k_end // LANES) * LANES
        c0 = min((q_lo // CHUNK + 1) * CHUNK, sk) // LANES * LANES
        tail = kw - c0
        if tail:
            q_chunk = (q_lo + lax.broadcasted_iota(jnp.int32, (tq, tail), 0)) // CHUNK
            k_pos = c0 + lax.broadcasted_iota(jnp.int32, (tq, tail), 1)
            visible = k_pos // CHUNK <= q_chunk
            if kw > sk:
                visible = visible & (k_pos < sk)
        outs = []
        for hh in range(2):
            hs = slice(hh * HEAD_PAD, (hh + 1) * HEAD_PAD)
            q = q_ref[i * tq:(i + 1) * tq, hs]
            parts = []
            if c0:
                parts.append(lax.dot_general(q, k_ref[0:c0, hs], nt, preferred_element_type=F32))
            if tail:
                s_t = lax.dot_general(q, k_ref[c0:kw, hs], nt, preferred_element_type=F32)
                parts.append(jnp.where(visible, s_t, NEG_INF))
            m = jnp.max(parts[0], -1, keepdims=True)
            for s_ in parts[1:]:
                m = jnp.maximum(m, jnp.max(s_, -1, keepdims=True))
            l = None
            acc = None
            lo = 0
            for s_ in parts:
                p = jnp.exp(s_ - m)
                ps = jnp.sum(p, -1, keepdims=True)
                pv = _dot(p.astype(BF16), v_ref[lo:lo + s_.shape[1], :])
                l = ps if l is None else l + ps
                acc = pv if acc is None else acc + pv
                lo += s_.shape[1]
            outs.append(acc / l)
        o_ref[i * tq:(i + 1) * tq, :] = jnp.where(lane < MLA_V, outs[0], outs[1])


def _attention(q, k, v, nb, sq, skp, sk, q_start, *, tq=256):
    tq = min(tq, sq)
    kern = functools.partial(_attn_kernel, tq=tq, nq=sq // tq, q_start=q_start, sk=sk)
    return pl.pallas_call(
        kern,
        grid=(nb, MLA_HEADS // 2),
        in_specs=[pl.BlockSpec((sq, 2 * HEAD_PAD), lambda b, j: (b, j)),
                  pl.BlockSpec((skp, 2 * HEAD_PAD), lambda b, j: (b, j)),
                  pl.BlockSpec((skp, 2 * MLA_V), lambda b, j: (b, j))],
        out_specs=pl.BlockSpec((sq, 2 * MLA_V), lambda b, j: (b, j)),
        out_shape=jax.ShapeDtypeStruct((nb * sq, MLA_HEADS * MLA_V), F32),
        compiler_params=_cparams("parallel", "parallel"),
        name="attention",
    )(q, k, v)


def _head_sum(z, ones_bd):
    hi = z.astype(BF16)
    lo = (z - hi.astype(F32)).astype(BF16)
    return _dot(hi, ones_bd) + _dot(lo, ones_bd)


def _rwkv_pre_kernel(p_ref, prev_ref, shift_ref, mu_ref, w0_ref, wup_ref, a0_ref, aup_ref, gup_ref,
                     kk_ref, ka_ref, rk_ref, ones_ref,
                     r_out, w_out, k_out, v_out, kk_out, b_out, g_out, bonus_out):
    p = p_ref[...]
    first = pl.program_id(1) == 0
    prev_row = jnp.where(first, shift_ref[0], prev_ref[7:8, :])
    row = lax.broadcasted_iota(jnp.int32, p.shape, 0)
    prev = jnp.where(row == 0, prev_row, pltpu.roll(p, 1, 0))
    ps = p + (prev - p) * mu_ref[...]
    d = RWKV_DIM
    r = ps[:, 0:d]
    k = ps[:, d:2 * d]
    v = ps[:, 2 * d:3 * d]
    wa = ps[:, 3 * d:3 * d + LANES]
    gd = ps[:, 3 * d + LANES:3 * d + 2 * LANES]
    ones_bd = ones_ref[...]
    wlin = w0_ref[...] + _dot(jnp.tanh(wa).astype(BF16), wup_ref[...])
    z = -wlin
    w = -(jnp.maximum(z, 0.0) + jnp.log1p(jnp.exp(-jnp.abs(z)))) - 0.5
    w_out[...] = jnp.exp(-jnp.exp(w))
    a = _sigmoid(a0_ref[...] + _dot(wa.astype(BF16), aup_ref[...]))
    g_out[...] = _dot(_sigmoid(gd).astype(BF16), gup_ref[...])
    kk = k * kk_ref[...]
    kkn = kk * lax.rsqrt(_head_sum(kk * kk, ones_bd) + 1e-12)
    kk_out[...] = kkn
    b_out[...] = kkn * a
    kn = k * (1.0 + (a - 1.0) * ka_ref[...])
    r_out[...] = r
    k_out[...] = kn
    v_out[...] = v
    bonus_out[...] = _head_sum(r * kn * rk_ref[...], ones_bd) * v


def _rwkv_pre(rw, shift_prev, lw, nb, s, *, tm=256):
    tm = min(tm, s)
    tps = s // tm
    d = RWKV_DIM
    vec = lambda n: _full((1, n))
    bm = pl.BlockSpec((tm, d), lambda b, i: (b * tps + i, 0))
    tmaj = pl.BlockSpec((tm, d), lambda b, i: (i, b))
    bm_shape = jax.ShapeDtypeStruct((nb * s, d), F32)
    tm_shape = jax.ShapeDtypeStruct((s, nb * d), F32)
    return pl.pallas_call(
        _rwkv_pre_kernel,
        grid=(nb, tps),
        in_specs=[pl.BlockSpec((tm, RWKV_PROJ), lambda b, i: (b * tps + i, 0)),
                  pl.BlockSpec((8, RWKV_PROJ), lambda b, i: (jnp.maximum((b * tps + i) * (tm // 8) - 1, 0), 0)),
                  pl.BlockSpec((1, 1, RWKV_PROJ), lambda b, i: (b, 0, 0)),
                  vec(RWKV_PROJ), vec(d), _full((LANES, d)), vec(d), _full((LANES, d)),
                  _full((LANES, d)), vec(d), vec(d), vec(d), _full((d, d))],
        out_specs=[tmaj] * 6 + [bm] * 2,
        out_shape=[tm_shape] * 6 + [bm_shape] * 2,
        compiler_params=_cparams("parallel", "parallel"),
        name="rwkv_pre",
    )(rw, rw, shift_prev.reshape(nb, 1, RWKV_PROJ), lw["mu"], lw["w0"], lw["w_up"], lw["a0"], lw["a_up"],
      lw["g_up"], lw["k_k"], lw["k_a"], lw["r_k"], lw["ones_bd"])


RWKV_VC = 16


def _rwkv_scan_kernel(r_ref, w_ref, k_ref, v_ref, kk_ref, b_ref, s0_ref, o_ref, sT_ref, st_ref, *, tt):
    n = RWKV_HEAD

    @pl.when(pl.program_id(0) == 0)
    def _():
        st_ref[...] = s0_ref[...]

    def step(t, carry):
        for c in range(n // RWKV_VC):
            vs = slice(c * RWKV_VC, (c + 1) * RWKV_VC)
            parts = [None] * 4
            for q in range(n):
                term = st_ref[q, vs, :] * kk_ref[t, q:q + 1, :]
                parts[q % 4] = term if parts[q % 4] is None else parts[q % 4] + term
            sa = -((parts[0] + parts[1]) + (parts[2] + parts[3]))
            vt = v_ref[t, vs, :]
            outs = [None] * 4
            for q in range(n):
                s_new = (st_ref[q, vs, :] * w_ref[t, q:q + 1, :] + sa * b_ref[t, q:q + 1, :]
                         + vt * k_ref[t, q:q + 1, :])
                st_ref[q, vs, :] = s_new
                term = s_new * r_ref[t, q:q + 1, :]
                outs[q % 4] = term if outs[q % 4] is None else outs[q % 4] + term
            o_ref[t, vs, :] = (outs[0] + outs[1]) + (outs[2] + outs[3])
        return carry

    lax.fori_loop(0, tt, step, 0)

    @pl.when(pl.program_id(0) == pl.num_programs(0) - 1)
    def _():
        sT_ref[...] = st_ref[...]


def _rwkv_scan(seqs, s0, s, *, tt=16):
    L = s0.shape[-1]
    n = RWKV_HEAD
    tt = min(tt, s)
    blk = pl.BlockSpec((tt, n, L), lambda i: (i, 0, 0))
    kern = functools.partial(_rwkv_scan_kernel, tt=tt)
    return pl.pallas_call(
        kern,
        grid=(s // tt,),
        in_specs=[blk] * 6 + [_full((n, n, L))],
        out_specs=[blk, _full((n, n, L))],
        out_shape=[jax.ShapeDtypeStruct((s, n, L), F32), jax.ShapeDtypeStruct((n, n, L), F32)],
        scratch_shapes=[pltpu.VMEM((n, n, L), F32)],
        compiler_params=_cparams("arbitrary"),
        name="rwkv_scan",
    )(*seqs, s0)


def _s5_params_kernel(are_ref, aim_ref, ldt_ref, bre_ref, bim_ref, abre_ref, abim_ref, bbre_ref, bbim_ref):
    lr = are_ref[...]
    li = aim_ref[...]
    dt = jnp.exp(ldt_ref[...])
    mag = jnp.exp(lr * dt)
    ab_re = mag * jnp.cos(li * dt)
    ab_im = mag * jnp.sin(li * dt)
    den = lr * lr + li * li
    f_re = ((ab_re - 1.0) * lr + ab_im * li) / den
    f_im = (ab_im * lr - (ab_re - 1.0) * li) / den
    abre_ref[...] = ab_re
    abim_ref[...] = ab_im
    for i in range(SSM_GROUP):
        br = bre_ref[i]
        bi = bim_ref[i]
        bbre_ref[i] = f_re * br - f_im * bi
        bbim_ref[i] = f_re * bi + f_im * br


def _s5_params(a_re, a_im, log_dt, b_re, b_im):
    g, n = a_re.shape
    gn = jax.ShapeDtypeStruct((g, n), F32)
    ign = jax.ShapeDtypeStruct((SSM_GROUP, g, n), F32)
    return pl.pallas_call(
        _s5_params_kernel,
        out_shape=[gn, gn, ign, ign],
        name="s5_params",
    )(a_re, a_im, log_dt.reshape(g, 1), jnp.transpose(b_re, (2, 0, 1)), jnp.transpose(b_im, (2, 0, 1)))


def _gelu_tanh(x):
    return 0.5 * x * (1.0 + jnp.tanh(math.sqrt(2.0 / math.pi) * (x + 0.044715 * (x * x * x))))


def _s5_kernel(u_ref, bre_ref, bim_ref, cre_ref, cim_ref, d_ref, are_ref, aim_ref, h0re_ref, h0im_ref,
               gw_ref, gb_ref, y_ref, hTre_ref, hTim_ref, hre_s, him_s, sre_s, sim_s, yy_s, *, tt, nb):
    cw = SSM_CH // SSM_CHUNKS

    @pl.when(pl.program_id(0) == 0)
    def _():
        hre_s[...] = h0re_ref[...]
        him_s[...] = h0im_ref[...]

    for c in range(SSM_CHUNKS):
        uc = u_ref[:, c * LANES:(c + 1) * LANES]
        ub = uc.astype(BF16)
        sre_s[...] = _dot(ub, bre_ref[c])
        sim_s[...] = _dot(ub, bim_ref[c])
        a_re = jnp.broadcast_to(are_ref[:, c * cw:(c + 1) * cw], (nb, cw))
        a_im = jnp.broadcast_to(aim_ref[:, c * cw:(c + 1) * cw], (nb, cw))

        def step(t, carry, a_re=a_re, a_im=a_im):
            hr, hi = carry
            rows = pl.ds(pl.multiple_of(t * nb, nb), nb)
            nr = a_re * hr - a_im * hi + sre_s[rows, :]
            ni = a_re * hi + a_im * hr + sim_s[rows, :]
            sre_s[rows, :] = nr
            sim_s[rows, :] = ni
            return nr, ni

        hr, hi = lax.fori_loop(0, tt, step, (hre_s[:, c * cw:(c + 1) * cw], him_s[:, c * cw:(c + 1) * cw]))
        hre_s[:, c * cw:(c + 1) * cw] = hr
        him_s[:, c * cw:(c + 1) * cw] = hi
        yc = _dot(sre_s[...].astype(BF16), cre_ref[c]) - _dot(sim_s[...].astype(BF16), cim_ref[c])
        yy_s[:, c * LANES:(c + 1) * LANES] = yc + d_ref[:, c * LANES:(c + 1) * LANES] * uc

    y = _gelu_tanh(yy_s[...])
    z = _dot(y.astype(BF16), gw_ref[...]) + gb_ref[...]
    y_ref[...] = z[:, :SSM_DIM] * _sigmoid(z[:, SSM_DIM:])

    @pl.when(pl.program_id(0) == pl.num_programs(0) - 1)
    def _():
        hTre_ref[...] = hre_s[...]
        hTim_ref[...] = him_s[...]


def _s5(u_tm, h0_re, h0_im, lw, nb, s, *, rows=512):
    tt = max(min(rows // nb, s), 1)
    R = tt * nb
    cw = SSM_CH // SSM_CHUNKS
    kern = functools.partial(_s5_kernel, tt=tt, nb=nb)
    return pl.pallas_call(
        kern,
        grid=(s // tt,),
        in_specs=[pl.BlockSpec((R, SSM_DIM), lambda i: (i, 0)),
                  _full((SSM_CHUNKS, LANES, cw)), _full((SSM_CHUNKS, LANES, cw)),
                  _full((SSM_CHUNKS, cw, LANES)), _full((SSM_CHUNKS, cw, LANES)),
                  _full((1, SSM_DIM)), _full((1, SSM_CH)), _full((1, SSM_CH)),
                  _full((nb, SSM_CH)), _full((nb, SSM_CH)),
                  _full((SSM_DIM, 2 * SSM_DIM)), _full((1, 2 * SSM_DIM))],
        out_specs=[pl.BlockSpec((R, SSM_DIM), lambda i: (i, 0)), _full((nb, SSM_CH)), _full((nb, SSM_CH))],
        out_shape=[jax.ShapeDtypeStruct((s * nb, SSM_DIM), F32),
                   jax.ShapeDtypeStruct((nb, SSM_CH), F32), jax.ShapeDtypeStruct((nb, SSM_CH), F32)],
        scratch_shapes=[pltpu.VMEM((nb, SSM_CH), F32), pltpu.VMEM((nb, SSM_CH), F32),
                        pltpu.VMEM((R, cw), F32), pltpu.VMEM((R, cw), F32), pltpu.VMEM((R, SSM_DIM), F32)],
        compiler_params=_cparams("arbitrary"),
        name="s5",
    )(u_tm, lw["s5_bre"], lw["s5_bim"], lw["s5_cre"], lw["s5_cim"], lw["s5_d"], lw["s5_are"], lw["s5_aim"],
      h0_re, h0_im, lw["glu_w"], lw["glu_b"])


def _merge_kernel(x_ref, ya_ref, o_ref, bonus_ref, g_ref, yc_ref, wg_ref, lng_ref, lnb_ref, ones_ref,
                  wa_ref, wb_ref, wc_ref, wo_ref, ln1g_ref, ln1b_ref, out_ref):
    ones_bd = ones_ref[...]
    o = o_ref[...]
    inv_n = 1.0 / RWKV_HEAD
    mean = _head_sum(o, ones_bd) * inv_n
    oc = o - mean
    var = _head_sum(oc * oc, ones_bd) * inv_n
    yb = (oc * lax.rsqrt(var + RWKV_LN_EPS) * lng_ref[...] + lnb_ref[...] + bonus_ref[...]) * g_ref[...]
    d = D_MODEL
    x = x_ref[...]
    xb = x.astype(BF16)
    merged = None
    for j, (y, w_ref) in enumerate(((ya_ref[...], wa_ref), (yb, wb_ref), (yc_ref[...], wc_ref))):
        gate = _sigmoid(_dot(xb, wg_ref[:, j * d:(j + 1) * d]))
        term = gate * _dot(y.astype(BF16), w_ref[...])
        merged = term if merged is None else merged + term
    y = DN_ALPHA * x + _dot(merged.astype(BF16), wo_ref[...])
    out_ref[...] = _layer_norm(y, ln1g_ref[...], ln1b_ref[...])


def _merge(x, ya, o_tm, bonus, g, yc_tm, lw, nb, s, *, tm=256):
    tm = min(tm, s)
    nt = s // tm
    d = D_MODEL
    h = RWKV_DIM
    row = lambda w: pl.BlockSpec((tm, w), lambda b, i: (b * nt + i, 0))
    tmaj = lambda w: pl.BlockSpec((tm, w), lambda b, i: (i, b))
    vec = lambda n: _full((1, n))
    return pl.pallas_call(
        _merge_kernel,
        grid=(nb, nt),
        in_specs=[row(d), row(h), tmaj(h), row(h), row(h), tmaj(SSM_DIM),
                  _full((d, 3 * d)), vec(h), vec(h), _full((h, h)),
                  _full((h, d)), _full((h, d)), _full((h, d)), _full((d, d)), vec(d), vec(d)],
        out_specs=row(d),
        out_shape=jax.ShapeDtypeStruct((nb * s, d), F32),
        compiler_params=_cparams("parallel", "parallel"),
        name="merge",
    )(x, ya, o_tm, bonus, g, yc_tm, lw["w_gates"], lw["lnx_g"], lw["lnx_b"], lw["ones_bd"],
      lw["w_br_a"], lw["w_br_b"], lw["w_br_c"], lw["w_out"], lw["ln1_g"], lw["ln1_b"])


INFO_GATE, INFO_EID, INFO_RANK = 0, 4, 8


def _router_kernel(x_ref, whi_ref, wlo_ref, b_ref, tri_ref, info_ref, cnt_ref, base_s):
    @pl.when(pl.program_id(0) == 0)
    def _():
        base_s[...] = jnp.zeros_like(base_s)

    x = x_ref[...]
    xh = x.astype(BF16)
    xl = (x - xh.astype(F32)).astype(BF16)
    logits = _dot(xh, whi_ref[...]) + _dot(xl, whi_ref[...]) + _dot(xh, wlo_ref[...]) + b_ref[...]
    lane_i = lax.broadcasted_iota(jnp.int32, logits.shape, 1)
    lane = lane_i.astype(F32)
    vals, sels, ids = [], [], []
    for _ in range(TOP_K):
        m = jnp.max(logits, -1, keepdims=True)
        idx = jnp.min(jnp.where(logits == m, lane, float(LANES)), -1, keepdims=True)
        sel = lane == idx
        vals.append(m)
        sels.append(sel)
        ids.append(idx)
        logits = jnp.where(sel, -3e38, logits)
    es = [jnp.exp(v - vals[0]) for v in vals]
    den = es[0] + es[1] + es[2] + es[3]
    chosen = jnp.zeros_like(logits)
    for sel in sels:
        chosen = chosen + jnp.where(sel, 1.0, 0.0)
    before = _dot(tri_ref[...], chosen.astype(BF16)) + base_s[...]
    info = jnp.zeros_like(logits)
    for k in range(TOP_K):
        rank = jnp.sum(jnp.where(sels[k], before, 0.0), -1, keepdims=True)
        info = (info + jnp.where(lane_i == INFO_GATE + k, es[k] / den, 0.0)
                + jnp.where(lane_i == INFO_EID + k, ids[k], 0.0)
                + jnp.where(lane_i == INFO_RANK + k, rank, 0.0))
    info_ref[...] = info
    base_s[...] += jnp.sum(chosen, 0, keepdims=True)
    cnt_ref[...] = base_s[...]


def _router(x, lw, *, tm=512):
    T = x.shape[0]
    tm = min(tm, T)
    tri = jnp.asarray(np.tril(np.ones((tm, tm), np.float32), -1), dtype=BF16)
    return pl.pallas_call(
        _router_kernel,
        grid=(T // tm,),
        in_specs=[pl.BlockSpec((tm, D_MODEL), lambda i: (i, 0)),
                  _full((D_MODEL, LANES)), _full((D_MODEL, LANES)), _full((1, LANES)), _full((tm, tm))],
        out_specs=[pl.BlockSpec((tm, LANES), lambda i: (i, 0)), _full((1, LANES))],
        out_shape=[jax.ShapeDtypeStruct((T, LANES), F32), jax.ShapeDtypeStruct((1, LANES), F32)],
        scratch_shapes=[pltpu.VMEM((1, LANES), F32)],
        compiler_params=_cparams("arbitrary"),
        name="router",
    )(x, lw["router_hi"], lw["router_lo"], lw["router_b"], tri)


def _moe_schedule(info, cnt, tm_e, nt):
    eid = info[:, INFO_EID:INFO_EID + TOP_K].astype(jnp.int32)
    rank = info[:, INFO_RANK:INFO_RANK + TOP_K].astype(jnp.int32)
    counts = cnt[0, :N_EXPERTS].astype(jnp.int32)
    gsz = (counts + tm_e - 1) // tm_e * tm_e
    gend = jnp.cumsum(gsz)
    goff = gend - gsz
    onehot = eid[..., None] == jnp.arange(N_EXPERTS, dtype=jnp.int32)
    pos = jnp.sum(jnp.where(onehot, goff, 0), -1) + rank
    tile_start = jnp.arange(nt, dtype=jnp.int32) * tm_e
    tile_e = jnp.minimum(jnp.sum((gend[None, :] <= tile_start[:, None]).astype(jnp.int32), -1), N_EXPERTS - 1)
    n_valid = (gend[-1] // tm_e).reshape(1)
    return pos.reshape(-1).astype(jnp.int32), tile_e.astype(jnp.int32), n_valid.astype(jnp.int32)


def _dispatch_kernel(pos_ref, x_ref, xs_in_ref, xs_ref, sem, *, tm):
    del xs_in_ref

    def issue(t, carry):
        for k in range(TOP_K):
            p = pos_ref[t * TOP_K + k]
            pltpu.make_async_copy(x_ref.at[pl.ds(t, 1)], xs_ref.at[pl.ds(p, 1)], sem).start()
        return carry

    lax.fori_loop(0, tm, issue, 0)
    for k in range(TOP_K):
        pltpu.make_async_copy(x_ref, xs_ref.at[pl.ds(0, tm)], sem).wait()


def _dispatch(x, pos, rows, *, tm=256):
    T, d = x.shape
    tm = min(tm, T)
    kern = functools.partial(_dispatch_kernel, tm=tm)
    return pl.pallas_call(
        kern,
        grid=(T // tm,),
        in_specs=[pl.BlockSpec((tm * TOP_K,), lambda i: (i,), memory_space=pltpu.SMEM),
                  pl.BlockSpec((tm, d), lambda i: (i, 0)),
                  pl.BlockSpec(memory_space=pl.ANY)],
        out_specs=pl.BlockSpec(memory_space=pl.ANY),
        out_shape=jax.ShapeDtypeStruct((rows, d), F32),
        scratch_shapes=[pltpu.SemaphoreType.DMA],
        input_output_aliases={2: 0},
        compiler_params=_cparams("arbitrary"),
        name="moe_dispatch",
    )(pos, x, jnp.zeros((rows, d), F32))


def _expert_kernel(te_ref, nv_ref, xs_ref, wgu_ref, bgu_ref, wd_ref, bd_ref, ys_ref):
    del te_ref

    @pl.when(pl.program_id(0) < nv_ref[0])
    def _():
        h = _dot(xs_ref[...].astype(BF16), wgu_ref[0]) + bgu_ref[0]
        hg = jnp.minimum(h[:, :D_FF], SWIGLU_LIMIT)
        hl = jnp.clip(h[:, D_FF:], -SWIGLU_LIMIT, SWIGLU_LIMIT)
        act = hg * _sigmoid(SWIGLU_ALPHA * hg) * (hl + 1.0)
        ys_ref[...] = _dot(act.astype(BF16), wd_ref[0]) + bd_ref[0]

    @pl.when(pl.program_id(0) >= nv_ref[0])
    def _():
        ys_ref[...] = jnp.zeros_like(ys_ref)


def _experts(xs, tile_e, n_valid, lw, tm_e):
    rows, d = xs.shape
    nt = rows // tm_e
    tile = lambda j, te, nv: (jnp.minimum(j, nv[0] - 1), 0)
    wsel = lambda j, te, nv: (te[jnp.minimum(j, nv[0] - 1)], 0, 0)
    return pl.pallas_call(
        _expert_kernel,
        grid_spec=pltpu.PrefetchScalarGridSpec(
            num_scalar_prefetch=2,
            grid=(nt,),
            in_specs=[pl.BlockSpec((tm_e, d), tile),
                      pl.BlockSpec((1, d, 2 * D_FF), wsel), pl.BlockSpec((1, 1, 2 * D_FF), wsel),
                      pl.BlockSpec((1, D_FF, d), wsel), pl.BlockSpec((1, 1, d), wsel)],
            out_specs=pl.BlockSpec((tm_e, d), lambda j, te, nv: (j, 0))),
        out_shape=jax.ShapeDtypeStruct((rows, d), F32),
        compiler_params=_cparams("arbitrary"),
        name="moe_experts",
    )(tile_e, n_valid, xs, lw["w_gu"], lw["b_gu"], lw["w_down"], lw["b_down"])


def _combine_kernel(pos_ref, x_ref, info_ref, ys_ref, ln2g_ref, ln2b_ref, out_ref, buf, sem, *, tm):
    def issue(t, carry):
        for k in range(TOP_K):
            p = pos_ref[t * TOP_K + k]
            pltpu.make_async_copy(ys_ref.at[pl.ds(p, 1)], buf.at[k, pl.ds(t, 1)], sem).start()
        return carry

    lax.fori_loop(0, tm, issue, 0)
    for k in range(TOP_K):
        pltpu.make_async_copy(ys_ref.at[pl.ds(0, tm)], buf.at[k], sem).wait()
    info = info_ref[...]
    lane = lax.broadcasted_iota(jnp.int32, info.shape, 1)
    acc = DN_ALPHA * x_ref[...]
    for k in range(TOP_K):
        gate = jnp.sum(jnp.where(lane == INFO_GATE + k, info, 0.0), -1, keepdims=True)
        acc = acc + gate * buf[k]
    out_ref[...] = _layer_norm(acc, ln2g_ref[...], ln2b_ref[...])


def _combine(x, info, pos, ys, lw, *, tm=256):
    T, d = x.shape
    tm = min(tm, T)
    kern = functools.partial(_combine_kernel, tm=tm)
    return pl.pallas_call(
        kern,
        grid=(T // tm,),
        in_specs=[pl.BlockSpec((tm * TOP_K,), lambda i: (i,), memory_space=pltpu.SMEM),
                  pl.BlockSpec((tm, d), lambda i: (i, 0)),
                  pl.BlockSpec((tm, LANES), lambda i: (i, 0)),
                  pl.BlockSpec(memory_space=pl.ANY),
                  _full((1, d)), _full((1, d))],
        out_specs=pl.BlockSpec((tm, d), lambda i: (i, 0)),
        out_shape=jax.ShapeDtypeStruct((T, d), F32),
        scratch_shapes=[pltpu.VMEM((TOP_K, tm, d), F32), pltpu.SemaphoreType.DMA],
        compiler_params=_cparams("arbitrary"),
        name="moe_combine",
    )(pos, x, info, ys, lw["ln2_g"], lw["ln2_b"])


def _moe(x, lw):
    T = x.shape[0]
    tm_e = 512 if T >= 4096 else 128
    nt = T * TOP_K // tm_e + N_EXPERTS
    info, cnt = _router(x, lw)
    pos, tile_e, n_valid = _moe_schedule(info, cnt, tm_e, nt)
    xs = _dispatch(x, pos, nt * tm_e)
    ys = _experts(xs, tile_e, n_valid, lw, tm_e)
    return _combine(x, info, pos, ys, lw)


def _ones_block_diag():
    idx = np.arange(RWKV_DIM) // RWKV_HEAD
    return jnp.asarray((idx[:, None] == idx[None, :]).astype(np.float32), dtype=BF16)


def _deinterleave_matrix():
    n = 2 * D_FF
    src = np.concatenate([np.arange(0, n, 2), np.arange(1, n, 2)])
    return jnp.asarray((np.arange(n)[:, None] == src[None, :]).astype(np.float32), dtype=BF16)


def _block_diag(x):
    C, G, r, c = x.shape
    eye = jnp.eye(G, dtype=x.dtype)
    return jnp.einsum("cgij,gh->cgihj", x, eye).reshape(C, G * r, G * c)


def _prep_layer(P, l):
    f = lambda name: P[name][l]
    lw = {}
    w_in = f("w_in")
    offs = np.cumsum((0,) + IN_SIZES)
    cols = lambda j: w_in[:, offs[j]:offs[j + 1]]
    w_q, w_kv = cols(0), cols(1)
    w_ckv, w_kpe = w_kv[:, :MLA_KV_RANK], w_kv[:, MLA_KV_RANK:]
    half = MLA_ROPE // 2
    zpad = jnp.zeros((D_MODEL, LANES - MLA_ROPE), F32)
    w_kpe_rot = jnp.concatenate([-w_kpe[:, half:], w_kpe[:, :half]], 1)
    lw["w_mla"] = jnp.concatenate([w_q, w_ckv, w_kpe, zpad, w_kpe_rot, zpad], 1).astype(BF16)
    lw["w_rw"] = cols(2).astype(BF16)
    lw["w_su"] = cols(3).astype(BF16)
    lw["w_gates"] = jnp.concatenate([cols(4), cols(5), cols(6)], 1).astype(BF16)
    lw["q_norm"] = f("mla_q_a_norm").reshape(1, -1)
    lw["kv_norm"] = f("mla_kv_a_norm").reshape(1, -1)
    wqb = f("mla_w_q_b").reshape(MLA_Q_RANK, MLA_HEADS, MLA_NOPE + MLA_ROPE)
    nope, x1, x2 = wqb[..., :MLA_NOPE], wqb[..., MLA_NOPE:MLA_NOPE + half], wqb[..., MLA_NOPE + half:]
    z32 = jnp.zeros_like(wqb[..., :HEAD_PAD - MLA_NOPE - MLA_ROPE])
    plain = jnp.concatenate([x1, x2, nope, z32], -1).reshape(MLA_Q_RANK, -1)
    rot = jnp.concatenate([-x2, x1, jnp.zeros_like(nope), z32], -1).reshape(MLA_Q_RANK, -1)
    lw["w_qb"] = jnp.concatenate([plain, rot], 1).astype(BF16)
    wkvb = f("mla_w_kv_b").reshape(MLA_KV_RANK, MLA_HEADS, MLA_NOPE + MLA_V)
    k_nope, v = wkvb[..., :MLA_NOPE], wkvb[..., MLA_NOPE:]
    zk = jnp.zeros_like(k_nope[..., :MLA_ROPE])
    lw["w_k"] = jnp.concatenate([zk, k_nope, zk], -1).reshape(MLA_KV_RANK, -1).astype(BF16)
    lw["w_v"] = v.reshape(MLA_KV_RANK, -1).astype(BF16)
    row = lambda name: f(name).reshape(1, -1)
    lw["mu"] = row("rwkv_mu")
    lw["w0"] = row("rwkv_w0")
    z64 = jnp.zeros((64, RWKV_DIM), F32)
    lw["w_up"] = jnp.concatenate([f("rwkv_w_up"), z64], 0).astype(BF16)
    lw["a_up"] = jnp.concatenate([z64, f("rwkv_a_up")], 0).astype(BF16)
    lw["a0"] = row("rwkv_a0")
    lw["g_up"] = f("rwkv_g_up").astype(BF16)
    lw["k_k"] = row("rwkv_k_k")
    lw["k_a"] = row("rwkv_k_a")
    lw["r_k"] = row("rwkv_r_k")
    lw["lnx_g"] = row("rwkv_lnx_g")
    lw["lnx_b"] = row("rwkv_lnx_b")
    lw["ones_bd"] = _ones_block_diag()
    ab_re, ab_im, bb_re, bb_im = _s5_params(f("ssm_a_re"), f("ssm_a_im"), f("ssm_log_dt"),
                                            f("ssm_b_re"), f("ssm_b_im"))
    gpc = SSM_GROUPS // SSM_CHUNKS
    chunked = lambda t: t.reshape(SSM_CHUNKS, gpc, t.shape[1], t.shape[2])
    lw["s5_bre"] = _block_diag(chunked(jnp.transpose(bb_re, (1, 0, 2)))).astype(BF16)
    lw["s5_bim"] = _block_diag(chunked(jnp.transpose(bb_im, (1, 0, 2)))).astype(BF16)
    lw["s5_cre"] = _block_diag(chunked(jnp.transpose(f("ssm_c_re"), (0, 2, 1)))).astype(BF16)
    lw["s5_cim"] = _block_diag(chunked(jnp.transpose(f("ssm_c_im"), (0, 2, 1)))).astype(BF16)
    lw["s5_d"] = row("ssm_d")
    lw["s5_are"] = ab_re.reshape(1, -1)
    lw["s5_aim"] = ab_im.reshape(1, -1)
    lw["glu_w"] = f("ssm_glu_w").astype(BF16)
    lw["glu_b"] = row("ssm_glu_b")
    for name in ("w_br_a", "w_br_b", "w_br_c", "w_out"):
        lw[name] = f(name).astype(BF16)
    for name in ("ln1_g", "ln1_b", "ln2_g", "ln2_b"):
        lw[name] = row(name)
    rw_ = jnp.pad(f("router_w"), ((0, 0), (0, LANES - N_EXPERTS)))
    hi = rw_.astype(BF16)
    lw["router_hi"] = hi
    lw["router_lo"] = (rw_ - hi.astype(F32)).astype(BF16)
    lw["router_b"] = jnp.pad(f("router_b"), (0, LANES - N_EXPERTS), constant_values=NEG_INF).reshape(1, -1)
    wgu = f("exp_w_gu").reshape(N_EXPERTS * D_MODEL, 2 * D_FF)
    wgu = _matmul(wgu, _deinterleave_matrix(), 1, wgu.shape[0], out_dtype=BF16)
    lw["w_gu"] = wgu.reshape(N_EXPERTS, D_MODEL, 2 * D_FF)
    bgu = f("exp_b_gu")
    lw["b_gu"] = jnp.concatenate([bgu[..., 0::2], bgu[..., 1::2]], -1).reshape(N_EXPERTS, 1, 2 * D_FF)
    lw["w_down"] = f("exp_w_down").astype(BF16)
    lw["b_down"] = f("exp_b_down").reshape(N_EXPERTS, 1, D_MODEL)
    return lw


def _rope_tables(pos):
    half = MLA_ROPE // 2
    inv = ROPE_THETA ** (-jnp.arange(half, dtype=F32) / half)
    ang = pos.astype(F32)[:, None] * inv
    cos, sin = jnp.cos(ang), jnp.sin(ang)
    n = pos.shape[0]
    tc = jnp.concatenate([cos, cos, jnp.ones((n, MLA_NOPE), F32),
                          jnp.zeros((n, HEAD_PAD - MLA_NOPE - MLA_ROPE), F32)], 1)
    ts = jnp.concatenate([sin, sin, jnp.zeros((n, HEAD_PAD - MLA_ROPE), F32)], 1)
    return tc, ts


def _to_lanes(t, nb, s, lanes):
    t = t.reshape(s, nb * RWKV_HEADS, RWKV_HEAD).transpose(0, 2, 1)
    return jnp.pad(t, ((0, 0), (0, 0), (0, lanes - nb * RWKV_HEADS)))


def _layer(x, lw, past, nb, s, tables, att_tk=LANES):
    T = nb * s
    tc, ts = tables
    if past is None:
        start = 0
        shift_p = jnp.zeros((nb, RWKV_PROJ), F32)
        wkv_p = jnp.zeros((nb, RWKV_HEADS, RWKV_HEAD, RWKV_HEAD), F32)
        sre_p = jnp.zeros((nb, SSM_CH), F32)
        sim_p = jnp.zeros((nb, SSM_CH), F32)
    else:
        ckv_p, kpe_p, shift_p, wkv_p, sre_p, sim_p = past
        start = ckv_p.shape[1]
        sre_p = sre_p.reshape(nb, SSM_CH)
        sim_p = sim_p.reshape(nb, SSM_CH)

    q, ckv, kpe128 = _mla_in(x, lw, tc, ts)
    if past is None:
        sk = s
        ckv_all, kpe_all = ckv, kpe128
    else:
        sk = start + s
        ckv_all = jnp.concatenate([ckv_p, ckv.reshape(nb, s, -1)], 1)
        kpe_new = kpe128.reshape(nb, s, LANES)
        kpe_all = jnp.concatenate([jnp.pad(kpe_p, ((0, 0), (0, 0), (0, LANES - MLA_ROPE))), kpe_new], 1)
    skp = -(-sk // att_tk) * att_tk
    if skp != sk:
        ckv_all = jnp.pad(ckv_all.reshape(nb, sk, -1), ((0, 0), (0, skp - sk), (0, 0)))
        kpe_all = jnp.pad(kpe_all.reshape(nb, sk, -1), ((0, 0), (0, skp - sk), (0, 0)))
    k_pad, v_all = _kv_expand(ckv_all.reshape(nb * skp, -1), kpe_all.reshape(nb * skp, -1), lw)
    ya = _attention(q, k_pad, v_all, nb, s, skp, sk, start)

    rw = _matmul(x, lw["w_rw"], nb, s, tn=896)
    r, wdec, kn, v, kk, kb, g, bonus = _rwkv_pre(rw, shift_p, lw, nb, s)
    lanes = -(-nb * RWKV_HEADS // LANES) * LANES
    seqs = [_to_lanes(t, nb, s, lanes) for t in (r, wdec, kn, v, kk, kb)]
    s0 = jnp.transpose(wkv_p, (3, 2, 0, 1)).reshape(RWKV_HEAD, RWKV_HEAD, nb * RWKV_HEADS)
    s0 = jnp.pad(s0, ((0, 0), (0, 0), (0, lanes - nb * RWKV_HEADS)))
    o_l, sT = _rwkv_scan(seqs, s0, s)
    o_tm = o_l[:, :, :nb * RWKV_HEADS].transpose(0, 2, 1).reshape(s, nb * RWKV_DIM)
    wkv_n = (sT[:, :, :nb * RWKV_HEADS].reshape(RWKV_HEAD, RWKV_HEAD, nb, RWKV_HEADS).transpose(2, 3, 1, 0))
    shift_n = rw.reshape(nb, s, RWKV_PROJ)[:, -1]

    su_tm = _matmul(x, lw["w_su"], nb, s, time_major_out=True).reshape(s * nb, SSM_DIM)
    yc_tm, sre_n, sim_n = _s5(su_tm, sre_p, sim_p, lw, nb, s)
    yc_tm = yc_tm.reshape(s, nb * SSM_DIM)

    x1 = _merge(x, ya, o_tm, bonus, g, yc_tm, lw, nb, s)
    x2 = _moe(x1, lw)
    new = (ckv.reshape(nb, s, MLA_KV_RANK), kpe128[:, :MLA_ROPE].reshape(nb, s, MLA_ROPE), shift_n, wkv_n,
           sre_n.reshape(nb, SSM_GROUPS, SSM_STATE), sim_n.reshape(nb, SSM_GROUPS, SSM_STATE))
    return x2, new


def _trunk(x3, weights, caches):
    nb, s, d = x3.shape
    start = 0 if caches is None else caches[0].shape[2]
    tc, ts = _rope_tables(start + jnp.arange(s))
    tables = (jnp.tile(tc, (nb, 1)), jnp.tile(ts, (nb, 1)))
    x = x3.reshape(nb * s, d)
    new = []
    for l in range(len(weights)):
        past = None if caches is None else tuple(c[l] for c in caches)
        x, st = _layer(x, weights[l], past, nb, s, tables)
        new.append(st)
    return (x.reshape(nb, s, d),) + tuple(jnp.stack([st[j] for st in new]) for j in range(6))


def kernel(x_prompt, x_sample, cache_mla_ckv, cache_mla_kpe, state_rwkv_shift, state_rwkv_wkv, state_ssm_re, state_ssm_im, w_in, mla_q_a_norm, mla_w_q_b, mla_kv_a_norm, mla_w_kv_b, rwkv_mu, rwkv_w0, rwkv_w_up, rwkv_a0, rwkv_a_up, rwkv_g_up, rwkv_k_k, rwkv_k_a, rwkv_r_k, rwkv_lnx_g, rwkv_lnx_b, ssm_a_re, ssm_a_im, ssm_b_re, ssm_b_im, ssm_c_re, ssm_c_im, ssm_d, ssm_log_dt, ssm_glu_w, ssm_glu_b, w_br_a, w_br_b, w_br_c, w_out, ln1_g, ln1_b, router_w, router_b, exp_w_gu, exp_b_gu, exp_w_down, exp_b_down, ln2_g, ln2_b):
    P = dict(w_in=w_in, mla_q_a_norm=mla_q_a_norm, mla_w_q_b=mla_w_q_b, mla_kv_a_norm=mla_kv_a_norm,
             mla_w_kv_b=mla_w_kv_b, rwkv_mu=rwkv_mu, rwkv_w0=rwkv_w0, rwkv_w_up=rwkv_w_up, rwkv_a0=rwkv_a0,
             rwkv_a_up=rwkv_a_up, rwkv_g_up=rwkv_g_up, rwkv_k_k=rwkv_k_k, rwkv_k_a=rwkv_k_a, rwkv_r_k=rwkv_r_k,
             rwkv_lnx_g=rwkv_lnx_g, rwkv_lnx_b=rwkv_lnx_b, ssm_a_re=ssm_a_re, ssm_a_im=ssm_a_im,
             ssm_b_re=ssm_b_re, ssm_b_im=ssm_b_im, ssm_c_re=ssm_c_re, ssm_c_im=ssm_c_im, ssm_d=ssm_d,
             ssm_log_dt=ssm_log_dt, ssm_glu_w=ssm_glu_w, ssm_glu_b=ssm_glu_b, w_br_a=w_br_a, w_br_b=w_br_b,
             w_br_c=w_br_c, w_out=w_out, ln1_g=ln1_g, ln1_b=ln1_b, router_w=router_w, router_b=router_b,
             exp_w_gu=exp_w_gu, exp_b_gu=exp_b_gu, exp_w_down=exp_w_down, exp_b_down=exp_b_down,
             ln2_g=ln2_g, ln2_b=ln2_b)
    depth = w_in.shape[0]
    weights = [_prep_layer(P, l) for l in range(depth)]
    outs_p = _trunk(x_prompt, weights, None)
    caches = (cache_mla_ckv, cache_mla_kpe, state_rwkv_shift, state_rwkv_wkv, state_ssm_re, state_ssm_im)
    outs_s = _trunk(x_sample, weights, caches)
    return (outs_p[0], outs_s[0]) + outs_p[1:] + outs_s[1:]
```

```python
import functools
import math

import numpy as np
import jax
import jax.numpy as jnp
from jax import lax
from jax.experimental import pallas as pl
from jax.experimental.pallas import tpu as pltpu

F32 = jnp.float32
BF16 = jnp.bfloat16

D_MODEL = 1024
CHUNK = 64
MLA_HEADS = 8
MLA_NOPE = 64
MLA_ROPE = 32
MLA_V = 64
MLA_Q_RANK = 384
MLA_KV_RANK = 256
ROPE_THETA = 10000.0
HEAD_PAD = 128
RWKV_HEADS = 8
RWKV_HEAD = 64
RWKV_DIM = RWKV_HEADS * RWKV_HEAD
RWKV_PROJ = 3 * RWKV_DIM + 64 + 64 + 128
RWKV_LN_EPS = 64e-5
SSM_DIM = 512
SSM_GROUP = 16
SSM_GROUPS = 32
SSM_STATE = 64
SSM_CHUNKS = 4
SSM_CH = SSM_GROUPS * SSM_STATE
IN_SIZES = (MLA_Q_RANK, MLA_KV_RANK + MLA_ROPE, RWKV_PROJ, SSM_DIM, D_MODEL, D_MODEL, D_MODEL)
N_EXPERTS = 32
TOP_K = 4
D_FF = 512
SWIGLU_LIMIT = 7.0
SWIGLU_ALPHA = 1.702
DEPTH = 2
DN_ALPHA = (2 * DEPTH) ** 0.25
LN_EPS = 1e-5
RMS_EPS = 1e-6
NEG_INF = -1e30
ATT_SCALE = (MLA_NOPE + MLA_ROPE) ** -0.5
LANES = 128
VMEM_LIMIT = 48 * 1024 * 1024


def _cparams(*sem):
    return pltpu.CompilerParams(dimension_semantics=sem, vmem_limit_bytes=VMEM_LIMIT)


def _dot(a, b):
    return jnp.dot(a, b, preferred_element_type=F32)


def _sigmoid(x):
    return 1.0 / (1.0 + jnp.exp(-x))


def _layer_norm(x, g, b):
    mu = jnp.mean(x, -1, keepdims=True)
    xc = x - mu
    var = jnp.mean(xc * xc, -1, keepdims=True)
    return xc * lax.rsqrt(var + LN_EPS) * g + b


def _full(shape):
    n = len(shape)
    return pl.BlockSpec(shape, lambda *_: (0,) * n)


def _mm_kernel(x_ref, w_ref, o_ref):
    o_ref[...] = _dot(x_ref[...].astype(BF16), w_ref[...]).astype(o_ref.dtype)


def _matmul(x, w, nb, s, *, time_major_out=False, tm=512, tn=1024, out_dtype=F32):
    K = x.shape[1]
    N = w.shape[1]
    tm = min(tm, s)
    tn = min(tn, N)
    while N % tn:
        tn -= LANES
    nt = s // tm
    if time_major_out:
        assert tn == N
        out_shape = jax.ShapeDtypeStruct((s, nb * N), out_dtype)
        out_spec = pl.BlockSpec((tm, N), lambda b, i, j: (i, b))
    else:
        out_shape = jax.ShapeDtypeStruct((nb * s, N), out_dtype)
        out_spec = pl.BlockSpec((tm, tn), lambda b, i, j: (b * nt + i, j))
    return pl.pallas_call(
        _mm_kernel,
        grid=(nb, nt, N // tn),
        in_specs=[pl.BlockSpec((tm, K), lambda b, i, j: (b * nt + i, 0)),
                  pl.BlockSpec((K, tn), lambda b, i, j: (0, j))],
        out_specs=out_spec,
        out_shape=out_shape,
        compiler_params=_cparams("parallel", "parallel", "arbitrary"),
        name="matmul",
    )(x, w)


def _mla_in_kernel(x_ref, w_ref, qg_ref, kvg_ref, wqb_ref, tc_ref, ts_ref, q_ref, ckv_ref, kpe_ref):
    h = _dot(x_ref[...].astype(BF16), w_ref[...])
    tc = tc_ref[...]
    ts = ts_ref[...]
    qa = h[:, :MLA_Q_RANK]
    qn = qa * lax.rsqrt(jnp.mean(qa * qa, -1, keepdims=True) + RMS_EPS) * qg_ref[...]
    q2 = _dot(qn.astype(BF16), wqb_ref[...])
    hp = MLA_HEADS * HEAD_PAD
    for hd in range(MLA_HEADS):
        lo = hd * HEAD_PAD
        q0 = q2[:, lo:lo + HEAD_PAD]
        q1 = q2[:, hp + lo:hp + lo + HEAD_PAD]
        q_ref[:, lo:lo + HEAD_PAD] = ((q0 * tc + q1 * ts) * ATT_SCALE).astype(BF16)
    c0 = MLA_Q_RANK
    ckv = h[:, c0:c0 + MLA_KV_RANK]
    ckv_ref[...] = ckv * lax.rsqrt(jnp.mean(ckv * ckv, -1, keepdims=True) + RMS_EPS) * kvg_ref[...]
    c1 = c0 + MLA_KV_RANK
    kpe_ref[...] = h[:, c1:c1 + LANES] * tc + h[:, c1 + LANES:c1 + 2 * LANES] * ts


def _mla_in(x, lw, tc, ts, *, tm=512):
    T = x.shape[0]
    tm = min(tm, T)
    hp = MLA_HEADS * HEAD_PAD
    nw = lw["w_mla"].shape[1]
    return pl.pallas_call(
        _mla_in_kernel,
        grid=(T // tm,),
        in_specs=[pl.BlockSpec((tm, D_MODEL), lambda i: (i, 0)),
                  _full((D_MODEL, nw)), _full((1, MLA_Q_RANK)), _full((1, MLA_KV_RANK)),
                  _full((MLA_Q_RANK, 2 * hp)),
                  pl.BlockSpec((tm, LANES), lambda i: (i, 0)),
                  pl.BlockSpec((tm, LANES), lambda i: (i, 0))],
        out_specs=[pl.BlockSpec((tm, hp), lambda i: (i, 0)),
                   pl.BlockSpec((tm, MLA_KV_RANK), lambda i: (i, 0)),
                   pl.BlockSpec((tm, LANES), lambda i: (i, 0))],
        out_shape=[jax.ShapeDtypeStruct((T, hp), BF16),
                   jax.ShapeDtypeStruct((T, MLA_KV_RANK), F32),
                   jax.ShapeDtypeStruct((T, LANES), F32)],
        compiler_params=_cparams("parallel"),
        name="mla_in",
    )(x, lw["w_mla"], lw["q_norm"], lw["kv_norm"], lw["w_qb"], tc, ts)


def _kv_expand_kernel(ckv_ref, kpe_ref, wk_ref, wv_ref, k_ref, v_ref):
    c = ckv_ref[...].astype(BF16)
    k = _dot(c, wk_ref[...])
    kpe = kpe_ref[...]
    for hd in range(MLA_HEADS):
        lo = hd * HEAD_PAD
        k_ref[:, lo:lo + HEAD_PAD] = (k[:, lo:lo + HEAD_PAD] + kpe).astype(BF16)
    v_ref[...] = _dot(c, wv_ref[...]).astype(BF16)


def _kv_expand(ckv, kpe, lw, *, tm=512):
    T = ckv.shape[0]
    tm = min(tm, T)
    while T % tm:
        tm //= 2
    hp = MLA_HEADS * HEAD_PAD
    hv = MLA_HEADS * MLA_V
    return pl.pallas_call(
        _kv_expand_kernel,
        grid=(T // tm,),
        in_specs=[pl.BlockSpec((tm, MLA_KV_RANK), lambda i: (i, 0)),
                  pl.BlockSpec((tm, LANES), lambda i: (i, 0)),
                  _full((MLA_KV_RANK, hp)), _full((MLA_KV_RANK, hv))],
        out_specs=[pl.BlockSpec((tm, hp), lambda i: (i, 0)),
                   pl.BlockSpec((tm, hv), lambda i: (i, 0))],
        out_shape=[jax.ShapeDtypeStruct((T, hp), BF16), jax.ShapeDtypeStruct((T, hv), BF16)],
        compiler_params=_cparams("parallel"),
        name="kv_expand",
    )(ckv, kpe, lw["w_k"], lw["w_v"])


def _attn_kernel(q_ref, k_ref, v_ref, o_ref, *, tq, nq, q_start, sk):
    lane = lax.broadcasted_iota(jnp.int32, (tq, LANES), 1)
    nt = (((1,), (1,)), ((), ()))
    for i in range(nq):
        q_lo = q_start + i * tq
        k_end = min(((q_lo + tq - 1) // CHUNK + 1) * CHUNK, sk)
        kw = -(-k_end // LANES) * LANES
        c0 = min((q_lo // CHUNK + 1) * CHUNK, sk) // LANES * LANES
        tail = kw - c0
        if tail:
            q_chunk = (q_lo + lax.broadcasted_iota(jnp.int32, (tq, tail), 0)) // CHUNK
            k_pos = c0 + lax.broadcasted_iota(jnp.int32, (tq, tail), 1)
            visible = k_pos // CHUNK <= q_chunk
            if kw > sk:
                visible = visible & (k_pos < sk)
        outs = []
        for hh in range(2):
            hs = slice(hh * HEAD_PAD, (hh + 1) * HEAD_PAD)
            q = q_ref[i * tq:(i + 1) * tq, hs]
            parts = []
            if c0:
                parts.append(lax.dot_general(q, k_ref[0:c0, hs], nt, preferred_element_type=F32))
            if tail:
                s_t = lax.dot_general(q, k_ref[c0:kw, hs], nt, preferred_element_type=F32)
                parts.append(jnp.where(visible, s_t, NEG_INF))
            m = jnp.max(parts[0], -1, keepdims=True)
            for s_ in parts[1:]:
                m = jnp.maximum(m, jnp.max(s_, -1, keepdims=True))
            l = None
            acc = None
            lo = 0
            for s_ in parts:
                p = jnp.exp(s_ - m)
                ps = jnp.sum(p, -1, keepdims=True)
                pv = _dot(p.astype(BF16), v_ref[lo:lo + s_.shape[1], :])
                l = ps if l is None else l + ps
                acc = pv if acc is None else acc + pv
                lo += s_.shape[1]
            outs.append(acc / l)
        o_ref[i * tq:(i + 1) * tq, :] = jnp.where(lane < MLA_V, outs[0], outs[1])


def _attention(q, k, v, nb, sq, skp, sk, q_start, *, tq=256):
    tq = min(tq, sq)
    kern = functools.partial(_attn_kernel, tq=tq, nq=sq // tq, q_start=q_start, sk=sk)
    return pl.pallas_call(
        kern,
        grid=(nb, MLA_HEADS // 2),
        in_specs=[pl.BlockSpec((sq, 2 * HEAD_PAD), lambda b, j: (b, j)),
                  pl.BlockSpec((skp, 2 * HEAD_PAD), lambda b, j: (b, j)),
                  pl.BlockSpec((skp, 2 * MLA_V), lambda b, j: (b, j))],
        out_specs=pl.BlockSpec((sq, 2 * MLA_V), lambda b, j: (b, j)),
        out_shape=jax.ShapeDtypeStruct((nb * sq, MLA_HEADS * MLA_V), F32),
        compiler_params=_cparams("parallel", "parallel"),
        name="attention",
    )(q, k, v)


def _head_sum(z, ones_bd):
    hi = z.astype(BF16)
    lo = (z - hi.astype(F32)).astype(BF16)
    return _dot(hi, ones_bd) + _dot(lo, ones_bd)


def _rwkv_pre_kernel(p_ref, prev_ref, shift_ref, mu_ref, w0_ref, wup_ref, a0_ref, aup_ref, gup_ref,
                     kk_ref, ka_ref, rk_ref, ones_ref,
                     r_out, w_out, k_out, v_out, kk_out, b_out, g_out, bonus_out):
    p = p_ref[...]
    first = pl.program_id(1) == 0
    prev_row = jnp.where(first, shift_ref[0], prev_ref[7:8, :])
    row = lax.broadcasted_iota(jnp.int32, p.shape, 0)
    prev = jnp.where(row == 0, prev_row, pltpu.roll(p, 1, 0))
    ps = p + (prev - p) * mu_ref[...]
    d = RWKV_DIM
    r = ps[:, 0:d]
    k = ps[:, d:2 * d]
    v = ps[:, 2 * d:3 * d]
    wa = ps[:, 3 * d:3 * d + LANES]
    gd = ps[:, 3 * d + LANES:3 * d + 2 * LANES]
    ones_bd = ones_ref[...]
    wlin = w0_ref[...] + _dot(jnp.tanh(wa).astype(BF16), wup_ref[...])
    z = -wlin
    w = -(jnp.maximum(z, 0.0) + jnp.log1p(jnp.exp(-jnp.abs(z)))) - 0.5
    w_out[...] = jnp.exp(-jnp.exp(w))
    a = _sigmoid(a0_ref[...] + _dot(wa.astype(BF16), aup_ref[...]))
    g_out[...] = _dot(_sigmoid(gd).astype(BF16), gup_ref[...])
    kk = k * kk_ref[...]
    kkn = kk * lax.rsqrt(_head_sum(kk * kk, ones_bd) + 1e-12)
    kk_out[...] = kkn
    b_out[...] = kkn * a
    kn = k * (1.0 + (a - 1.0) * ka_ref[...])
    r_out[...] = r
    k_out[...] = kn
    v_out[...] = v
    bonus_out[...] = _head_sum(r * kn * rk_ref[...], ones_bd) * v


def _rwkv_pre(rw, shift_prev, lw, nb, s, *, tm=256):
    tm = min(tm, s)
    tps = s // tm
    d = RWKV_DIM
    vec = lambda n: _full((1, n))
    bm = pl.BlockSpec((tm, d), lambda b, i: (b * tps + i, 0))
    tmaj = pl.BlockSpec((tm, d), lambda b, i: (i, b))
    bm_shape = jax.ShapeDtypeStruct((nb * s, d), F32)
    tm_shape = jax.ShapeDtypeStruct((s, nb * d), F32)
    return pl.pallas_call(
        _rwkv_pre_kernel,
        grid=(nb, tps),
        in_specs=[pl.BlockSpec((tm, RWKV_PROJ), lambda b, i: (b * tps + i, 0)),
                  pl.BlockSpec((8, RWKV_PROJ), lambda b, i: (jnp.maximum((b * tps + i) * (tm // 8) - 1, 0), 0)),
                  pl.BlockSpec((1, 1, RWKV_PROJ), lambda b, i: (b, 0, 0)),
                  vec(RWKV_PROJ), vec(d), _full((LANES, d)), vec(d), _full((LANES, d)),
                  _full((LANES, d)), vec(d), vec(d), vec(d), _full((d, d))],
        out_specs=[tmaj] * 6 + [bm] * 2,
        out_shape=[tm_shape] * 6 + [bm_shape] * 2,
        compiler_params=_cparams("parallel", "parallel"),
        name="rwkv_pre",
    )(rw, rw, shift_prev.reshape(nb, 1, RWKV_PROJ), lw["mu"], lw["w0"], lw["w_up"], lw["a0"], lw["a_up"],
      lw["g_up"], lw["k_k"], lw["k_a"], lw["r_k"], lw["ones_bd"])


RWKV_VC = 16


def _rwkv_scan_kernel(r_ref, w_ref, k_ref, v_ref, kk_ref, b_ref, s0_ref, o_ref, sT_ref, st_ref, *, tt):
    n = RWKV_HEAD

    @pl.when(pl.program_id(0) == 0)
    def _():
        st_ref[...] = s0_ref[...]

    def step(t, carry):
        for c in range(n // RWKV_VC):
            vs = slice(c * RWKV_VC, (c + 1) * RWKV_VC)
            parts = [None] * 4
            for q in range(n):
                term = st_ref[q, vs, :] * kk_ref[t, q:q + 1, :]
                parts[q % 4] = term if parts[q % 4] is None else parts[q % 4] + term
            sa = -((parts[0] + parts[1]) + (parts[2] + parts[3]))
            vt = v_ref[t, vs, :]
            outs = [None] * 4
            for q in range(n):
                s_new = (st_ref[q, vs, :] * w_ref[t, q:q + 1, :] + sa * b_ref[t, q:q + 1, :]
                         + vt * k_ref[t, q:q + 1, :])
                st_ref[q, vs, :] = s_new
                term = s_new * r_ref[t, q:q + 1, :]
                outs[q % 4] = term if outs[q % 4] is None else outs[q % 4] + term
            o_ref[t, vs, :] = (outs[0] + outs[1]) + (outs[2] + outs[3])
        return carry

    lax.fori_loop(0, tt, step, 0)

    @pl.when(pl.program_id(0) == pl.num_programs(0) - 1)
    def _():
        sT_ref[...] = st_ref[...]


def _rwkv_scan(seqs, s0, s, *, tt=16):
    L = s0.shape[-1]
    n = RWKV_HEAD
    tt = min(tt, s)
    blk = pl.BlockSpec((tt, n, L), lambda i: (i, 0, 0))
    kern = functools.partial(_rwkv_scan_kernel, tt=tt)
    return pl.pallas_call(
        kern,
        grid=(s // tt,),
        in_specs=[blk] * 6 + [_full((n, n, L))],
        out_specs=[blk, _full((n, n, L))],
        out_shape=[jax.ShapeDtypeStruct((s, n, L), F32), jax.ShapeDtypeStruct((n, n, L), F32)],
        scratch_shapes=[pltpu.VMEM((n, n, L), F32)],
        compiler_params=_cparams("arbitrary"),
        name="rwkv_scan",
    )(*seqs, s0)


def _s5_params_kernel(are_ref, aim_ref, ldt_ref, bre_ref, bim_ref, abre_ref, abim_ref, bbre_ref, bbim_ref):
    lr = are_ref[...]
    li = aim_ref[...]
    dt = jnp.exp(ldt_ref[...])
    mag = jnp.exp(lr * dt)
    ab_re = mag * jnp.cos(li * dt)
    ab_im = mag * jnp.sin(li * dt)
    den = lr * lr + li * li
    f_re = ((ab_re - 1.0) * lr + ab_im * li) / den
    f_im = (ab_im * lr - (ab_re - 1.0) * li) / den
    abre_ref[...] = ab_re
    abim_ref[...] = ab_im
    for i in range(SSM_GROUP):
        br = bre_ref[i]
        bi = bim_ref[i]
        bbre_ref[i] = f_re * br - f_im * bi
        bbim_ref[i] = f_re * bi + f_im * br


def _s5_params(a_re, a_im, log_dt, b_re, b_im):
    g, n = a_re.shape
    gn = jax.ShapeDtypeStruct((g, n), F32)
    ign = jax.ShapeDtypeStruct((SSM_GROUP, g, n), F32)
    return pl.pallas_call(
        _s5_params_kernel,
        out_shape=[gn, gn, ign, ign],
        name="s5_params",
    )(a_re, a_im, log_dt.reshape(g, 1), jnp.transpose(b_re, (2, 0, 1)), jnp.transpose(b_im, (2, 0, 1)))


def _gelu_tanh(x):
    return 0.5 * x * (1.0 + jnp.tanh(math.sqrt(2.0 / math.pi) * (x + 0.044715 * (x * x * x))))


def _s5_kernel(u_ref, bre_ref, bim_ref, cre_ref, cim_ref, d_ref, are_ref, aim_ref, h0re_ref, h0im_ref,
               gw_ref, gb_ref, y_ref, hTre_ref, hTim_ref, hre_s, him_s, sre_s, sim_s, yy_s, *, tt, nb):
    cw = SSM_CH // SSM_CHUNKS

    @pl.when(pl.program_id(0) == 0)
    def _():
        hre_s[...] = h0re_ref[...]
        him_s[...] = h0im_ref[...]

    for c in range(SSM_CHUNKS):
        uc = u_ref[:, c * LANES:(c + 1) * LANES]
        ub = uc.astype(BF16)
        sre_s[...] = _dot(ub, bre_ref[c])
        sim_s[...] = _dot(ub, bim_ref[c])
        a_re = jnp.broadcast_to(are_ref[:, c * cw:(c + 1) * cw], (nb, cw))
        a_im = jnp.broadcast_to(aim_ref[:, c * cw:(c + 1) * cw], (nb, cw))

        def step(t, carry, a_re=a_re, a_im=a_im):
            hr, hi = carry
            rows = pl.ds(pl.multiple_of(t * nb, nb), nb)
            nr = a_re * hr - a_im * hi + sre_s[rows, :]
            ni = a_re * hi + a_im * hr + sim_s[rows, :]
            sre_s[rows, :] = nr
            sim_s[rows, :] = ni
            return nr, ni

        hr, hi = lax.fori_loop(0, tt, step, (hre_s[:, c * cw:(c + 1) * cw], him_s[:, c * cw:(c + 1) * cw]))
        hre_s[:, c * cw:(c + 1) * cw] = hr
        him_s[:, c * cw:(c + 1) * cw] = hi
        yc = _dot(sre_s[...].astype(BF16), cre_ref[c]) - _dot(sim_s[...].astype(BF16), cim_ref[c])
        yy_s[:, c * LANES:(c + 1) * LANES] = yc + d_ref[:, c * LANES:(c + 1) * LANES] * uc

    y = _gelu_tanh(yy_s[...])
    z = _dot(y.astype(BF16), gw_ref[...]) + gb_ref[...]
    y_ref[...] = z[:, :SSM_DIM] * _sigmoid(z[:, SSM_DIM:])

    @pl.when(pl.program_id(0) == pl.num_programs(0) - 1)
    def _():
        hTre_ref[...] = hre_s[...]
        hTim_ref[...] = him_s[...]


def _s5(u_tm, h0_re, h0_im, lw, nb, s, *, rows=512):
    tt = max(min(rows // nb, s), 1)
    R = tt * nb
    cw = SSM_CH // SSM_CHUNKS
    kern = functools.partial(_s5_kernel, tt=tt, nb=nb)
    return pl.pallas_call(
        kern,
        grid=(s // tt,),
        in_specs=[pl.BlockSpec((R, SSM_DIM), lambda i: (i, 0)),
                  _full((SSM_CHUNKS, LANES, cw)), _full((SSM_CHUNKS, LANES, cw)),
                  _full((SSM_CHUNKS, cw, LANES)), _full((SSM_CHUNKS, cw, LANES)),
                  _full((1, SSM_DIM)), _full((1, SSM_CH)), _full((1, SSM_CH)),
                  _full((nb, SSM_CH)), _full((nb, SSM_CH)),
                  _full((SSM_DIM, 2 * SSM_DIM)), _full((1, 2 * SSM_DIM))],
        out_specs=[pl.BlockSpec((R, SSM_DIM), lambda i: (i, 0)), _full((nb, SSM_CH)), _full((nb, SSM_CH))],
        out_shape=[jax.ShapeDtypeStruct((s * nb, SSM_DIM), F32),
                   jax.ShapeDtypeStruct((nb, SSM_CH), F32), jax.ShapeDtypeStruct((nb, SSM_CH), F32)],
        scratch_shapes=[pltpu.VMEM((nb, SSM_CH), F32), pltpu.VMEM((nb, SSM_CH), F32),
                        pltpu.VMEM((R, cw), F32), pltpu.VMEM((R, cw), F32), pltpu.VMEM((R, SSM_DIM), F32)],
        compiler_params=_cparams("arbitrary"),
        name="s5",
    )(u_tm, lw["s5_bre"], lw["s5_bim"], lw["s5_cre"], lw["s5_cim"], lw["s5_d"], lw["s5_are"], lw["s5_aim"],
      h0_re, h0_im, lw["glu_w"], lw["glu_b"])


def _merge_kernel(x_ref, ya_ref, o_ref, bonus_ref, g_ref, yc_ref, wg_ref, lng_ref, lnb_ref, ones_ref,
                  wa_ref, wb_ref, wc_ref, wo_ref, ln1g_ref, ln1b_ref, out_ref):
    ones_bd = ones_ref[...]
    o = o_ref[...]
    inv_n = 1.0 / RWKV_HEAD
    mean = _head_sum(o, ones_bd) * inv_n
    oc = o - mean
    var = _head_sum(oc * oc, ones_bd) * inv_n
    yb = (oc * lax.rsqrt(var + RWKV_LN_EPS) * lng_ref[...] + lnb_ref[...] + bonus_ref[...]) * g_ref[...]
    d = D_MODEL
    x = x_ref[...]
    xb = x.astype(BF16)
    merged = None
    for j, (y, w_ref) in enumerate(((ya_ref[...], wa_ref), (yb, wb_ref), (yc_ref[...], wc_ref))):
        gate = _sigmoid(_dot(xb, wg_ref[:, j * d:(j + 1) * d]))
        term = gate * _dot(y.astype(BF16), w_ref[...])
        merged = term if merged is None else merged + term
    y = DN_ALPHA * x + _dot(merged.astype(BF16), wo_ref[...])
    out_ref[...] = _layer_norm(y, ln1g_ref[...], ln1b_ref[...])


def _merge(x, ya, o_tm, bonus, g, yc_tm, lw, nb, s, *, tm=256):
    tm = min(tm, s)
    nt = s // tm
    d = D_MODEL
    h = RWKV_DIM
    row = lambda w: pl.BlockSpec((tm, w), lambda b, i: (b * nt + i, 0))
    tmaj = lambda w: pl.BlockSpec((tm, w), lambda b, i: (i, b))
    vec = lambda n: _full((1, n))
    return pl.pallas_call(
        _merge_kernel,
        grid=(nb, nt),
        in_specs=[row(d), row(h), tmaj(h), row(h), row(h), tmaj(SSM_DIM),
                  _full((d, 3 * d)), vec(h), vec(h), _full((h, h)),
                  _full((h, d)), _full((h, d)), _full((h, d)), _full((d, d)), vec(d), vec(d)],
        out_specs=row(d),
        out_shape=jax.ShapeDtypeStruct((nb * s, d), F32),
        compiler_params=_cparams("parallel", "parallel"),
        name="merge",
    )(x, ya, o_tm, bonus, g, yc_tm, lw["w_gates"], lw["lnx_g"], lw["lnx_b"], lw["ones_bd"],
      lw["w_br_a"], lw["w_br_b"], lw["w_br_c"], lw["w_out"], lw["ln1_g"], lw["ln1_b"])


INFO_GATE, INFO_EID, INFO_RANK = 0, 4, 8


def _router_kernel(x_ref, whi_ref, wlo_ref, b_ref, tri_ref, info_ref, cnt_ref, base_s):
    @pl.when(pl.program_id(0) == 0)
    def _():
        base_s[...] = jnp.zeros_like(base_s)

    x = x_ref[...]
    xh = x.astype(BF16)
    xl = (x - xh.astype(F32)).astype(BF16)
    logits = _dot(xh, whi_ref[...]) + _dot(xl, whi_ref[...]) + _dot(xh, wlo_ref[...]) + b_ref[...]
    lane_i = lax.broadcasted_iota(jnp.int32, logits.shape, 1)
    lane = lane_i.astype(F32)
    vals, sels, ids = [], [], []
    for _ in range(TOP_K):
        m = jnp.max(logits, -1, keepdims=True)
        idx = jnp.min(jnp.where(logits == m, lane, float(LANES)), -1, keepdims=True)
        sel = lane == idx
        vals.append(m)
        sels.append(sel)
        ids.append(idx)
        logits = jnp.where(sel, -3e38, logits)
    es = [jnp.exp(v - vals[0]) for v in vals]
    den = es[0] + es[1] + es[2] + es[3]
    chosen = jnp.zeros_like(logits)
    for sel in sels:
        chosen = chosen + jnp.where(sel, 1.0, 0.0)
    before = _dot(tri_ref[...], chosen.astype(BF16)) + base_s[...]
    info = jnp.zeros_like(logits)
    for k in range(TOP_K):
        rank = jnp.sum(jnp.where(sels[k], before, 0.0), -1, keepdims=True)
        info = (info + jnp.where(lane_i == INFO_GATE + k, es[k] / den, 0.0)
                + jnp.where(lane_i == INFO_EID + k, ids[k], 0.0)
                + jnp.where(lane_i == INFO_RANK + k, rank, 0.0))
    info_ref[...] = info
    base_s[...] += jnp.sum(chosen, 0, keepdims=True)
    cnt_ref[...] = base_s[...]


def _router(x, lw, *, tm=512):
    T = x.shape[0]
    tm = min(tm, T)
    tri = jnp.asarray(np.tril(np.ones((tm, tm), np.float32), -1), dtype=BF16)
    return pl.pallas_call(
        _router_kernel,
        grid=(T // tm,),
        in_specs=[pl.BlockSpec((tm, D_MODEL), lambda i: (i, 0)),
                  _full((D_MODEL, LANES)), _full((D_MODEL, LANES)), _full((1, LANES)), _full((tm, tm))],
        out_specs=[pl.BlockSpec((tm, LANES), lambda i: (i, 0)), _full((1, LANES))],
        out_shape=[jax.ShapeDtypeStruct((T, LANES), F32), jax.ShapeDtypeStruct((1, LANES), F32)],
        scratch_shapes=[pltpu.VMEM((1, LANES), F32)],
        compiler_params=_cparams("arbitrary"),
        name="router",
    )(x, lw["router_hi"], lw["router_lo"], lw["router_b"], tri)


def _moe_schedule(info, cnt, tm_e, nt):
    eid = info[:, INFO_EID:INFO_EID + TOP_K].astype(jnp.int32)
    rank = info[:, INFO_RANK:INFO_RANK + TOP_K].astype(jnp.int32)
    counts = cnt[0, :N_EXPERTS].astype(jnp.int32)
    gsz = (counts + tm_e - 1) // tm_e * tm_e
    gend = jnp.cumsum(gsz)
    goff = gend - gsz
    onehot = eid[..., None] == jnp.arange(N_EXPERTS, dtype=jnp.int32)
    pos = jnp.sum(jnp.where(onehot, goff, 0), -1) + rank
    tile_start = jnp.arange(nt, dtype=jnp.int32) * tm_e
    tile_e = jnp.minimum(jnp.sum((gend[None, :] <= tile_start[:, None]).astype(jnp.int32), -1), N_EXPERTS - 1)
    n_valid = (gend[-1] // tm_e).reshape(1)
    i32 = lambda a: a.astype(jnp.int32)
    return i32(pos.reshape(-1)), i32(tile_e), i32(n_valid), i32(gend), i32(gsz)


def _dispatch_kernel(gend_ref, gsz_ref, nv_ref, pos_ref, x_ref, xs_ref, zbuf, sem, zsem, *, tm, tm_e, nt):
    @pl.when(pl.program_id(0) == 0)
    def _():
        zbuf[...] = jnp.zeros_like(zbuf)
        fill = lambda row0: pltpu.make_async_copy(zbuf, xs_ref.at[pl.ds(pl.multiple_of(row0, tm_e), tm_e)], zsem)
        for phase in ("start", "wait"):
            for e in range(N_EXPERTS):
                @pl.when(gsz_ref[e] > 0)
                def _(e=e, phase=phase):
                    getattr(fill(gend_ref[e] - tm_e), phase)()

            def tail(j, carry, phase=phase):
                getattr(fill(j * tm_e), phase)()
                return carry

            lax.fori_loop(nv_ref[0], nt, tail, 0)

    def issue(t, carry):
        for k in range(TOP_K):
            p = pos_ref[t * TOP_K + k]
            pltpu.make_async_copy(x_ref.at[pl.ds(t, 1)], xs_ref.at[pl.ds(p, 1)], sem).start(priority=k % 2)
        return carry

    lax.fori_loop(0, tm, issue, 0)
    for k in range(TOP_K):
        pltpu.make_async_copy(x_ref, xs_ref.at[pl.ds(0, tm)], sem).wait()


def _dispatch(x, pos, gend, gsz, n_valid, tm_e, nt, *, tm=256):
    T, d = x.shape
    tm = min(tm, T)
    kern = functools.partial(_dispatch_kernel, tm=tm, tm_e=tm_e, nt=nt)
    return pl.pallas_call(
        kern,
        grid_spec=pltpu.PrefetchScalarGridSpec(
            num_scalar_prefetch=3,
            grid=(T // tm,),
            in_specs=[pl.BlockSpec((tm * TOP_K,), lambda i, *_: (i,), memory_space=pltpu.SMEM),
                      pl.BlockSpec((tm, d), lambda i, *_: (i, 0))],
            out_specs=pl.BlockSpec(memory_space=pl.ANY),
            scratch_shapes=[pltpu.VMEM((tm_e, d), F32), pltpu.SemaphoreType.DMA, pltpu.SemaphoreType.DMA]),
        out_shape=jax.ShapeDtypeStruct((nt * tm_e, d), F32),
        compiler_params=_cparams("arbitrary"),
        name="moe_dispatch",
    )(gend, gsz, n_valid, pos, x)


def _expert_kernel(te_ref, nv_ref, xs_ref, wgu_ref, bgu_ref, wd_ref, bd_ref, ys_ref):
    del te_ref

    @pl.when(pl.program_id(0) < nv_ref[0])
    def _():
        h = _dot(xs_ref[...].astype(BF16), wgu_ref[0]) + bgu_ref[0]
        hg = jnp.minimum(h[:, :D_FF], SWIGLU_LIMIT)
        hl = jnp.clip(h[:, D_FF:], -SWIGLU_LIMIT, SWIGLU_LIMIT)
        act = hg * _sigmoid(SWIGLU_ALPHA * hg) * (hl + 1.0)
        ys_ref[...] = _dot(act.astype(BF16), wd_ref[0]) + bd_ref[0]

    @pl.when(pl.program_id(0) >= nv_ref[0])
    def _():
        ys_ref[...] = jnp.zeros_like(ys_ref)


def _experts(xs, tile_e, n_valid, lw, tm_e):
    rows, d = xs.shape
    nt = rows // tm_e
    tile = lambda j, te, nv: (jnp.minimum(j, nv[0] - 1), 0)
    wsel = lambda j, te, nv: (te[jnp.minimum(j, nv[0] - 1)], 0, 0)
    return pl.pallas_call(
        _expert_kernel,
        grid_spec=pltpu.PrefetchScalarGridSpec(
            num_scalar_prefetch=2,
            grid=(nt,),
            in_specs=[pl.BlockSpec((tm_e, d), tile),
                      pl.BlockSpec((1, d, 2 * D_FF), wsel), pl.BlockSpec((1, 1, 2 * D_FF), wsel),
                      pl.BlockSpec((1, D_FF, d), wsel), pl.BlockSpec((1, 1, d), wsel)],
            out_specs=pl.BlockSpec((tm_e, d), lambda j, te, nv: (j, 0))),
        out_shape=jax.ShapeDtypeStruct((rows, d), F32),
        compiler_params=_cparams("arbitrary"),
        name="moe_experts",
    )(tile_e, n_valid, xs, lw["w_gu"], lw["b_gu"], lw["w_down"], lw["b_down"])


def _combine_kernel(pos_ref, x_ref, info_ref, ys_ref, ln2g_ref, ln2b_ref, out_ref, buf, sem, *, tm):
    def issue(t, carry):
        for k in range(TOP_K):
            p = pos_ref[t * TOP_K + k]
            pltpu.make_async_copy(ys_ref.at[pl.ds(p, 1)], buf.at[k, pl.ds(t, 1)], sem).start(priority=k % 2)
        return carry

    lax.fori_loop(0, tm, issue, 0)
    for k in range(TOP_K):
        pltpu.make_async_copy(ys_ref.at[pl.ds(0, tm)], buf.at[k], sem).wait()
    info = info_ref[...]
    lane = lax.broadcasted_iota(jnp.int32, info.shape, 1)
    acc = DN_ALPHA * x_ref[...]
    for k in range(TOP_K):
        gate = jnp.sum(jnp.where(lane == INFO_GATE + k, info, 0.0), -1, keepdims=True)
        acc = acc + gate * buf[k]
    out_ref[...] = _layer_norm(acc, ln2g_ref[...], ln2b_ref[...])


def _combine(x, info, pos, ys, lw, *, tm=256):
    T, d = x.shape
    tm = min(tm, T)
    kern = functools.partial(_combine_kernel, tm=tm)
    return pl.pallas_call(
        kern,
        grid=(T // tm,),
        in_specs=[pl.BlockSpec((tm * TOP_K,), lambda i: (i,), memory_space=pltpu.SMEM),
                  pl.BlockSpec((tm, d), lambda i: (i, 0)),
                  pl.BlockSpec((tm, LANES), lambda i: (i, 0)),
                  pl.BlockSpec(memory_space=pl.ANY),
                  _full((1, d)), _full((1, d))],
        out_specs=pl.BlockSpec((tm, d), lambda i: (i, 0)),
        out_shape=jax.ShapeDtypeStruct((T, d), F32),
        scratch_shapes=[pltpu.VMEM((TOP_K, tm, d), F32), pltpu.SemaphoreType.DMA],
        compiler_params=_cparams("arbitrary"),
        name="moe_combine",
    )(pos, x, info, ys, lw["ln2_g"], lw["ln2_b"])


def _moe(x, lw):
    T = x.shape[0]
    tm_e = 512 if T >= 4096 else 128
    nt = T * TOP_K // tm_e + N_EXPERTS
    info, cnt = _router(x, lw)
    pos, tile_e, n_valid, gend, gsz = _moe_schedule(info, cnt, tm_e, nt)
    xs = _dispatch(x, pos, gend, gsz, n_valid, tm_e, nt)
    ys = _experts(xs, tile_e, n_valid, lw, tm_e)
    return _combine(x, info, pos, ys, lw)


def _ones_block_diag():
    idx = np.arange(RWKV_DIM) // RWKV_HEAD
    return jnp.asarray((idx[:, None] == idx[None, :]).astype(np.float32), dtype=BF16)


def _deinterleave_matrix():
    n = 2 * D_FF
    src = np.concatenate([np.arange(0, n, 2), np.arange(1, n, 2)])
    return jnp.asarray((np.arange(n)[:, None] == src[None, :]).astype(np.float32), dtype=BF16)


def _block_diag(x):
    C, G, r, c = x.shape
    eye = jnp.eye(G, dtype=x.dtype)
    return jnp.einsum("cgij,gh->cgihj", x, eye).reshape(C, G * r, G * c)


def _prep_layer(P, l):
    f = lambda name: P[name][l]
    lw = {}
    w_in = f("w_in")
    offs = np.cumsum((0,) + IN_SIZES)
    cols = lambda j: w_in[:, offs[j]:offs[j + 1]]
    w_q, w_kv = cols(0), cols(1)
    w_ckv, w_kpe = w_kv[:, :MLA_KV_RANK], w_kv[:, MLA_KV_RANK:]
    half = MLA_ROPE // 2
    zpad = jnp.zeros((D_MODEL, LANES - MLA_ROPE), F32)
    w_kpe_rot = jnp.concatenate([-w_kpe[:, half:], w_kpe[:, :half]], 1)
    lw["w_mla"] = jnp.concatenate([w_q, w_ckv, w_kpe, zpad, w_kpe_rot, zpad], 1).astype(BF16)
    lw["w_rw"] = cols(2).astype(BF16)
    lw["w_su"] = cols(3).astype(BF16)
    lw["w_gates"] = jnp.concatenate([cols(4), cols(5), cols(6)], 1).astype(BF16)
    lw["q_norm"] = f("mla_q_a_norm").reshape(1, -1)
    lw["kv_norm"] = f("mla_kv_a_norm").reshape(1, -1)
    wqb = f("mla_w_q_b").reshape(MLA_Q_RANK, MLA_HEADS, MLA_NOPE + MLA_ROPE)
    nope, x1, x2 = wqb[..., :MLA_NOPE], wqb[..., MLA_NOPE:MLA_NOPE + half], wqb[..., MLA_NOPE + half:]
    z32 = jnp.zeros_like(wqb[..., :HEAD_PAD - MLA_NOPE - MLA_ROPE])
    plain = jnp.concatenate([x1, x2, nope, z32], -1).reshape(MLA_Q_RANK, -1)
    rot = jnp.concatenate([-x2, x1, jnp.zeros_like(nope), z32], -1).reshape(MLA_Q_RANK, -1)
    lw["w_qb"] = jnp.concatenate([plain, rot], 1).astype(BF16)
    wkvb = f("mla_w_kv_b").reshape(MLA_KV_RANK, MLA_HEADS, MLA_NOPE + MLA_V)
    k_nope, v = wkvb[..., :MLA_NOPE], wkvb[..., MLA_NOPE:]
    zk = jnp.zeros_like(k_nope[..., :MLA_ROPE])
    lw["w_k"] = jnp.concatenate([zk, k_nope, zk], -1).reshape(MLA_KV_RANK, -1).astype(BF16)
    lw["w_v"] = v.reshape(MLA_KV_RANK, -1).astype(BF16)
    row = lambda name: f(name).reshape(1, -1)
    lw["mu"] = row("rwkv_mu")
    lw["w0"] = row("rwkv_w0")
    z64 = jnp.zeros((64, RWKV_DIM), F32)
    lw["w_up"] = jnp.concatenate([f("rwkv_w_up"), z64], 0).astype(BF16)
    lw["a_up"] = jnp.concatenate([z64, f("rwkv_a_up")], 0).astype(BF16)
    lw["a0"] = row("rwkv_a0")
    lw["g_up"] = f("rwkv_g_up").astype(BF16)
    lw["k_k"] = row("rwkv_k_k")
    lw["k_a"] = row("rwkv_k_a")
    lw["r_k"] = row("rwkv_r_k")
    lw["lnx_g"] = row("rwkv_lnx_g")
    lw["lnx_b"] = row("rwkv_lnx_b")
    lw["ones_bd"] = _ones_block_diag()
    ab_re, ab_im, bb_re, bb_im = _s5_params(f("ssm_a_re"), f("ssm_a_im"), f("ssm_log_dt"),
                                            f("ssm_b_re"), f("ssm_b_im"))
    gpc = SSM_GROUPS // SSM_CHUNKS
    chunked = lambda t: t.reshape(SSM_CHUNKS, gpc, t.shape[1], t.shape[2])
    lw["s5_bre"] = _block_diag(chunked(jnp.transpose(bb_re, (1, 0, 2)))).astype(BF16)
    lw["s5_bim"] = _block_diag(chunked(jnp.transpose(bb_im, (1, 0, 2)))).astype(BF16)
    lw["s5_cre"] = _block_diag(chunked(jnp.transpose(f("ssm_c_re"), (0, 2, 1)))).astype(BF16)
    lw["s5_cim"] = _block_diag(chunked(jnp.transpose(f("ssm_c_im"), (0, 2, 1)))).astype(BF16)
    lw["s5_d"] = row("ssm_d")
    lw["s5_are"] = ab_re.reshape(1, -1)
    lw["s5_aim"] = ab_im.reshape(1, -1)
    lw["glu_w"] = f("ssm_glu_w").astype(BF16)
    lw["glu_b"] = row("ssm_glu_b")
    for name in ("w_br_a", "w_br_b", "w_br_c", "w_out"):
        lw[name] = f(name).astype(BF16)
    for name in ("ln1_g", "ln1_b", "ln2_g", "ln2_b"):
        lw[name] = row(name)
    rw_ = jnp.pad(f("router_w"), ((0, 0), (0, LANES - N_EXPERTS)))
    hi = rw_.astype(BF16)
    lw["router_hi"] = hi
    lw["router_lo"] = (rw_ - hi.astype(F32)).astype(BF16)
    lw["router_b"] = jnp.pad(f("router_b"), (0, LANES - N_EXPERTS), constant_values=NEG_INF).reshape(1, -1)
    wgu = f("exp_w_gu").reshape(N_EXPERTS * D_MODEL, 2 * D_FF)
    wgu = _matmul(wgu, _deinterleave_matrix(), 1, wgu.shape[0], out_dtype=BF16)
    lw["w_gu"] = wgu.reshape(N_EXPERTS, D_MODEL, 2 * D_FF)
    bgu = f("exp_b_gu")
    lw["b_gu"] = jnp.concatenate([bgu[..., 0::2], bgu[..., 1::2]], -1).reshape(N_EXPERTS, 1, 2 * D_FF)
    lw["w_down"] = f("exp_w_down").astype(BF16)
    lw["b_down"] = f("exp_b_down").reshape(N_EXPERTS, 1, D_MODEL)
    return lw


def _rope_tables(pos):
    half = MLA_ROPE // 2
    inv = ROPE_THETA ** (-jnp.arange(half, dtype=F32) / half)
    ang = pos.astype(F32)[:, None] * inv
    cos, sin = jnp.cos(ang), jnp.sin(ang)
    n = pos.shape[0]
    tc = jnp.concatenate([cos, cos, jnp.ones((n, MLA_NOPE), F32),
                          jnp.zeros((n, HEAD_PAD - MLA_NOPE - MLA_ROPE), F32)], 1)
    ts = jnp.concatenate([sin, sin, jnp.zeros((n, HEAD_PAD - MLA_ROPE), F32)], 1)
    return tc, ts


def _to_lanes(t, nb, s, lanes):
    t = t.reshape(s, nb * RWKV_HEADS, RWKV_HEAD).transpose(0, 2, 1)
    return jnp.pad(t, ((0, 0), (0, 0), (0, lanes - nb * RWKV_HEADS)))


def _layer(x, lw, past, nb, s, tables, att_tk=LANES):
    T = nb * s
    tc, ts = tables
    if past is None:
        start = 0
        shift_p = jnp.zeros((nb, RWKV_PROJ), F32)
        wkv_p = jnp.zeros((nb, RWKV_HEADS, RWKV_HEAD, RWKV_HEAD), F32)
        sre_p = jnp.zeros((nb, SSM_CH), F32)
        sim_p = jnp.zeros((nb, SSM_CH), F32)
    else:
        ckv_p, kpe_p, shift_p, wkv_p, sre_p, sim_p = past
        start = ckv_p.shape[1]
        sre_p = sre_p.reshape(nb, SSM_CH)
        sim_p = sim_p.reshape(nb, SSM_CH)

    q, ckv, kpe128 = _mla_in(x, lw, tc, ts)
    if past is None:
        sk = s
        ckv_all, kpe_all = ckv, kpe128
    else:
        sk = start + s
        ckv_all = jnp.concatenate([ckv_p, ckv.reshape(nb, s, -1)], 1)
        kpe_new = kpe128.reshape(nb, s, LANES)
        kpe_all = jnp.concatenate([jnp.pad(kpe_p, ((0, 0), (0, 0), (0, LANES - MLA_ROPE))), kpe_new], 1)
    skp = -(-sk // att_tk) * att_tk
    if skp != sk:
        ckv_all = jnp.pad(ckv_all.reshape(nb, sk, -1), ((0, 0), (0, skp - sk), (0, 0)))
        kpe_all = jnp.pad(kpe_all.reshape(nb, sk, -1), ((0, 0), (0, skp - sk), (0, 0)))
    k_pad, v_all = _kv_expand(ckv_all.reshape(nb * skp, -1), kpe_all.reshape(nb * skp, -1), lw)
    ya = _attention(q, k_pad, v_all, nb, s, skp, sk, start)

    rw = _matmul(x, lw["w_rw"], nb, s, tn=896)
    r, wdec, kn, v, kk, kb, g, bonus = _rwkv_pre(rw, shift_p, lw, nb, s)
    lanes = -(-nb * RWKV_HEADS // LANES) * LANES
    seqs = [_to_lanes(t, nb, s, lanes) for t in (r, wdec, kn, v, kk, kb)]
    s0 = jnp.transpose(wkv_p, (3, 2, 0, 1)).reshape(RWKV_HEAD, RWKV_HEAD, nb * RWKV_HEADS)
    s0 = jnp.pad(s0, ((0, 0), (0, 0), (0, lanes - nb * RWKV_HEADS)))
    o_l, sT = _rwkv_scan(seqs, s0, s)
    o_tm = o_l[:, :, :nb * RWKV_HEADS].transpose(0, 2, 1).reshape(s, nb * RWKV_DIM)
    wkv_n = (sT[:, :, :nb * RWKV_HEADS].reshape(RWKV_HEAD, RWKV_HEAD, nb, RWKV_HEADS).transpose(2, 3, 1, 0))
    shift_n = rw.reshape(nb, s, RWKV_PROJ)[:, -1]

    su_tm = _matmul(x, lw["w_su"], nb, s, time_major_out=True).reshape(s * nb, SSM_DIM)
    yc_tm, sre_n, sim_n = _s5(su_tm, sre_p, sim_p, lw, nb, s)
    yc_tm = yc_tm.reshape(s, nb * SSM_DIM)

    x1 = _merge(x, ya, o_tm, bonus, g, yc_tm, lw, nb, s)
    x2 = _moe(x1, lw)
    new = (ckv.reshape(nb, s, MLA_KV_RANK), kpe128[:, :MLA_ROPE].reshape(nb, s, MLA_ROPE), shift_n, wkv_n,
           sre_n.reshape(nb, SSM_GROUPS, SSM_STATE), sim_n.reshape(nb, SSM_GROUPS, SSM_STATE))
    return x2, new


def _trunk(x3, weights, caches):
    nb, s, d = x3.shape
    start = 0 if caches is None else caches[0].shape[2]
    tc, ts = _rope_tables(start + jnp.arange(s))
    tables = (jnp.tile(tc, (nb, 1)), jnp.tile(ts, (nb, 1)))
    x = x3.reshape(nb * s, d)
    new = []
    for l in range(len(weights)):
        past = None if caches is None else tuple(c[l] for c in caches)
        x, st = _layer(x, weights[l], past, nb, s, tables)
        new.append(st)
    return (x.reshape(nb, s, d),) + tuple(jnp.stack([st[j] for st in new]) for j in range(6))


def kernel(x_prompt, x_sample, cache_mla_ckv, cache_mla_kpe, state_rwkv_shift, state_rwkv_wkv, state_ssm_re, state_ssm_im, w_in, mla_q_a_norm, mla_w_q_b, mla_kv_a_norm, mla_w_kv_b, rwkv_mu, rwkv_w0, rwkv_w_up, rwkv_a0, rwkv_a_up, rwkv_g_up, rwkv_k_k, rwkv_k_a, rwkv_r_k, rwkv_lnx_g, rwkv_lnx_b, ssm_a_re, ssm_a_im, ssm_b_re, ssm_b_im, ssm_c_re, ssm_c_im, ssm_d, ssm_log_dt, ssm_glu_w, ssm_glu_b, w_br_a, w_br_b, w_br_c, w_out, ln1_g, ln1_b, router_w, router_b, exp_w_gu, exp_b_gu, exp_w_down, exp_b_down, ln2_g, ln2_b):
    P = dict(w_in=w_in, mla_q_a_norm=mla_q_a_norm, mla_w_q_b=mla_w_q_b, mla_kv_a_norm=mla_kv_a_norm,
             mla_w_kv_b=mla_w_kv_b, rwkv_mu=rwkv_mu, rwkv_w0=rwkv_w0, rwkv_w_up=rwkv_w_up, rwkv_a0=rwkv_a0,
             rwkv_a_up=rwkv_a_up, rwkv_g_up=rwkv_g_up, rwkv_k_k=rwkv_k_k, rwkv_k_a=rwkv_k_a, rwkv_r_k=rwkv_r_k,
             rwkv_lnx_g=rwkv_lnx_g, rwkv_lnx_b=rwkv_lnx_b, ssm_a_re=ssm_a_re, ssm_a_im=ssm_a_im,
             ssm_b_re=ssm_b_re, ssm_b_im=ssm_b_im, ssm_c_re=ssm_c_re, ssm_c_im=ssm_c_im, ssm_d=ssm_d,
             ssm_log_dt=ssm_log_dt, ssm_glu_w=ssm_glu_w, ssm_glu_b=ssm_glu_b, w_br_a=w_br_a, w_br_b=w_br_b,
             w_br_c=w_br_c, w_out=w_out, ln1_g=ln1_g, ln1_b=ln1_b, router_w=router_w, router_b=router_b,
             exp_w_gu=exp_w_gu, exp_b_gu=exp_b_gu, exp_w_down=exp_w_down, exp_b_down=exp_b_down,
             ln2_g=ln2_g, ln2_b=ln2_b)
    depth = w_in.shape[0]
    weights = [_prep_layer(P, l) for l in range(depth)]
    outs_p = _trunk(x_prompt, weights, None)
    caches = (cache_mla_ckv, cache_mla_kpe, state_rwkv_shift, state_rwkv_wkv, state_ssm_re, state_ssm_im)
    outs_s = _trunk(x_sample, weights, caches)
    return (outs_p[0], outs_s[0]) + outs_p[1:] + outs_s[1:]
```

```python
import functools
import math

import numpy as np
import jax
import jax.numpy as jnp
from jax import lax
from jax.experimental import pallas as pl
from jax.experimental.pallas import tpu as pltpu

F32 = jnp.float32
BF16 = jnp.bfloat16

D_MODEL = 1024
CHUNK = 64
MLA_HEADS = 8
MLA_NOPE = 64
MLA_ROPE = 32
MLA_V = 64
MLA_Q_RANK = 384
MLA_KV_RANK = 256
ROPE_THETA = 10000.0
HEAD_PAD = 128
RWKV_HEADS = 8
RWKV_HEAD = 64
RWKV_DIM = RWKV_HEADS * RWKV_HEAD
RWKV_PROJ = 3 * RWKV_DIM + 64 + 64 + 128
RWKV_LN_EPS = 64e-5
SSM_DIM = 512
SSM_GROUP = 16
SSM_GROUPS = 32
SSM_STATE = 64
SSM_CHUNKS = 4
SSM_CH = SSM_GROUPS * SSM_STATE
IN_SIZES = (MLA_Q_RANK, MLA_KV_RANK + MLA_ROPE, RWKV_PROJ, SSM_DIM, D_MODEL, D_MODEL, D_MODEL)
N_EXPERTS = 32
TOP_K = 4
D_FF = 512
SWIGLU_LIMIT = 7.0
SWIGLU_ALPHA = 1.702
DEPTH = 2
DN_ALPHA = (2 * DEPTH) ** 0.25
LN_EPS = 1e-5
RMS_EPS = 1e-6
NEG_INF = -1e30
ATT_SCALE = (MLA_NOPE + MLA_ROPE) ** -0.5
LANES = 128
VMEM_LIMIT = 48 * 1024 * 1024


def _cparams(*sem):
    return pltpu.CompilerParams(dimension_semantics=sem, vmem_limit_bytes=VMEM_LIMIT)


def _dot(a, b):
    return jnp.dot(a, b, preferred_element_type=F32)


def _sigmoid(x):
    return 1.0 / (1.0 + jnp.exp(-x))


def _layer_norm(x, g, b):
    mu = jnp.mean(x, -1, keepdims=True)
    xc = x - mu
    var = jnp.mean(xc * xc, -1, keepdims=True)
    return xc * lax.rsqrt(var + LN_EPS) * g + b


def _full(shape):
    n = len(shape)
    return pl.BlockSpec(shape, lambda *_: (0,) * n)


def _mm_kernel(x_ref, w_ref, o_ref):
    o_ref[...] = _dot(x_ref[...].astype(BF16), w_ref[...]).astype(o_ref.dtype)


def _matmul(x, w, nb, s, *, time_major_out=False, tm=512, tn=1024, out_dtype=F32):
    K = x.shape[1]
    N = w.shape[1]
    tm = min(tm, s)
    tn = min(tn, N)
    while N % tn:
        tn -= LANES
    nt = s // tm
    if time_major_out:
        assert tn == N
        out_shape = jax.ShapeDtypeStruct((s, nb * N), out_dtype)
        out_spec = pl.BlockSpec((tm, N), lambda b, i, j: (i, b))
    else:
        out_shape = jax.ShapeDtypeStruct((nb * s, N), out_dtype)
        out_spec = pl.BlockSpec((tm, tn), lambda b, i, j: (b * nt + i, j))
    return pl.pallas_call(
        _mm_kernel,
        grid=(nb, nt, N // tn),
        in_specs=[pl.BlockSpec((tm, K), lambda b, i, j: (b * nt + i, 0)),
                  pl.BlockSpec((K, tn), lambda b, i, j: (0, j))],
        out_specs=out_spec,
        out_shape=out_shape,
        compiler_params=_cparams("parallel", "parallel", "arbitrary"),
        name="matmul",
    )(x, w)


def _mla_in_kernel(x_ref, w_ref, qg_ref, kvg_ref, wqb_ref, tc_ref, ts_ref, q_ref, ckv_ref, kpe_ref):
    h = _dot(x_ref[...].astype(BF16), w_ref[...])
    tc = tc_ref[...]
    ts = ts_ref[...]
    qa = h[:, :MLA_Q_RANK]
    qn = qa * lax.rsqrt(jnp.mean(qa * qa, -1, keepdims=True) + RMS_EPS) * qg_ref[...]
    q2 = _dot(qn.astype(BF16), wqb_ref[...])
    hp = MLA_HEADS * HEAD_PAD
    for hd in range(MLA_HEADS):
        lo = hd * HEAD_PAD
        q0 = q2[:, lo:lo + HEAD_PAD]
        q1 = q2[:, hp + lo:hp + lo + HEAD_PAD]
        q_ref[:, lo:lo + HEAD_PAD] = ((q0 * tc + q1 * ts) * ATT_SCALE).astype(BF16)
    c0 = MLA_Q_RANK
    ckv = h[:, c0:c0 + MLA_KV_RANK]
    ckv_ref[...] = ckv * lax.rsqrt(jnp.mean(ckv * ckv, -1, keepdims=True) + RMS_EPS) * kvg_ref[...]
    c1 = c0 + MLA_KV_RANK
    kpe_ref[...] = h[:, c1:c1 + LANES] * tc + h[:, c1 + LANES:c1 + 2 * LANES] * ts


def _mla_in(x, lw, tc, ts, *, tm=512):
    T = x.shape[0]
    tm = min(tm, T)
    hp = MLA_HEADS * HEAD_PAD
    nw = lw["w_mla"].shape[1]
    return pl.pallas_call(
        _mla_in_kernel,
        grid=(T // tm,),
        in_specs=[pl.BlockSpec((tm, D_MODEL), lambda i: (i, 0)),
                  _full((D_MODEL, nw)), _full((1, MLA_Q_RANK)), _full((1, MLA_KV_RANK)),
                  _full((MLA_Q_RANK, 2 * hp)),
                  pl.BlockSpec((tm, LANES), lambda i: (i, 0)),
                  pl.BlockSpec((tm, LANES), lambda i: (i, 0))],
        out_specs=[pl.BlockSpec((tm, hp), lambda i: (i, 0)),
                   pl.BlockSpec((tm, MLA_KV_RANK), lambda i: (i, 0)),
                   pl.BlockSpec((tm, LANES), lambda i: (i, 0))],
        out_shape=[jax.ShapeDtypeStruct((T, hp), BF16),
                   jax.ShapeDtypeStruct((T, MLA_KV_RANK), F32),
                   jax.ShapeDtypeStruct((T, LANES), F32)],
        compiler_params=_cparams("parallel"),
        name="mla_in",
    )(x, lw["w_mla"], lw["q_norm"], lw["kv_norm"], lw["w_qb"], tc, ts)


def _kv_expand_kernel(ckv_ref, kpe_ref, wk_ref, wv_ref, k_ref, v_ref):
    c = ckv_ref[...].astype(BF16)
    k = _dot(c, wk_ref[...])
    kpe = kpe_ref[...]
    for hd in range(MLA_HEADS):
        lo = hd * HEAD_PAD
        k_ref[:, lo:lo + HEAD_PAD] = (k[:, lo:lo + HEAD_PAD] + kpe).astype(BF16)
    v_ref[...] = _dot(c, wv_ref[...]).astype(BF16)


def _kv_expand(ckv, kpe, lw, *, tm=512):
    T = ckv.shape[0]
    tm = min(tm, T)
    while T % tm:
        tm //= 2
    hp = MLA_HEADS * HEAD_PAD
    hv = MLA_HEADS * MLA_V
    return pl.pallas_call(
        _kv_expand_kernel,
        grid=(T // tm,),
        in_specs=[pl.BlockSpec((tm, MLA_KV_RANK), lambda i: (i, 0)),
                  pl.BlockSpec((tm, LANES), lambda i: (i, 0)),
                  _full((MLA_KV_RANK, hp)), _full((MLA_KV_RANK, hv))],
        out_specs=[pl.BlockSpec((tm, hp), lambda i: (i, 0)),
                   pl.BlockSpec((tm, hv), lambda i: (i, 0))],
        out_shape=[jax.ShapeDtypeStruct((T, hp), BF16), jax.ShapeDtypeStruct((T, hv), BF16)],
        compiler_params=_cparams("parallel"),
        name="kv_expand",
    )(ckv, kpe, lw["w_k"], lw["w_v"])


def _attn_kernel(q_ref, k_ref, v_ref, o_ref, *, tq, nq, q_start, sk):
    lane = lax.broadcasted_iota(jnp.int32, (tq, LANES), 1)
    nt = (((1,), (1,)), ((), ()))
    for i in range(nq):
        q_lo = q_start + i * tq
        k_end = min(((q_lo + tq - 1) // CHUNK + 1) * CHUNK, sk)
        kw = -(-k_end // LANES) * LANES
        c0 = min((q_lo // CHUNK + 1) * CHUNK, sk) // LANES * LANES
        tail = kw - c0
        if tail:
            q_chunk = (q_lo + lax.broadcasted_iota(jnp.int32, (tq, tail), 0)) // CHUNK
            k_pos = c0 + lax.broadcasted_iota(jnp.int32, (tq, tail), 1)
            visible = k_pos // CHUNK <= q_chunk
            if kw > sk:
                visible = visible & (k_pos < sk)
        outs = []
        for hh in range(2):
            hs = slice(hh * HEAD_PAD, (hh + 1) * HEAD_PAD)
            q = q_ref[i * tq:(i + 1) * tq, hs]
            parts = []
            if c0:
                parts.append(lax.dot_general(q, k_ref[0:c0, hs], nt, preferred_element_type=F32))
            if tail:
                s_t = lax.dot_general(q, k_ref[c0:kw, hs], nt, preferred_element_type=F32)
                parts.append(jnp.where(visible, s_t, NEG_INF))
            m = jnp.max(parts[0], -1, keepdims=True)
            for s_ in parts[1:]:
                m = jnp.maximum(m, jnp.max(s_, -1, keepdims=True))
            l = None
            acc = None
            lo = 0
            for s_ in parts:
                p = jnp.exp(s_ - m)
                ps = jnp.sum(p, -1, keepdims=True)
                pv = _dot(p.astype(BF16), v_ref[lo:lo + s_.shape[1], :])
                l = ps if l is None else l + ps
                acc = pv if acc is None else acc + pv
                lo += s_.shape[1]
            outs.append(acc / l)
        o_ref[i * tq:(i + 1) * tq, :] = jnp.where(lane < MLA_V, outs[0], outs[1])


def _attention(q, k, v, nb, sq, skp, sk, q_start, *, tq=256):
    tq = min(tq, sq)
    kern = functools.partial(_attn_kernel, tq=tq, nq=sq // tq, q_start=q_start, sk=sk)
    return pl.pallas_call(
        kern,
        grid=(nb, MLA_HEADS // 2),
        in_specs=[pl.BlockSpec((sq, 2 * HEAD_PAD), lambda b, j: (b, j)),
                  pl.BlockSpec((skp, 2 * HEAD_PAD), lambda b, j: (b, j)),
                  pl.BlockSpec((skp, 2 * MLA_V), lambda b, j: (b, j))],
        out_specs=pl.BlockSpec((sq, 2 * MLA_V), lambda b, j: (b, j)),
        out_shape=jax.ShapeDtypeStruct((nb * sq, MLA_HEADS * MLA_V), F32),
        compiler_params=_cparams("parallel", "parallel"),
        name="attention",
    )(q, k, v)


def _head_sum(z, ones_bd):
    hi = z.astype(BF16)
    lo = (z - hi.astype(F32)).astype(BF16)
    return _dot(hi, ones_bd) + _dot(lo, ones_bd)


def _rwkv_pre_kernel(x_ref, xprev_ref, shift_ref, wrw_ref, mu_ref, w0_ref, wup_ref, a0_ref, aup_ref, gup_ref,
                     kk_ref, ka_ref, rk_ref, ones_ref,
                     r_out, w_out, k_out, v_out, kk_out, b_out, g_out, bonus_out, last_out):
    p = _dot(x_ref[...].astype(BF16), wrw_ref[...])
    first = pl.program_id(1) == 0
    p_before = _dot(xprev_ref[...].astype(BF16), wrw_ref[...])
    prev_row = jnp.where(first, shift_ref[0], p_before[7:8, :])
    last_out[0] = p[p.shape[0] - 1:, :]
    row = lax.broadcasted_iota(jnp.int32, p.shape, 0)
    prev = jnp.where(row == 0, prev_row, pltpu.roll(p, 1, 0))
    ps = p + (prev - p) * mu_ref[...]
    d = RWKV_DIM
    r = ps[:, 0:d]
    k = ps[:, d:2 * d]
    v = ps[:, 2 * d:3 * d]
    wa = ps[:, 3 * d:3 * d + LANES]
    gd = ps[:, 3 * d + LANES:3 * d + 2 * LANES]
    ones_bd = ones_ref[...]
    wlin = w0_ref[...] + _dot(jnp.tanh(wa).astype(BF16), wup_ref[...])
    z = -wlin
    w = -(jnp.maximum(z, 0.0) + jnp.log1p(jnp.exp(-jnp.abs(z)))) - 0.5
    w_out[...] = jnp.exp(-jnp.exp(w))
    a = _sigmoid(a0_ref[...] + _dot(wa.astype(BF16), aup_ref[...]))
    g_out[...] = _dot(_sigmoid(gd).astype(BF16), gup_ref[...])
    kk = k * kk_ref[...]
    kkn = kk * lax.rsqrt(_head_sum(kk * kk, ones_bd) + 1e-12)
    kk_out[...] = kkn
    b_out[...] = kkn * a
    kn = k * (1.0 + (a - 1.0) * ka_ref[...])
    r_out[...] = r
    k_out[...] = kn
    v_out[...] = v
    bonus_out[...] = _head_sum(r * kn * rk_ref[...], ones_bd) * v


def _rwkv_pre(x, shift_prev, lw, nb, s, *, tm=256):
    tm = min(tm, s)
    tps = s // tm
    d = RWKV_DIM
    vec = lambda n: _full((1, n))
    bm = pl.BlockSpec((tm, d), lambda b, i: (b * tps + i, 0))
    tmaj = pl.BlockSpec((tm, d), lambda b, i: (i, b))
    per_seq = pl.BlockSpec((1, 1, RWKV_PROJ), lambda b, i: (b, 0, 0))
    bm_shape = jax.ShapeDtypeStruct((nb * s, d), F32)
    tm_shape = jax.ShapeDtypeStruct((s, nb * d), F32)
    return pl.pallas_call(
        _rwkv_pre_kernel,
        grid=(nb, tps),
        in_specs=[pl.BlockSpec((tm, D_MODEL), lambda b, i: (b * tps + i, 0)),
                  pl.BlockSpec((8, D_MODEL), lambda b, i: (jnp.maximum((b * tps + i) * (tm // 8) - 1, 0), 0)),
                  per_seq, _full((D_MODEL, RWKV_PROJ)),
                  vec(RWKV_PROJ), vec(d), _full((LANES, d)), vec(d), _full((LANES, d)),
                  _full((LANES, d)), vec(d), vec(d), vec(d), _full((d, d))],
        out_specs=[tmaj] * 6 + [bm] * 2 + [per_seq],
        out_shape=[tm_shape] * 6 + [bm_shape] * 2 + [jax.ShapeDtypeStruct((nb, 1, RWKV_PROJ), F32)],
        compiler_params=_cparams("parallel", "arbitrary"),
        name="rwkv_pre",
    )(x, x, shift_prev.reshape(nb, 1, RWKV_PROJ), lw["w_rw"], lw["mu"], lw["w0"], lw["w_up"], lw["a0"],
      lw["a_up"], lw["g_up"], lw["k_k"], lw["k_a"], lw["r_k"], lw["ones_bd"])


RWKV_VC = 32


def _rwkv_scan_kernel(r_ref, w_ref, k_ref, v_ref, kk_ref, b_ref, s0_ref, o_ref, sT_ref, st_ref, *, tt):
    n = RWKV_HEAD

    @pl.when(pl.program_id(0) == 0)
    def _():
        st_ref[...] = s0_ref[...]

    def step(t, carry):
        for c in range(n // RWKV_VC):
            vs = slice(c * RWKV_VC, (c + 1) * RWKV_VC)
            parts = [None] * 4
            for q in range(n):
                term = st_ref[q, vs, :] * kk_ref[t, q:q + 1, :]
                parts[q % 4] = term if parts[q % 4] is None else parts[q % 4] + term
            sa = -((parts[0] + parts[1]) + (parts[2] + parts[3]))
            vt = v_ref[t, vs, :]
            outs = [None] * 4
            for q in range(n):
                s_new = (st_ref[q, vs, :] * w_ref[t, q:q + 1, :] + sa * b_ref[t, q:q + 1, :]
                         + vt * k_ref[t, q:q + 1, :])
                st_ref[q, vs, :] = s_new
                term = s_new * r_ref[t, q:q + 1, :]
                outs[q % 4] = term if outs[q % 4] is None else outs[q % 4] + term
            o_ref[t, vs, :] = (outs[0] + outs[1]) + (outs[2] + outs[3])
        return carry

    lax.fori_loop(0, tt, step, 0)

    @pl.when(pl.program_id(0) == pl.num_programs(0) - 1)
    def _():
        sT_ref[...] = st_ref[...]


def _rwkv_scan(seqs, s0, s, *, tt=16):
    L = s0.shape[-1]
    n = RWKV_HEAD
    tt = min(tt, s)
    blk = pl.BlockSpec((tt, n, L), lambda i: (i, 0, 0))
    kern = functools.partial(_rwkv_scan_kernel, tt=tt)
    return pl.pallas_call(
        kern,
        grid=(s // tt,),
        in_specs=[blk] * 6 + [_full((n, n, L))],
        out_specs=[blk, _full((n, n, L))],
        out_shape=[jax.ShapeDtypeStruct((s, n, L), F32), jax.ShapeDtypeStruct((n, n, L), F32)],
        scratch_shapes=[pltpu.VMEM((n, n, L), F32)],
        compiler_params=_cparams("arbitrary"),
        name="rwkv_scan",
    )(*seqs, s0)


def _s5_params_kernel(are_ref, aim_ref, ldt_ref, bre_ref, bim_ref, abre_ref, abim_ref, bbre_ref, bbim_ref):
    lr = are_ref[...]
    li = aim_ref[...]
    dt = jnp.exp(ldt_ref[...])
    mag = jnp.exp(lr * dt)
    ab_re = mag * jnp.cos(li * dt)
    ab_im = mag * jnp.sin(li * dt)
    den = lr * lr + li * li
    f_re = ((ab_re - 1.0) * lr + ab_im * li) / den
    f_im = (ab_im * lr - (ab_re - 1.0) * li) / den
    abre_ref[...] = ab_re
    abim_ref[...] = ab_im
    for i in range(SSM_GROUP):
        br = bre_ref[i]
        bi = bim_ref[i]
        bbre_ref[i] = f_re * br - f_im * bi
        bbim_ref[i] = f_re * bi + f_im * br


def _s5_params(a_re, a_im, log_dt, b_re, b_im):
    g, n = a_re.shape
    gn = jax.ShapeDtypeStruct((g, n), F32)
    ign = jax.ShapeDtypeStruct((SSM_GROUP, g, n), F32)
    return pl.pallas_call(
        _s5_params_kernel,
        out_shape=[gn, gn, ign, ign],
        name="s5_params",
    )(a_re, a_im, log_dt.reshape(g, 1), jnp.transpose(b_re, (2, 0, 1)), jnp.transpose(b_im, (2, 0, 1)))


def _gelu_tanh(x):
    return 0.5 * x * (1.0 + jnp.tanh(math.sqrt(2.0 / math.pi) * (x + 0.044715 * (x * x * x))))


def _s5_kernel(u_ref, bre_ref, bim_ref, cre_ref, cim_ref, d_ref, are_ref, aim_ref, h0re_ref, h0im_ref,
               gw_ref, gb_ref, y_ref, hTre_ref, hTim_ref, hre_s, him_s, sre_s, sim_s, yy_s, *, tt, nb):
    cw = SSM_CH // SSM_CHUNKS

    @pl.when(pl.program_id(0) == 0)
    def _():
        hre_s[...] = h0re_ref[...]
        him_s[...] = h0im_ref[...]

    for c in range(SSM_CHUNKS):
        uc = u_ref[:, c * LANES:(c + 1) * LANES]
        ub = uc.astype(BF16)
        sre_s[...] = _dot(ub, bre_ref[c])
        sim_s[...] = _dot(ub, bim_ref[c])
        a_re = jnp.broadcast_to(are_ref[:, c * cw:(c + 1) * cw], (nb, cw))
        a_im = jnp.broadcast_to(aim_ref[:, c * cw:(c + 1) * cw], (nb, cw))

        def step(t, carry, a_re=a_re, a_im=a_im):
            hr, hi = carry
            rows = pl.ds(pl.multiple_of(t * nb, nb), nb)
            nr = a_re * hr - a_im * hi + sre_s[rows, :]
            ni = a_re * hi + a_im * hr + sim_s[rows, :]
            sre_s[rows, :] = nr
            sim_s[rows, :] = ni
            return nr, ni

        hr, hi = lax.fori_loop(0, tt, step, (hre_s[:, c * cw:(c + 1) * cw], him_s[:, c * cw:(c + 1) * cw]))
        hre_s[:, c * cw:(c + 1) * cw] = hr
        him_s[:, c * cw:(c + 1) * cw] = hi
        yc = _dot(sre_s[...].astype(BF16), cre_ref[c]) - _dot(sim_s[...].astype(BF16), cim_ref[c])
        yy_s[:, c * LANES:(c + 1) * LANES] = yc + d_ref[:, c * LANES:(c + 1) * LANES] * uc

    y = _gelu_tanh(yy_s[...])
    z = _dot(y.astype(BF16), gw_ref[...]) + gb_ref[...]
    y_ref[...] = z[:, :SSM_DIM] * _sigmoid(z[:, SSM_DIM:])

    @pl.when(pl.program_id(0) == pl.num_programs(0) - 1)
    def _():
        hTre_ref[...] = hre_s[...]
        hTim_ref[...] = him_s[...]


def _s5(u_tm, h0_re, h0_im, lw, nb, s, *, rows=512):
    tt = max(min(rows // nb, s), 1)
    R = tt * nb
    cw = SSM_CH // SSM_CHUNKS
    kern = functools.partial(_s5_kernel, tt=tt, nb=nb)
    return pl.pallas_call(
        kern,
        grid=(s // tt,),
        in_specs=[pl.BlockSpec((R, SSM_DIM), lambda i: (i, 0)),
                  _full((SSM_CHUNKS, LANES, cw)), _full((SSM_CHUNKS, LANES, cw)),
                  _full((SSM_CHUNKS, cw, LANES)), _full((SSM_CHUNKS, cw, LANES)),
                  _full((1, SSM_DIM)), _full((1, SSM_CH)), _full((1, SSM_CH)),
                  _full((nb, SSM_CH)), _full((nb, SSM_CH)),
                  _full((SSM_DIM, 2 * SSM_DIM)), _full((1, 2 * SSM_DIM))],
        out_specs=[pl.BlockSpec((R, SSM_DIM), lambda i: (i, 0)), _full((nb, SSM_CH)), _full((nb, SSM_CH))],
        out_shape=[jax.ShapeDtypeStruct((s * nb, SSM_DIM), F32),
                   jax.ShapeDtypeStruct((nb, SSM_CH), F32), jax.ShapeDtypeStruct((nb, SSM_CH), F32)],
        scratch_shapes=[pltpu.VMEM((nb, SSM_CH), F32), pltpu.VMEM((nb, SSM_CH), F32),
                        pltpu.VMEM((R, cw), F32), pltpu.VMEM((R, cw), F32), pltpu.VMEM((R, SSM_DIM), F32)],
        compiler_params=_cparams("arbitrary"),
        name="s5",
    )(u_tm, lw["s5_bre"], lw["s5_bim"], lw["s5_cre"], lw["s5_cim"], lw["s5_d"], lw["s5_are"], lw["s5_aim"],
      h0_re, h0_im, lw["glu_w"], lw["glu_b"])


def _merge_kernel(x_ref, ya_ref, o_ref, bonus_ref, g_ref, yc_ref, wg_ref, lng_ref, lnb_ref, ones_ref,
                  wa_ref, wb_ref, wc_ref, wo_ref, ln1g_ref, ln1b_ref, out_ref):
    ones_bd = ones_ref[...]
    o = o_ref[...]
    inv_n = 1.0 / RWKV_HEAD
    mean = _head_sum(o, ones_bd) * inv_n
    oc = o - mean
    var = _head_sum(oc * oc, ones_bd) * inv_n
    yb = (oc * lax.rsqrt(var + RWKV_LN_EPS) * lng_ref[...] + lnb_ref[...] + bonus_ref[...]) * g_ref[...]
    d = D_MODEL
    x = x_ref[...]
    xb = x.astype(BF16)
    merged = None
    for j, (y, w_ref) in enumerate(((ya_ref[...], wa_ref), (yb, wb_ref), (yc_ref[...], wc_ref))):
        gate = _sigmoid(_dot(xb, wg_ref[:, j * d:(j + 1) * d]))
        term = gate * _dot(y.astype(BF16), w_ref[...])
        merged = term if merged is None else merged + term
    y = DN_ALPHA * x + _dot(merged.astype(BF16), wo_ref[...])
    out_ref[...] = _layer_norm(y, ln1g_ref[...], ln1b_ref[...])


def _merge(x, ya, o_tm, bonus, g, yc_tm, lw, nb, s, *, tm=256):
    tm = min(tm, s)
    nt = s // tm
    d = D_MODEL
    h = RWKV_DIM
    row = lambda w: pl.BlockSpec((tm, w), lambda b, i: (b * nt + i, 0))
    tmaj = lambda w: pl.BlockSpec((tm, w), lambda b, i: (i, b))
    vec = lambda n: _full((1, n))
    return pl.pallas_call(
        _merge_kernel,
        grid=(nb, nt),
        in_specs=[row(d), row(h), tmaj(h), row(h), row(h), tmaj(SSM_DIM),
                  _full((d, 3 * d)), vec(h), vec(h), _full((h, h)),
                  _full((h, d)), _full((h, d)), _full((h, d)), _full((d, d)), vec(d), vec(d)],
        out_specs=row(d),
        out_shape=jax.ShapeDtypeStruct((nb * s, d), F32),
        compiler_params=_cparams("parallel", "parallel"),
        name="merge",
    )(x, ya, o_tm, bonus, g, yc_tm, lw["w_gates"], lw["lnx_g"], lw["lnx_b"], lw["ones_bd"],
      lw["w_br_a"], lw["w_br_b"], lw["w_br_c"], lw["w_out"], lw["ln1_g"], lw["ln1_b"])


INFO_GATE, INFO_EID, INFO_RANK = 0, 4, 8


def _router_kernel(x_ref, whi_ref, wlo_ref, b_ref, tri_ref, info_ref, cnt_ref, base_s):
    @pl.when(pl.program_id(0) == 0)
    def _():
        base_s[...] = jnp.zeros_like(base_s)

    x = x_ref[...]
    xh = x.astype(BF16)
    xl = (x - xh.astype(F32)).astype(BF16)
    logits = _dot(xh, whi_ref[...]) + _dot(xl, whi_ref[...]) + _dot(xh, wlo_ref[...]) + b_ref[...]
    lane_i = lax.broadcasted_iota(jnp.int32, logits.shape, 1)
    lane = lane_i.astype(F32)
    vals, sels, ids = [], [], []
    for _ in range(TOP_K):
        m = jnp.max(logits, -1, keepdims=True)
        idx = jnp.min(jnp.where(logits == m, lane, float(LANES)), -1, keepdims=True)
        sel = lane == idx
        vals.append(m)
        sels.append(sel)
        ids.append(idx)
        logits = jnp.where(sel, -3e38, logits)
    es = [jnp.exp(v - vals[0]) for v in vals]
    den = es[0] + es[1] + es[2] + es[3]
    chosen = jnp.zeros_like(logits)
    for sel in sels:
        chosen = chosen + jnp.where(sel, 1.0, 0.0)
    before = _dot(tri_ref[...], chosen.astype(BF16)) + base_s[...]
    info = jnp.zeros_like(logits)
    for k in range(TOP_K):
        rank = jnp.sum(jnp.where(sels[k], before, 0.0), -1, keepdims=True)
        info = (info + jnp.where(lane_i == INFO_GATE + k, es[k] / den, 0.0)
                + jnp.where(lane_i == INFO_EID + k, ids[k], 0.0)
                + jnp.where(lane_i == INFO_RANK + k, rank, 0.0))
    info_ref[...] = info
    base_s[...] += jnp.sum(chosen, 0, keepdims=True)
    cnt_ref[...] = base_s[...]


def _router(x, lw, *, tm=512):
    T = x.shape[0]
    tm = min(tm, T)
    tri = jnp.asarray(np.tril(np.ones((tm, tm), np.float32), -1), dtype=BF16)
    return pl.pallas_call(
        _router_kernel,
        grid=(T // tm,),
        in_specs=[pl.BlockSpec((tm, D_MODEL), lambda i: (i, 0)),
                  _full((D_MODEL, LANES)), _full((D_MODEL, LANES)), _full((1, LANES)), _full((tm, tm))],
        out_specs=[pl.BlockSpec((tm, LANES), lambda i: (i, 0)), _full((1, LANES))],
        out_shape=[jax.ShapeDtypeStruct((T, LANES), F32), jax.ShapeDtypeStruct((1, LANES), F32)],
        scratch_shapes=[pltpu.VMEM((1, LANES), F32)],
        compiler_params=_cparams("arbitrary"),
        name="router",
    )(x, lw["router_hi"], lw["router_lo"], lw["router_b"], tri)


def _moe_schedule(info, cnt, tm_e, nt):
    eid = info[:, INFO_EID:INFO_EID + TOP_K].astype(jnp.int32)
    rank = info[:, INFO_RANK:INFO_RANK + TOP_K].astype(jnp.int32)
    counts = cnt[0, :N_EXPERTS].astype(jnp.int32)
    gsz = (counts + tm_e - 1) // tm_e * tm_e
    gend = jnp.cumsum(gsz)
    goff = gend - gsz
    onehot = eid[..., None] == jnp.arange(N_EXPERTS, dtype=jnp.int32)
    pos = jnp.sum(jnp.where(onehot, goff, 0), -1) + rank
    tile_start = jnp.arange(nt, dtype=jnp.int32) * tm_e
    tile_e = jnp.minimum(jnp.sum((gend[None, :] <= tile_start[:, None]).astype(jnp.int32), -1), N_EXPERTS - 1)
    n_valid = (gend[-1] // tm_e).reshape(1)
    i32 = lambda a: a.astype(jnp.int32)
    return i32(pos.reshape(-1)), i32(tile_e), i32(n_valid), i32(gend), i32(gsz)


def _dispatch_kernel(gend_ref, gsz_ref, nv_ref, pos_ref, x_ref, xs_ref, zbuf, sem, zsem, *, tm, tm_e, nt):
    @pl.when(pl.program_id(0) == 0)
    def _():
        zbuf[...] = jnp.zeros_like(zbuf)
        fill = lambda row0: pltpu.make_async_copy(zbuf, xs_ref.at[pl.ds(pl.multiple_of(row0, tm_e), tm_e)], zsem)
        for phase in ("start", "wait"):
            for e in range(N_EXPERTS):
                @pl.when(gsz_ref[e] > 0)
                def _(e=e, phase=phase):
                    getattr(fill(gend_ref[e] - tm_e), phase)()

            def tail(j, carry, phase=phase):
                getattr(fill(j * tm_e), phase)()
                return carry

            lax.fori_loop(nv_ref[0], nt, tail, 0)

    def issue(t, carry):
        for k in range(TOP_K):
            p = pos_ref[t * TOP_K + k]
            pltpu.make_async_copy(x_ref.at[pl.ds(t, 1)], xs_ref.at[pl.ds(p, 1)], sem).start(priority=k % 2)
        return carry

    lax.fori_loop(0, tm, issue, 0)
    for k in range(TOP_K):
        pltpu.make_async_copy(x_ref, xs_ref.at[pl.ds(0, tm)], sem).wait()


def _dispatch(x, pos, gend, gsz, n_valid, tm_e, nt, *, tm=512):
    T, d = x.shape
    tm = min(tm, T)
    kern = functools.partial(_dispatch_kernel, tm=tm, tm_e=tm_e, nt=nt)
    return pl.pallas_call(
        kern,
        grid_spec=pltpu.PrefetchScalarGridSpec(
            num_scalar_prefetch=3,
            grid=(T // tm,),
            in_specs=[pl.BlockSpec((tm * TOP_K,), lambda i, *_: (i,), memory_space=pltpu.SMEM),
                      pl.BlockSpec((tm, d), lambda i, *_: (i, 0))],
            out_specs=pl.BlockSpec(memory_space=pl.ANY),
            scratch_shapes=[pltpu.VMEM((tm_e, d), F32), pltpu.SemaphoreType.DMA, pltpu.SemaphoreType.DMA]),
        out_shape=jax.ShapeDtypeStruct((nt * tm_e, d), F32),
        compiler_params=_cparams("arbitrary"),
        name="moe_dispatch",
    )(gend, gsz, n_valid, pos, x)


def _expert_kernel(te_ref, nv_ref, xs_ref, wgu_ref, bgu_ref, wd_ref, bd_ref, ys_ref):
    del te_ref

    @pl.when(pl.program_id(0) < nv_ref[0])
    def _():
        h = _dot(xs_ref[...].astype(BF16), wgu_ref[0]) + bgu_ref[0]
        hg = jnp.minimum(h[:, :D_FF], SWIGLU_LIMIT)
        hl = jnp.clip(h[:, D_FF:], -SWIGLU_LIMIT, SWIGLU_LIMIT)
        act = hg * _sigmoid(SWIGLU_ALPHA * hg) * (hl + 1.0)
        ys_ref[...] = _dot(act.astype(BF16), wd_ref[0].astype(BF16)) + bd_ref[0]

    @pl.when(pl.program_id(0) >= nv_ref[0])
    def _():
        ys_ref[...] = jnp.zeros_like(ys_ref)


def _experts(xs, tile_e, n_valid, lw, tm_e):
    rows, d = xs.shape
    nt = rows // tm_e
    tile = lambda j, te, nv: (jnp.minimum(j, nv[0] - 1), 0)
    wsel = lambda j, te, nv: (te[jnp.minimum(j, nv[0] - 1)], 0, 0)
    return pl.pallas_call(
        _expert_kernel,
        grid_spec=pltpu.PrefetchScalarGridSpec(
            num_scalar_prefetch=2,
            grid=(nt,),
            in_specs=[pl.BlockSpec((tm_e, d), tile),
                      pl.BlockSpec((1, d, 2 * D_FF), wsel), pl.BlockSpec((1, 1, 2 * D_FF), wsel),
                      pl.BlockSpec((1, D_FF, d), wsel), pl.BlockSpec((1, 1, d), wsel)],
            out_specs=pl.BlockSpec((tm_e, d), lambda j, te, nv: (j, 0))),
        out_shape=jax.ShapeDtypeStruct((rows, d), F32),
        compiler_params=_cparams("arbitrary"),
        name="moe_experts",
    )(tile_e, n_valid, xs, lw["w_gu"], lw["b_gu"], lw["w_down"], lw["b_down"])


def _combine_kernel(pos_ref, x_ref, info_ref, ys_ref, ln2g_ref, ln2b_ref, out_ref, buf, sem, *, tm):
    def issue(t, carry):
        for k in range(TOP_K):
            p = pos_ref[t * TOP_K + k]
            pltpu.make_async_copy(ys_ref.at[pl.ds(p, 1)], buf.at[k, pl.ds(t, 1)], sem).start(priority=k % 2)
        return carry

    lax.fori_loop(0, tm, issue, 0)
    for k in range(TOP_K):
        pltpu.make_async_copy(ys_ref.at[pl.ds(0, tm)], buf.at[k], sem).wait()
    rc = min(128, tm)
    lane = lax.broadcasted_iota(jnp.int32, (rc, LANES), 1)

    def rows_pass(c, carry):
        rows = pl.ds(pl.multiple_of(c * rc, rc), rc)
        info = info_ref[rows, :]
        acc = DN_ALPHA * x_ref[rows, :]
        for k in range(TOP_K):
            gate = jnp.sum(jnp.where(lane == INFO_GATE + k, info, 0.0), -1, keepdims=True)
            acc = acc + gate * buf[k, rows, :]
        out_ref[rows, :] = _layer_norm(acc, ln2g_ref[...], ln2b_ref[...])
        return carry

    lax.fori_loop(0, tm // rc, rows_pass, 0)


def _combine(x, info, pos, ys, lw, *, tm=512):
    T, d = x.shape
    tm = min(tm, T)
    kern = functools.partial(_combine_kernel, tm=tm)
    return pl.pallas_call(
        kern,
        grid=(T // tm,),
        in_specs=[pl.BlockSpec((tm * TOP_K,), lambda i: (i,), memory_space=pltpu.SMEM),
                  pl.BlockSpec((tm, d), lambda i: (i, 0)),
                  pl.BlockSpec((tm, LANES), lambda i: (i, 0)),
                  pl.BlockSpec(memory_space=pl.ANY),
                  _full((1, d)), _full((1, d))],
        out_specs=pl.BlockSpec((tm, d), lambda i: (i, 0)),
        out_shape=jax.ShapeDtypeStruct((T, d), F32),
        scratch_shapes=[pltpu.VMEM((TOP_K, tm, d), F32), pltpu.SemaphoreType.DMA],
        compiler_params=_cparams("arbitrary"),
        name="moe_combine",
    )(pos, x, info, ys, lw["ln2_g"], lw["ln2_b"])


def _moe(x, lw):
    T = x.shape[0]
    tm_e = 512 if T >= 4096 else 128
    nt = T * TOP_K // tm_e + N_EXPERTS
    info, cnt = _router(x, lw)
    pos, tile_e, n_valid, gend, gsz = _moe_schedule(info, cnt, tm_e, nt)
    xs = _dispatch(x, pos, gend, gsz, n_valid, tm_e, nt)
    ys = _experts(xs, tile_e, n_valid, lw, tm_e)
    return _combine(x, info, pos, ys, lw)


def _ones_block_diag():
    idx = np.arange(RWKV_DIM) // RWKV_HEAD
    return jnp.asarray((idx[:, None] == idx[None, :]).astype(np.float32), dtype=BF16)


def _deinterleave_matrix():
    n = 2 * D_FF
    src = np.concatenate([np.arange(0, n, 2), np.arange(1, n, 2)])
    return jnp.asarray((np.arange(n)[:, None] == src[None, :]).astype(np.float32), dtype=BF16)


def _block_diag(x):
    C, G, r, c = x.shape
    eye = jnp.eye(G, dtype=x.dtype)
    return jnp.einsum("cgij,gh->cgihj", x, eye).reshape(C, G * r, G * c)


def _prep_layer(P, l):
    f = lambda name: P[name][l]
    lw = {}
    w_in = f("w_in")
    offs = np.cumsum((0,) + IN_SIZES)
    cols = lambda j: w_in[:, offs[j]:offs[j + 1]]
    w_q, w_kv = cols(0), cols(1)
    w_ckv, w_kpe = w_kv[:, :MLA_KV_RANK], w_kv[:, MLA_KV_RANK:]
    half = MLA_ROPE // 2
    zpad = jnp.zeros((D_MODEL, LANES - MLA_ROPE), F32)
    w_kpe_rot = jnp.concatenate([-w_kpe[:, half:], w_kpe[:, :half]], 1)
    lw["w_mla"] = jnp.concatenate([w_q, w_ckv, w_kpe, zpad, w_kpe_rot, zpad], 1).astype(BF16)
    lw["w_rw"] = cols(2).astype(BF16)
    lw["w_su"] = cols(3).astype(BF16)
    lw["w_gates"] = jnp.concatenate([cols(4), cols(5), cols(6)], 1).astype(BF16)
    lw["q_norm"] = f("mla_q_a_norm").reshape(1, -1)
    lw["kv_norm"] = f("mla_kv_a_norm").reshape(1, -1)
    wqb = f("mla_w_q_b").reshape(MLA_Q_RANK, MLA_HEADS, MLA_NOPE + MLA_ROPE)
    nope, x1, x2 = wqb[..., :MLA_NOPE], wqb[..., MLA_NOPE:MLA_NOPE + half], wqb[..., MLA_NOPE + half:]
    z32 = jnp.zeros_like(wqb[..., :HEAD_PAD - MLA_NOPE - MLA_ROPE])
    plain = jnp.concatenate([x1, x2, nope, z32], -1).reshape(MLA_Q_RANK, -1)
    rot = jnp.concatenate([-x2, x1, jnp.zeros_like(nope), z32], -1).reshape(MLA_Q_RANK, -1)
    lw["w_qb"] = jnp.concatenate([plain, rot], 1).astype(BF16)
    wkvb = f("mla_w_kv_b").reshape(MLA_KV_RANK, MLA_HEADS, MLA_NOPE + MLA_V)
    k_nope, v = wkvb[..., :MLA_NOPE], wkvb[..., MLA_NOPE:]
    zk = jnp.zeros_like(k_nope[..., :MLA_ROPE])
    lw["w_k"] = jnp.concatenate([zk, k_nope, zk], -1).reshape(MLA_KV_RANK, -1).astype(BF16)
    lw["w_v"] = v.reshape(MLA_KV_RANK, -1).astype(BF16)
    row = lambda name: f(name).reshape(1, -1)
    lw["mu"] = row("rwkv_mu")
    lw["w0"] = row("rwkv_w0")
    z64 = jnp.zeros((64, RWKV_DIM), F32)
    lw["w_up"] = jnp.concatenate([f("rwkv_w_up"), z64], 0).astype(BF16)
    lw["a_up"] = jnp.concatenate([z64, f("rwkv_a_up")], 0).astype(BF16)
    lw["a0"] = row("rwkv_a0")
    lw["g_up"] = f("rwkv_g_up").astype(BF16)
    lw["k_k"] = row("rwkv_k_k")
    lw["k_a"] = row("rwkv_k_a")
    lw["r_k"] = row("rwkv_r_k")
    lw["lnx_g"] = row("rwkv_lnx_g")
    lw["lnx_b"] = row("rwkv_lnx_b")
    lw["ones_bd"] = _ones_block_diag()
    ab_re, ab_im, bb_re, bb_im = _s5_params(f("ssm_a_re"), f("ssm_a_im"), f("ssm_log_dt"),
                                            f("ssm_b_re"), f("ssm_b_im"))
    gpc = SSM_GROUPS // SSM_CHUNKS
    chunked = lambda t: t.reshape(SSM_CHUNKS, gpc, t.shape[1], t.shape[2])
    lw["s5_bre"] = _block_diag(chunked(jnp.transpose(bb_re, (1, 0, 2)))).astype(BF16)
    lw["s5_bim"] = _block_diag(chunked(jnp.transpose(bb_im, (1, 0, 2)))).astype(BF16)
    lw["s5_cre"] = _block_diag(chunked(jnp.transpose(f("ssm_c_re"), (0, 2, 1)))).astype(BF16)
    lw["s5_cim"] = _block_diag(chunked(jnp.transpose(f("ssm_c_im"), (0, 2, 1)))).astype(BF16)
    lw["s5_d"] = row("ssm_d")
    lw["s5_are"] = ab_re.reshape(1, -1)
    lw["s5_aim"] = ab_im.reshape(1, -1)
    lw["glu_w"] = f("ssm_glu_w").astype(BF16)
    lw["glu_b"] = row("ssm_glu_b")
    for name in ("w_br_a", "w_br_b", "w_br_c", "w_out"):
        lw[name] = f(name).astype(BF16)
    for name in ("ln1_g", "ln1_b", "ln2_g", "ln2_b"):
        lw[name] = row(name)
    rw_ = jnp.pad(f("router_w"), ((0, 0), (0, LANES - N_EXPERTS)))
    hi = rw_.astype(BF16)
    lw["router_hi"] = hi
    lw["router_lo"] = (rw_ - hi.astype(F32)).astype(BF16)
    lw["router_b"] = jnp.pad(f("router_b"), (0, LANES - N_EXPERTS), constant_values=NEG_INF).reshape(1, -1)
    wgu = f("exp_w_gu").reshape(N_EXPERTS * D_MODEL, 2 * D_FF)
    wgu = _matmul(wgu, _deinterleave_matrix(), 1, wgu.shape[0], out_dtype=BF16)
    lw["w_gu"] = wgu.reshape(N_EXPERTS, D_MODEL, 2 * D_FF)
    bgu = f("exp_b_gu")
    lw["b_gu"] = jnp.concatenate([bgu[..., 0::2], bgu[..., 1::2]], -1).reshape(N_EXPERTS, 1, 2 * D_FF)
    lw["w_down"] = f("exp_w_down")
    lw["b_down"] = f("exp_b_down").reshape(N_EXPERTS, 1, D_MODEL)
    return lw


def _rope_tables(pos):
    half = MLA_ROPE // 2
    inv = ROPE_THETA ** (-jnp.arange(half, dtype=F32) / half)
    ang = pos.astype(F32)[:, None] * inv
    cos, sin = jnp.cos(ang), jnp.sin(ang)
    n = pos.shape[0]
    tc = jnp.concatenate([cos, cos, jnp.ones((n, MLA_NOPE), F32),
                          jnp.zeros((n, HEAD_PAD - MLA_NOPE - MLA_ROPE), F32)], 1)
    ts = jnp.concatenate([sin, sin, jnp.zeros((n, HEAD_PAD - MLA_ROPE), F32)], 1)
    return tc, ts


def _to_lanes(t, nb, s, lanes):
    t = t.reshape(s, nb * RWKV_HEADS, RWKV_HEAD).transpose(0, 2, 1)
    return jnp.pad(t, ((0, 0), (0, 0), (0, lanes - nb * RWKV_HEADS)))


def _layer(x, lw, past, nb, s, tables, att_tk=LANES):
    T = nb * s
    tc, ts = tables
    if past is None:
        start = 0
        shift_p = jnp.zeros((nb, RWKV_PROJ), F32)
        wkv_p = jnp.zeros((nb, RWKV_HEADS, RWKV_HEAD, RWKV_HEAD), F32)
        sre_p = jnp.zeros((nb, SSM_CH), F32)
        sim_p = jnp.zeros((nb, SSM_CH), F32)
    else:
        ckv_p, kpe_p, shift_p, wkv_p, sre_p, sim_p = past
        start = ckv_p.shape[1]
        sre_p = sre_p.reshape(nb, SSM_CH)
        sim_p = sim_p.reshape(nb, SSM_CH)

    q, ckv, kpe128 = _mla_in(x, lw, tc, ts)
    if past is None:
        sk = s
        ckv_all, kpe_all = ckv, kpe128
    else:
        sk = start + s
        ckv_all = jnp.concatenate([ckv_p, ckv.reshape(nb, s, -1)], 1)
        kpe_new = kpe128.reshape(nb, s, LANES)
        kpe_all = jnp.concatenate([jnp.pad(kpe_p, ((0, 0), (0, 0), (0, LANES - MLA_ROPE))), kpe_new], 1)
    skp = -(-sk // att_tk) * att_tk
    if skp != sk:
        ckv_all = jnp.pad(ckv_all.reshape(nb, sk, -1), ((0, 0), (0, skp - sk), (0, 0)))
        kpe_all = jnp.pad(kpe_all.reshape(nb, sk, -1), ((0, 0), (0, skp - sk), (0, 0)))
    k_pad, v_all = _kv_expand(ckv_all.reshape(nb * skp, -1), kpe_all.reshape(nb * skp, -1), lw)
    ya = _attention(q, k_pad, v_all, nb, s, skp, sk, start)

    r, wdec, kn, v, kk, kb, g, bonus, last_row = _rwkv_pre(x, shift_p, lw, nb, s)
    lanes = -(-nb * RWKV_HEADS // LANES) * LANES
    seqs = [_to_lanes(t, nb, s, lanes) for t in (r, wdec, kn, v, kk, kb)]
    s0 = jnp.transpose(wkv_p, (3, 2, 0, 1)).reshape(RWKV_HEAD, RWKV_HEAD, nb * RWKV_HEADS)
    s0 = jnp.pad(s0, ((0, 0), (0, 0), (0, lanes - nb * RWKV_HEADS)))
    o_l, sT = _rwkv_scan(seqs, s0, s)
    o_tm = o_l[:, :, :nb * RWKV_HEADS].transpose(0, 2, 1).reshape(s, nb * RWKV_DIM)
    wkv_n = (sT[:, :, :nb * RWKV_HEADS].reshape(RWKV_HEAD, RWKV_HEAD, nb, RWKV_HEADS).transpose(2, 3, 1, 0))
    shift_n = last_row.reshape(nb, RWKV_PROJ)

    su_tm = _matmul(x, lw["w_su"], nb, s, time_major_out=True).reshape(s * nb, SSM_DIM)
    yc_tm, sre_n, sim_n = _s5(su_tm, sre_p, sim_p, lw, nb, s)
    yc_tm = yc_tm.reshape(s, nb * SSM_DIM)

    x1 = _merge(x, ya, o_tm, bonus, g, yc_tm, lw, nb, s)
    x2 = _moe(x1, lw)
    new = (ckv.reshape(nb, s, MLA_KV_RANK), kpe128[:, :MLA_ROPE].reshape(nb, s, MLA_ROPE), shift_n, wkv_n,
           sre_n.reshape(nb, SSM_GROUPS, SSM_STATE), sim_n.reshape(nb, SSM_GROUPS, SSM_STATE))
    return x2, new


def _trunk(x3, weights, caches):
    nb, s, d = x3.shape
    start = 0 if caches is None else caches[0].shape[2]
    tc, ts = _rope_tables(start + jnp.arange(s))
    tables = (jnp.tile(tc, (nb, 1)), jnp.tile(ts, (nb, 1)))
    x = x3.reshape(nb * s, d)
    new = []
    for l in range(len(weights)):
        past = None if caches is None else tuple(c[l] for c in caches)
        x, st = _layer(x, weights[l], past, nb, s, tables)
        new.append(st)
    return (x.reshape(nb, s, d),) + tuple(jnp.stack([st[j] for st in new]) for j in range(6))


def kernel(x_prompt, x_sample, cache_mla_ckv, cache_mla_kpe, state_rwkv_shift, state_rwkv_wkv, state_ssm_re, state_ssm_im, w_in, mla_q_a_norm, mla_w_q_b, mla_kv_a_norm, mla_w_kv_b, rwkv_mu, rwkv_w0, rwkv_w_up, rwkv_a0, rwkv_a_up, rwkv_g_up, rwkv_k_k, rwkv_k_a, rwkv_r_k, rwkv_lnx_g, rwkv_lnx_b, ssm_a_re, ssm_a_im, ssm_b_re, ssm_b_im, ssm_c_re, ssm_c_im, ssm_d, ssm_log_dt, ssm_glu_w, ssm_glu_b, w_br_a, w_br_b, w_br_c, w_out, ln1_g, ln1_b, router_w, router_b, exp_w_gu, exp_b_gu, exp_w_down, exp_b_down, ln2_g, ln2_b):
    P = dict(w_in=w_in, mla_q_a_norm=mla_q_a_norm, mla_w_q_b=mla_w_q_b, mla_kv_a_norm=mla_kv_a_norm,
             mla_w_kv_b=mla_w_kv_b, rwkv_mu=rwkv_mu, rwkv_w0=rwkv_w0, rwkv_w_up=rwkv_w_up, rwkv_a0=rwkv_a0,
             rwkv_a_up=rwkv_a_up, rwkv_g_up=rwkv_g_up, rwkv_k_k=rwkv_k_k, rwkv_k_a=rwkv_k_a, rwkv_r_k=rwkv_r_k,
             rwkv_lnx_g=rwkv_lnx_g, rwkv_lnx_b=rwkv_lnx_b, ssm_a_re=ssm_a_re, ssm_a_im=ssm_a_im,
             ssm_b_re=ssm_b_re, ssm_b_im=ssm_b_im, ssm_c_re=ssm_c_re, ssm_c_im=ssm_c_im, ssm_d=ssm_d,
             ssm_log_dt=ssm_log_dt, ssm_glu_w=ssm_glu_w, ssm_glu_b=ssm_glu_b, w_br_a=w_br_a, w_br_b=w_br_b,
             w_br_c=w_br_c, w_out=w_out, ln1_g=ln1_g, ln1_b=ln1_b, router_w=router_w, router_b=router_b,
             exp_w_gu=exp_w_gu, exp_b_gu=exp_b_gu, exp_w_down=exp_w_down, exp_b_down=exp_b_down,
             ln2_g=ln2_g, ln2_b=ln2_b)
    depth = w_in.shape[0]
    weights = [_prep_layer(P, l) for l in range(depth)]
    outs_p = _trunk(x_prompt, weights, None)
    caches = (cache_mla_ckv, cache_mla_kpe, state_rwkv_shift, state_rwkv_wkv, state_ssm_re, state_ssm_im)
    outs_s = _trunk(x_sample, weights, caches)
    return (outs_p[0], outs_s[0]) + outs_p[1:] + outs_s[1:]
```

```python
import functools
import math

import numpy as np
import jax
import jax.numpy as jnp
from jax import lax
from jax.experimental import pallas as pl
from jax.experimental.pallas import tpu as pltpu

F32 = jnp.float32
BF16 = jnp.bfloat16

D_MODEL = 1024
CHUNK = 64
MLA_HEADS = 8
MLA_NOPE = 64
MLA_ROPE = 32
MLA_V = 64
MLA_Q_RANK = 384
MLA_KV_RANK = 256
ROPE_THETA = 10000.0
HEAD_PAD = 128
RWKV_HEADS = 8
RWKV_HEAD = 64
RWKV_DIM = RWKV_HEADS * RWKV_HEAD
RWKV_PROJ = 3 * RWKV_DIM + 64 + 64 + 128
RWKV_LN_EPS = 64e-5
SSM_DIM = 512
SSM_GROUP = 16
SSM_GROUPS = 32
SSM_STATE = 64
SSM_CHUNKS = 4
SSM_CH = SSM_GROUPS * SSM_STATE
IN_SIZES = (MLA_Q_RANK, MLA_KV_RANK + MLA_ROPE, RWKV_PROJ, SSM_DIM, D_MODEL, D_MODEL, D_MODEL)
N_EXPERTS = 32
TOP_K = 4
D_FF = 512
SWIGLU_LIMIT = 7.0
SWIGLU_ALPHA = 1.702
DEPTH = 2
DN_ALPHA = (2 * DEPTH) ** 0.25
LN_EPS = 1e-5
RMS_EPS = 1e-6
NEG_INF = -1e30
ATT_SCALE = (MLA_NOPE + MLA_ROPE) ** -0.5
LANES = 128
VMEM_LIMIT = 48 * 1024 * 1024


def _cparams(*sem):
    return pltpu.CompilerParams(dimension_semantics=sem, vmem_limit_bytes=VMEM_LIMIT)


def _dot(a, b):
    return jnp.dot(a, b, preferred_element_type=F32)


def _sigmoid(x):
    return 1.0 / (1.0 + jnp.exp(-x))


def _layer_norm(x, g, b):
    mu = jnp.mean(x, -1, keepdims=True)
    xc = x - mu
    var = jnp.mean(xc * xc, -1, keepdims=True)
    return xc * lax.rsqrt(var + LN_EPS) * g + b


def _full(shape):
    n = len(shape)
    return pl.BlockSpec(shape, lambda *_: (0,) * n)


def _mm_kernel(x_ref, w_ref, o_ref):
    o_ref[...] = _dot(x_ref[...].astype(BF16), w_ref[...]).astype(o_ref.dtype)


def _matmul(x, w, nb, s, *, time_major_out=False, tm=512, tn=1024, out_dtype=F32, row0=0):
    K = x.shape[1]
    N = w.shape[1]
    tm = min(tm, s)
    tn = min(tn, N)
    while N % tn:
        tn -= LANES
    nt = s // tm
    blk0 = row0 // tm
    if time_major_out:
        assert tn == N
        out_shape = jax.ShapeDtypeStruct((s, nb * N), out_dtype)
        out_spec = pl.BlockSpec((tm, N), lambda b, i, j: (i, b))
    else:
        out_shape = jax.ShapeDtypeStruct((nb * s, N), out_dtype)
        out_spec = pl.BlockSpec((tm, tn), lambda b, i, j: (b * nt + i, j))
    return pl.pallas_call(
        _mm_kernel,
        grid=(nb, nt, N // tn),
        in_specs=[pl.BlockSpec((tm, K), lambda b, i, j: (blk0 + b * nt + i, 0)),
                  pl.BlockSpec((K, tn), lambda b, i, j: (0, j))],
        out_specs=out_spec,
        out_shape=out_shape,
        compiler_params=_cparams("parallel", "parallel", "arbitrary"),
        name="matmul",
    )(x, w)


def _mla_in_kernel(x_ref, w_ref, qg_ref, kvg_ref, wqb_ref, tc_ref, ts_ref, q_ref, ckv_ref, kpe_ref):
    h = _dot(x_ref[...].astype(BF16), w_ref[...])
    tc = tc_ref[...]
    ts = ts_ref[...]
    qa = h[:, :MLA_Q_RANK]
    qn = qa * lax.rsqrt(jnp.mean(qa * qa, -1, keepdims=True) + RMS_EPS) * qg_ref[...]
    q2 = _dot(qn.astype(BF16), wqb_ref[...])
    hp = MLA_HEADS * HEAD_PAD
    for hd in range(MLA_HEADS):
        lo = hd * HEAD_PAD
        q0 = q2[:, lo:lo + HEAD_PAD]
        q1 = q2[:, hp + lo:hp + lo + HEAD_PAD]
        q_ref[:, lo:lo + HEAD_PAD] = ((q0 * tc + q1 * ts) * ATT_SCALE).astype(BF16)
    c0 = MLA_Q_RANK
    ckv = h[:, c0:c0 + MLA_KV_RANK]
    ckv_ref[...] = ckv * lax.rsqrt(jnp.mean(ckv * ckv, -1, keepdims=True) + RMS_EPS) * kvg_ref[...]
    c1 = c0 + MLA_KV_RANK
    kpe_ref[...] = h[:, c1:c1 + LANES] * tc + h[:, c1 + LANES:c1 + 2 * LANES] * ts


def _mla_in(x, lw, tc, ts, *, tm=512):
    T = x.shape[0]
    tm = min(tm, T)
    hp = MLA_HEADS * HEAD_PAD
    nw = lw["w_mla"].shape[1]
    return pl.pallas_call(
        _mla_in_kernel,
        grid=(T // tm,),
        in_specs=[pl.BlockSpec((tm, D_MODEL), lambda i: (i, 0)),
                  _full((D_MODEL, nw)), _full((1, MLA_Q_RANK)), _full((1, MLA_KV_RANK)),
                  _full((MLA_Q_RANK, 2 * hp)),
                  pl.BlockSpec((tm, LANES), lambda i: (i, 0)),
                  pl.BlockSpec((tm, LANES), lambda i: (i, 0))],
        out_specs=[pl.BlockSpec((tm, hp), lambda i: (i, 0)),
                   pl.BlockSpec((tm, MLA_KV_RANK), lambda i: (i, 0)),
                   pl.BlockSpec((tm, LANES), lambda i: (i, 0))],
        out_shape=[jax.ShapeDtypeStruct((T, hp), BF16),
                   jax.ShapeDtypeStruct((T, MLA_KV_RANK), F32),
                   jax.ShapeDtypeStruct((T, LANES), F32)],
        compiler_params=_cparams("parallel"),
        name="mla_in",
    )(x, lw["w_mla"], lw["q_norm"], lw["kv_norm"], lw["w_qb"], tc, ts)


def _kv_expand_kernel(ckv_ref, kpe_ref, wk_ref, wv_ref, k_ref, v_ref):
    c = ckv_ref[...].astype(BF16)
    k = _dot(c, wk_ref[...])
    kpe = kpe_ref[...]
    for hd in range(MLA_HEADS):
        lo = hd * HEAD_PAD
        k_ref[:, lo:lo + HEAD_PAD] = (k[:, lo:lo + HEAD_PAD] + kpe).astype(BF16)
    v_ref[...] = _dot(c, wv_ref[...]).astype(BF16)


def _kv_expand(ckv, kpe, lw, *, tm=512):
    T = ckv.shape[0]
    tm = min(tm, T)
    while T % tm:
        tm //= 2
    hp = MLA_HEADS * HEAD_PAD
    hv = MLA_HEADS * MLA_V
    return pl.pallas_call(
        _kv_expand_kernel,
        grid=(T // tm,),
        in_specs=[pl.BlockSpec((tm, MLA_KV_RANK), lambda i: (i, 0)),
                  pl.BlockSpec((tm, LANES), lambda i: (i, 0)),
                  _full((MLA_KV_RANK, hp)), _full((MLA_KV_RANK, hv))],
        out_specs=[pl.BlockSpec((tm, hp), lambda i: (i, 0)),
                   pl.BlockSpec((tm, hv), lambda i: (i, 0))],
        out_shape=[jax.ShapeDtypeStruct((T, hp), BF16), jax.ShapeDtypeStruct((T, hv), BF16)],
        compiler_params=_cparams("parallel"),
        name="kv_expand",
    )(ckv, kpe, lw["w_k"], lw["w_v"])


def _attn_kernel(q_ref, k_ref, v_ref, o_ref, *, tq, nq, q_start, sk):
    lane = lax.broadcasted_iota(jnp.int32, (tq, LANES), 1)
    nt = (((1,), (1,)), ((), ()))
    for i in range(nq):
        q_lo = q_start + i * tq
        k_end = min(((q_lo + tq - 1) // CHUNK + 1) * CHUNK, sk)
        kw = -(-k_end // LANES) * LANES
        c0 = min((q_lo // CHUNK + 1) * CHUNK, sk) // LANES * LANES
        tail = kw - c0
        if tail:
            q_chunk = (q_lo + lax.broadcasted_iota(jnp.int32, (tq, tail), 0)) // CHUNK
            k_pos = c0 + lax.broadcasted_iota(jnp.int32, (tq, tail), 1)
            visible = k_pos // CHUNK <= q_chunk
            if kw > sk:
                visible = visible & (k_pos < sk)
        outs = []
        for hh in range(2):
            hs = slice(hh * HEAD_PAD, (hh + 1) * HEAD_PAD)
            q = q_ref[i * tq:(i + 1) * tq, hs]
            parts = []
            if c0:
                parts.append(lax.dot_general(q, k_ref[0:c0, hs], nt, preferred_element_type=F32))
            if tail:
                s_t = lax.dot_general(q, k_ref[c0:kw, hs], nt, preferred_element_type=F32)
                parts.append(jnp.where(visible, s_t, NEG_INF))
            m = jnp.max(parts[0], -1, keepdims=True)
            for s_ in parts[1:]:
                m = jnp.maximum(m, jnp.max(s_, -1, keepdims=True))
            l = None
            acc = None
            lo = 0
            for s_ in parts:
                p = jnp.exp(s_ - m)
                ps = jnp.sum(p, -1, keepdims=True)
                pv = _dot(p.astype(BF16), v_ref[lo:lo + s_.shape[1], :])
                l = ps if l is None else l + ps
                acc = pv if acc is None else acc + pv
                lo += s_.shape[1]
            outs.append(acc / l)
        o_ref[i * tq:(i + 1) * tq, :] = jnp.where(lane < MLA_V, outs[0], outs[1])


def _attention(q, k, v, nb, sq, skp, sk, q_start, *, tq=512):
    tq = min(tq, sq)
    kern = functools.partial(_attn_kernel, tq=tq, nq=sq // tq, q_start=q_start, sk=sk)
    return pl.pallas_call(
        kern,
        grid=(nb, MLA_HEADS // 2),
        in_specs=[pl.BlockSpec((sq, 2 * HEAD_PAD), lambda b, j: (b, j)),
                  pl.BlockSpec((skp, 2 * HEAD_PAD), lambda b, j: (b, j)),
                  pl.BlockSpec((skp, 2 * MLA_V), lambda b, j: (b, j))],
        out_specs=pl.BlockSpec((sq, 2 * MLA_V), lambda b, j: (b, j)),
        out_shape=jax.ShapeDtypeStruct((nb * sq, MLA_HEADS * MLA_V), F32),
        compiler_params=_cparams("parallel", "parallel"),
        name="attention",
    )(q, k, v)


def _head_sum(z, ones_bd):
    hi = z.astype(BF16)
    lo = (z - hi.astype(F32)).astype(BF16)
    return _dot(hi, ones_bd) + _dot(lo, ones_bd)


def _rwkv_pre_kernel(x_ref, xprev_ref, shift_ref, wrw_ref, mu_ref, w0_ref, wup_ref, a0_ref, aup_ref, gup_ref,
                     kk_ref, ka_ref, rk_ref, ones_ref,
                     r_out, w_out, k_out, v_out, kk_out, b_out, g_out, bonus_out, last_out):
    p = _dot(x_ref[...].astype(BF16), wrw_ref[...])
    first = pl.program_id(1) == 0
    p_before = _dot(xprev_ref[...].astype(BF16), wrw_ref[...])
    prev_row = jnp.where(first, shift_ref[0], p_before[7:8, :])
    last_out[0] = p[p.shape[0] - 1:, :]
    row = lax.broadcasted_iota(jnp.int32, p.shape, 0)
    prev = jnp.where(row == 0, prev_row, pltpu.roll(p, 1, 0))
    ps = p + (prev - p) * mu_ref[...]
    d = RWKV_DIM
    r = ps[:, 0:d]
    k = ps[:, d:2 * d]
    v = ps[:, 2 * d:3 * d]
    wa = ps[:, 3 * d:3 * d + LANES]
    gd = ps[:, 3 * d + LANES:3 * d + 2 * LANES]
    ones_bd = ones_ref[...]
    wlin = w0_ref[...] + _dot(jnp.tanh(wa).astype(BF16), wup_ref[...])
    z = -wlin
    w = -(jnp.maximum(z, 0.0) + jnp.log1p(jnp.exp(-jnp.abs(z)))) - 0.5
    w_out[...] = jnp.exp(-jnp.exp(w))
    a = _sigmoid(a0_ref[...] + _dot(wa.astype(BF16), aup_ref[...]))
    g_out[...] = _dot(_sigmoid(gd).astype(BF16), gup_ref[...])
    kk = k * kk_ref[...]
    kkn = kk * lax.rsqrt(_head_sum(kk * kk, ones_bd) + 1e-12)
    kk_out[...] = kkn
    b_out[...] = kkn * a
    kn = k * (1.0 + (a - 1.0) * ka_ref[...])
    r_out[...] = r
    k_out[...] = kn
    v_out[...] = v
    bonus_out[...] = _head_sum(r * kn * rk_ref[...], ones_bd) * v


def _rwkv_pre(x, shift_prev, lw, nb, s, *, tm=256):
    tm = min(tm, s)
    tps = s // tm
    d = RWKV_DIM
    vec = lambda n: _full((1, n))
    bm = pl.BlockSpec((tm, d), lambda b, i: (b * tps + i, 0))
    tmaj = pl.BlockSpec((tm, d), lambda b, i: (i, b))
    per_seq = pl.BlockSpec((1, 1, RWKV_PROJ), lambda b, i: (b, 0, 0))
    bm_shape = jax.ShapeDtypeStruct((nb * s, d), F32)
    tm_shape = jax.ShapeDtypeStruct((s, nb * d), F32)
    return pl.pallas_call(
        _rwkv_pre_kernel,
        grid=(nb, tps),
        in_specs=[pl.BlockSpec((tm, D_MODEL), lambda b, i: (b * tps + i, 0)),
                  pl.BlockSpec((8, D_MODEL), lambda b, i: (jnp.maximum((b * tps + i) * (tm // 8) - 1, 0), 0)),
                  per_seq, _full((D_MODEL, RWKV_PROJ)),
                  vec(RWKV_PROJ), vec(d), _full((LANES, d)), vec(d), _full((LANES, d)),
                  _full((LANES, d)), vec(d), vec(d), vec(d), _full((d, d))],
        out_specs=[tmaj] * 6 + [bm] * 2 + [per_seq],
        out_shape=[tm_shape] * 6 + [bm_shape] * 2 + [jax.ShapeDtypeStruct((nb, 1, RWKV_PROJ), F32)],
        compiler_params=_cparams("parallel", "arbitrary"),
        name="rwkv_pre",
    )(x, x, shift_prev.reshape(nb, 1, RWKV_PROJ), lw["w_rw"], lw["mu"], lw["w0"], lw["w_up"], lw["a0"],
      lw["a_up"], lw["g_up"], lw["k_k"], lw["k_a"], lw["r_k"], lw["ones_bd"])


RWKV_VC = 32


def _rwkv_scan_kernel(r_ref, w_ref, k_ref, v_ref, kk_ref, b_ref, s0_ref, o_ref, sT_ref, st_ref, *, tt):
    n = RWKV_HEAD

    @pl.when(pl.program_id(0) == 0)
    def _():
        st_ref[...] = s0_ref[...]

    def step(t, carry):
        for c in range(n // RWKV_VC):
            vs = slice(c * RWKV_VC, (c + 1) * RWKV_VC)
            parts = [None] * 4
            for q in range(n):
                term = st_ref[q, vs, :] * kk_ref[t, q:q + 1, :]
                parts[q % 4] = term if parts[q % 4] is None else parts[q % 4] + term
            sa = -((parts[0] + parts[1]) + (parts[2] + parts[3]))
            vt = v_ref[t, vs, :]
            outs = [None] * 4
            for q in range(n):
                s_new = (st_ref[q, vs, :] * w_ref[t, q:q + 1, :] + sa * b_ref[t, q:q + 1, :]
                         + vt * k_ref[t, q:q + 1, :])
                st_ref[q, vs, :] = s_new
                term = s_new * r_ref[t, q:q + 1, :]
                outs[q % 4] = term if outs[q % 4] is None else outs[q % 4] + term
            o_ref[t, vs, :] = (outs[0] + outs[1]) + (outs[2] + outs[3])
        return carry

    lax.fori_loop(0, tt, step, 0)

    @pl.when(pl.program_id(0) == pl.num_programs(0) - 1)
    def _():
        sT_ref[...] = st_ref[...]


def _rwkv_scan(seqs, s0, s, *, tt=16):
    L = s0.shape[-1]
    n = RWKV_HEAD
    tt = min(tt, s)
    blk = pl.BlockSpec((tt, n, L), lambda i: (i, 0, 0))
    kern = functools.partial(_rwkv_scan_kernel, tt=tt)
    return pl.pallas_call(
        kern,
        grid=(s // tt,),
        in_specs=[blk] * 6 + [_full((n, n, L))],
        out_specs=[blk, _full((n, n, L))],
        out_shape=[jax.ShapeDtypeStruct((s, n, L), F32), jax.ShapeDtypeStruct((n, n, L), F32)],
        scratch_shapes=[pltpu.VMEM((n, n, L), F32)],
        compiler_params=_cparams("arbitrary"),
        name="rwkv_scan",
    )(*seqs, s0)


def _s5_params_kernel(are_ref, aim_ref, ldt_ref, bre_ref, bim_ref, abre_ref, abim_ref, bbre_ref, bbim_ref):
    lr = are_ref[...]
    li = aim_ref[...]
    dt = jnp.exp(ldt_ref[...])
    mag = jnp.exp(lr * dt)
    ab_re = mag * jnp.cos(li * dt)
    ab_im = mag * jnp.sin(li * dt)
    den = lr * lr + li * li
    f_re = ((ab_re - 1.0) * lr + ab_im * li) / den
    f_im = (ab_im * lr - (ab_re - 1.0) * li) / den
    abre_ref[...] = ab_re
    abim_ref[...] = ab_im
    for i in range(SSM_GROUP):
        br = bre_ref[i]
        bi = bim_ref[i]
        bbre_ref[i] = f_re * br - f_im * bi
        bbim_ref[i] = f_re * bi + f_im * br


def _s5_params(a_re, a_im, log_dt, b_re, b_im):
    g, n = a_re.shape
    gn = jax.ShapeDtypeStruct((g, n), F32)
    ign = jax.ShapeDtypeStruct((SSM_GROUP, g, n), F32)
    return pl.pallas_call(
        _s5_params_kernel,
        out_shape=[gn, gn, ign, ign],
        name="s5_params",
    )(a_re, a_im, log_dt.reshape(g, 1), jnp.transpose(b_re, (2, 0, 1)), jnp.transpose(b_im, (2, 0, 1)))


def _gelu_tanh(x):
    return 0.5 * x * (1.0 + jnp.tanh(math.sqrt(2.0 / math.pi) * (x + 0.044715 * (x * x * x))))


def _s5_kernel(u_ref, bre_ref, bim_ref, cre_ref, cim_ref, d_ref, are_ref, aim_ref, h0re_ref, h0im_ref,
               gw_ref, gb_ref, y_ref, hTre_ref, hTim_ref, hre_s, him_s, sre_s, sim_s, yy_s, *, tt, nb):
    cw = SSM_CH // SSM_CHUNKS

    @pl.when(pl.program_id(0) == 0)
    def _():
        hre_s[...] = h0re_ref[...]
        him_s[...] = h0im_ref[...]

    for c in range(SSM_CHUNKS):
        uc = u_ref[:, c * LANES:(c + 1) * LANES]
        ub = uc.astype(BF16)
        sre_s[...] = _dot(ub, bre_ref[c])
        sim_s[...] = _dot(ub, bim_ref[c])
        a_re = jnp.broadcast_to(are_ref[:, c * cw:(c + 1) * cw], (nb, cw))
        a_im = jnp.broadcast_to(aim_ref[:, c * cw:(c + 1) * cw], (nb, cw))

        def step(t, carry, a_re=a_re, a_im=a_im):
            hr, hi = carry
            rows = pl.ds(pl.multiple_of(t * nb, nb), nb)
            nr = a_re * hr - a_im * hi + sre_s[rows, :]
            ni = a_re * hi + a_im * hr + sim_s[rows, :]
            sre_s[rows, :] = nr
            sim_s[rows, :] = ni
            return nr, ni

        hr, hi = lax.fori_loop(0, tt, step, (hre_s[:, c * cw:(c + 1) * cw], him_s[:, c * cw:(c + 1) * cw]))
        hre_s[:, c * cw:(c + 1) * cw] = hr
        him_s[:, c * cw:(c + 1) * cw] = hi
        yc = _dot(sre_s[...].astype(BF16), cre_ref[c]) - _dot(sim_s[...].astype(BF16), cim_ref[c])
        yy_s[:, c * LANES:(c + 1) * LANES] = yc + d_ref[:, c * LANES:(c + 1) * LANES] * uc

    y = _gelu_tanh(yy_s[...])
    z = _dot(y.astype(BF16), gw_ref[...]) + gb_ref[...]
    y_ref[...] = z[:, :SSM_DIM] * _sigmoid(z[:, SSM_DIM:])

    @pl.when(pl.program_id(0) == pl.num_programs(0) - 1)
    def _():
        hTre_ref[...] = hre_s[...]
        hTim_ref[...] = him_s[...]


def _s5(u_tm, h0_re, h0_im, lw, nb, s, *, rows=512):
    tt = max(min(rows // nb, s), 1)
    R = tt * nb
    cw = SSM_CH // SSM_CHUNKS
    kern = functools.partial(_s5_kernel, tt=tt, nb=nb)
    return pl.pallas_call(
        kern,
        grid=(s // tt,),
        in_specs=[pl.BlockSpec((R, SSM_DIM), lambda i: (i, 0)),
                  _full((SSM_CHUNKS, LANES, cw)), _full((SSM_CHUNKS, LANES, cw)),
                  _full((SSM_CHUNKS, cw, LANES)), _full((SSM_CHUNKS, cw, LANES)),
                  _full((1, SSM_DIM)), _full((1, SSM_CH)), _full((1, SSM_CH)),
                  _full((nb, SSM_CH)), _full((nb, SSM_CH)),
                  _full((SSM_DIM, 2 * SSM_DIM)), _full((1, 2 * SSM_DIM))],
        out_specs=[pl.BlockSpec((R, SSM_DIM), lambda i: (i, 0)), _full((nb, SSM_CH)), _full((nb, SSM_CH))],
        out_shape=[jax.ShapeDtypeStruct((s * nb, SSM_DIM), F32),
                   jax.ShapeDtypeStruct((nb, SSM_CH), F32), jax.ShapeDtypeStruct((nb, SSM_CH), F32)],
        scratch_shapes=[pltpu.VMEM((nb, SSM_CH), F32), pltpu.VMEM((nb, SSM_CH), F32),
                        pltpu.VMEM((R, cw), F32), pltpu.VMEM((R, cw), F32), pltpu.VMEM((R, SSM_DIM), F32)],
        compiler_params=_cparams("arbitrary"),
        name="s5",
    )(u_tm, lw["s5_bre"], lw["s5_bim"], lw["s5_cre"], lw["s5_cim"], lw["s5_d"], lw["s5_are"], lw["s5_aim"],
      h0_re, h0_im, lw["glu_w"], lw["glu_b"])


def _merge_kernel(x_ref, ya_ref, o_ref, bonus_ref, g_ref, yc_ref, wg_ref, lng_ref, lnb_ref, ones_ref,
                  wa_ref, wb_ref, wc_ref, wo_ref, ln1g_ref, ln1b_ref, out_ref):
    ones_bd = ones_ref[...]
    o = o_ref[...]
    inv_n = 1.0 / RWKV_HEAD
    mean = _head_sum(o, ones_bd) * inv_n
    oc = o - mean
    var = _head_sum(oc * oc, ones_bd) * inv_n
    yb = (oc * lax.rsqrt(var + RWKV_LN_EPS) * lng_ref[...] + lnb_ref[...] + bonus_ref[...]) * g_ref[...]
    d = D_MODEL
    x = x_ref[...]
    xb = x.astype(BF16)
    merged = None
    for j, (y, w_ref) in enumerate(((ya_ref[...], wa_ref), (yb, wb_ref), (yc_ref[...], wc_ref))):
        gate = _sigmoid(_dot(xb, wg_ref[:, j * d:(j + 1) * d]))
        term = gate * _dot(y.astype(BF16), w_ref[...])
        merged = term if merged is None else merged + term
    y = DN_ALPHA * x + _dot(merged.astype(BF16), wo_ref[...])
    out_ref[...] = _layer_norm(y, ln1g_ref[...], ln1b_ref[...])


def _merge(x, ya, o_tm, bonus, g, yc_tm, lw, nb, s, *, tm=256):
    tm = min(tm, s)
    nt = s // tm
    d = D_MODEL
    h = RWKV_DIM
    row = lambda w: pl.BlockSpec((tm, w), lambda b, i: (b * nt + i, 0))
    tmaj = lambda w: pl.BlockSpec((tm, w), lambda b, i: (i, b))
    vec = lambda n: _full((1, n))
    return pl.pallas_call(
        _merge_kernel,
        grid=(nb, nt),
        in_specs=[row(d), row(h), tmaj(h), row(h), row(h), tmaj(SSM_DIM),
                  _full((d, 3 * d)), vec(h), vec(h), _full((h, h)),
                  _full((h, d)), _full((h, d)), _full((h, d)), _full((d, d)), vec(d), vec(d)],
        out_specs=row(d),
        out_shape=jax.ShapeDtypeStruct((nb * s, d), F32),
        compiler_params=_cparams("parallel", "parallel"),
        name="merge",
    )(x, ya, o_tm, bonus, g, yc_tm, lw["w_gates"], lw["lnx_g"], lw["lnx_b"], lw["ones_bd"],
      lw["w_br_a"], lw["w_br_b"], lw["w_br_c"], lw["w_out"], lw["ln1_g"], lw["ln1_b"])


INFO_GATE, INFO_EID, INFO_RANK = 0, 4, 8


def _router_kernel(x_ref, whi_ref, wlo_ref, b_ref, tri_ref, info_ref, cnt_ref, base_s):
    @pl.when(pl.program_id(0) == 0)
    def _():
        base_s[...] = jnp.zeros_like(base_s)

    x = x_ref[...]
    xh = x.astype(BF16)
    xl = (x - xh.astype(F32)).astype(BF16)
    logits = _dot(xh, whi_ref[...]) + _dot(xl, whi_ref[...]) + _dot(xh, wlo_ref[...]) + b_ref[...]
    lane_i = lax.broadcasted_iota(jnp.int32, logits.shape, 1)
    lane = lane_i.astype(F32)
    vals, sels, ids = [], [], []
    for _ in range(TOP_K):
        m = jnp.max(logits, -1, keepdims=True)
        idx = jnp.min(jnp.where(logits == m, lane, float(LANES)), -1, keepdims=True)
        sel = lane == idx
        vals.append(m)
        sels.append(sel)
        ids.append(idx)
        logits = jnp.where(sel, -3e38, logits)
    es = [jnp.exp(v - vals[0]) for v in vals]
    den = es[0] + es[1] + es[2] + es[3]
    chosen = jnp.zeros_like(logits)
    for sel in sels:
        chosen = chosen + jnp.where(sel, 1.0, 0.0)
    before = _dot(tri_ref[...], chosen.astype(BF16)) + base_s[...]
    info = jnp.zeros_like(logits)
    for k in range(TOP_K):
        rank = jnp.sum(jnp.where(sels[k], before, 0.0), -1, keepdims=True)
        info = (info + jnp.where(lane_i == INFO_GATE + k, es[k] / den, 0.0)
                + jnp.where(lane_i == INFO_EID + k, ids[k], 0.0)
                + jnp.where(lane_i == INFO_RANK + k, rank, 0.0))
    info_ref[...] = info
    base_s[...] += jnp.sum(chosen, 0, keepdims=True)
    cnt_ref[...] = base_s[...]


def _router(x, lw, *, tm=512):
    T = x.shape[0]
    tm = min(tm, T)
    tri = jnp.asarray(np.tril(np.ones((tm, tm), np.float32), -1), dtype=BF16)
    return pl.pallas_call(
        _router_kernel,
        grid=(T // tm,),
        in_specs=[pl.BlockSpec((tm, D_MODEL), lambda i: (i, 0)),
                  _full((D_MODEL, LANES)), _full((D_MODEL, LANES)), _full((1, LANES)), _full((tm, tm))],
        out_specs=[pl.BlockSpec((tm, LANES), lambda i: (i, 0)), _full((1, LANES))],
        out_shape=[jax.ShapeDtypeStruct((T, LANES), F32), jax.ShapeDtypeStruct((1, LANES), F32)],
        scratch_shapes=[pltpu.VMEM((1, LANES), F32)],
        compiler_params=_cparams("arbitrary"),
        name="router",
    )(x, lw["router_hi"], lw["router_lo"], lw["router_b"], tri)


def _moe_schedule(info, cnt, tm_e, nt):
    eid = info[:, INFO_EID:INFO_EID + TOP_K].astype(jnp.int32)
    rank = info[:, INFO_RANK:INFO_RANK + TOP_K].astype(jnp.int32)
    counts = cnt[0, :N_EXPERTS].astype(jnp.int32)
    gsz = (counts + tm_e - 1) // tm_e * tm_e
    gend = jnp.cumsum(gsz)
    goff = gend - gsz
    onehot = eid[..., None] == jnp.arange(N_EXPERTS, dtype=jnp.int32)
    pos = jnp.sum(jnp.where(onehot, goff, 0), -1) + rank
    tile_start = jnp.arange(nt, dtype=jnp.int32) * tm_e
    tile_e = jnp.minimum(jnp.sum((gend[None, :] <= tile_start[:, None]).astype(jnp.int32), -1), N_EXPERTS - 1)
    n_valid = (gend[-1] // tm_e).reshape(1)
    i32 = lambda a: a.astype(jnp.int32)
    return i32(pos.reshape(-1)), i32(tile_e), i32(n_valid), i32(gend), i32(gsz)


def _dispatch_kernel(gend_ref, gsz_ref, nv_ref, pos_ref, x_ref, xs_ref, zbuf, sem, zsem, *, tm, tm_e, nt):
    @pl.when(pl.program_id(0) == 0)
    def _():
        zbuf[...] = jnp.zeros_like(zbuf)
        fill = lambda row0: pltpu.make_async_copy(zbuf, xs_ref.at[pl.ds(pl.multiple_of(row0, tm_e), tm_e)], zsem)
        for phase in ("start", "wait"):
            for e in range(N_EXPERTS):
                @pl.when(gsz_ref[e] > 0)
                def _(e=e, phase=phase):
                    getattr(fill(gend_ref[e] - tm_e), phase)()

            def tail(j, carry, phase=phase):
                getattr(fill(j * tm_e), phase)()
                return carry

            lax.fori_loop(nv_ref[0], nt, tail, 0)

    def issue(t, carry):
        for k in range(TOP_K):
            p = pos_ref[t * TOP_K + k]
            pltpu.make_async_copy(x_ref.at[pl.ds(t, 1)], xs_ref.at[pl.ds(p, 1)], sem).start(priority=k % 2)
        return carry

    lax.fori_loop(0, tm, issue, 0)
    for k in range(TOP_K):
        pltpu.make_async_copy(x_ref, xs_ref.at[pl.ds(0, tm)], sem).wait()


def _dispatch(x, pos, gend, gsz, n_valid, tm_e, nt, *, tm=512):
    T, d = x.shape
    tm = min(tm, T)
    kern = functools.partial(_dispatch_kernel, tm=tm, tm_e=tm_e, nt=nt)
    return pl.pallas_call(
        kern,
        grid_spec=pltpu.PrefetchScalarGridSpec(
            num_scalar_prefetch=3,
            grid=(T // tm,),
            in_specs=[pl.BlockSpec((tm * TOP_K,), lambda i, *_: (i,), memory_space=pltpu.SMEM),
                      pl.BlockSpec((tm, d), lambda i, *_: (i, 0))],
            out_specs=pl.BlockSpec(memory_space=pl.ANY),
            scratch_shapes=[pltpu.VMEM((tm_e, d), F32), pltpu.SemaphoreType.DMA, pltpu.SemaphoreType.DMA]),
        out_shape=jax.ShapeDtypeStruct((nt * tm_e, d), F32),
        compiler_params=_cparams("arbitrary"),
        name="moe_dispatch",
    )(gend, gsz, n_valid, pos, x)


def _expert_kernel(te_ref, nv_ref, xs_ref, wgu_ref, bgu_ref, wd_ref, bd_ref, ys_ref):
    del te_ref

    @pl.when(pl.program_id(0) < nv_ref[0])
    def _():
        h = _dot(xs_ref[...].astype(BF16), wgu_ref[0]) + bgu_ref[0]
        hg = jnp.minimum(h[:, :D_FF], SWIGLU_LIMIT)
        hl = jnp.clip(h[:, D_FF:], -SWIGLU_LIMIT, SWIGLU_LIMIT)
        act = hg * _sigmoid(SWIGLU_ALPHA * hg) * (hl + 1.0)
        ys_ref[...] = _dot(act.astype(BF16), wd_ref[0].astype(BF16)) + bd_ref[0]

    @pl.when(pl.program_id(0) >= nv_ref[0])
    def _():
        ys_ref[...] = jnp.zeros_like(ys_ref)


def _experts(xs, tile_e, n_valid, lw, tm_e):
    rows, d = xs.shape
    nt = rows // tm_e
    tile = lambda j, te, nv: (jnp.minimum(j, nv[0] - 1), 0)
    wsel = lambda j, te, nv: (te[jnp.minimum(j, nv[0] - 1)], 0, 0)
    e0 = lw["w_down_e0"]
    wsel_down = lambda j, te, nv: (e0 + te[jnp.minimum(j, nv[0] - 1)], 0, 0)
    return pl.pallas_call(
        _expert_kernel,
        grid_spec=pltpu.PrefetchScalarGridSpec(
            num_scalar_prefetch=2,
            grid=(nt,),
            in_specs=[pl.BlockSpec((tm_e, d), tile),
                      pl.BlockSpec((1, d, 2 * D_FF), wsel), pl.BlockSpec((1, 1, 2 * D_FF), wsel),
                      pl.BlockSpec((1, D_FF, d), wsel_down), pl.BlockSpec((1, 1, d), wsel)],
            out_specs=pl.BlockSpec((tm_e, d), lambda j, te, nv: (j, 0))),
        out_shape=jax.ShapeDtypeStruct((rows, d), F32),
        compiler_params=_cparams("arbitrary"),
        name="moe_experts",
    )(tile_e, n_valid, xs, lw["w_gu"], lw["b_gu"], lw["w_down"], lw["b_down"])


def _combine_kernel(pos_ref, x_ref, info_ref, ys_ref, ln2g_ref, ln2b_ref, out_ref, buf, sem, *, tm):
    def issue(t, carry):
        for k in range(TOP_K):
            p = pos_ref[t * TOP_K + k]
            pltpu.make_async_copy(ys_ref.at[pl.ds(p, 1)], buf.at[k, pl.ds(t, 1)], sem).start(priority=k % 2)
        return carry

    lax.fori_loop(0, tm, issue, 0)
    for k in range(TOP_K):
        pltpu.make_async_copy(ys_ref.at[pl.ds(0, tm)], buf.at[k], sem).wait()
    rc = min(128, tm)
    lane = lax.broadcasted_iota(jnp.int32, (rc, LANES), 1)

    def rows_pass(c, carry):
        rows = pl.ds(pl.multiple_of(c * rc, rc), rc)
        info = info_ref[rows, :]
        acc = DN_ALPHA * x_ref[rows, :]
        for k in range(TOP_K):
            gate = jnp.sum(jnp.where(lane == INFO_GATE + k, info, 0.0), -1, keepdims=True)
            acc = acc + gate * buf[k, rows, :]
        out_ref[rows, :] = _layer_norm(acc, ln2g_ref[...], ln2b_ref[...])
        return carry

    lax.fori_loop(0, tm // rc, rows_pass, 0)


def _combine(x, info, pos, ys, lw, *, tm=512):
    T, d = x.shape
    tm = min(tm, T)
    kern = functools.partial(_combine_kernel, tm=tm)
    return pl.pallas_call(
        kern,
        grid=(T // tm,),
        in_specs=[pl.BlockSpec((tm * TOP_K,), lambda i: (i,), memory_space=pltpu.SMEM),
                  pl.BlockSpec((tm, d), lambda i: (i, 0)),
                  pl.BlockSpec((tm, LANES), lambda i: (i, 0)),
                  pl.BlockSpec(memory_space=pl.ANY),
                  _full((1, d)), _full((1, d))],
        out_specs=pl.BlockSpec((tm, d), lambda i: (i, 0)),
        out_shape=jax.ShapeDtypeStruct((T, d), F32),
        scratch_shapes=[pltpu.VMEM((TOP_K, tm, d), F32), pltpu.SemaphoreType.DMA],
        compiler_params=_cparams("arbitrary"),
        name="moe_combine",
    )(pos, x, info, ys, lw["ln2_g"], lw["ln2_b"])


def _moe(x, lw):
    T = x.shape[0]
    tm_e = 512 if T >= 4096 else 128
    nt = T * TOP_K // tm_e + N_EXPERTS
    info, cnt = _router(x, lw)
    pos, tile_e, n_valid, gend, gsz = _moe_schedule(info, cnt, tm_e, nt)
    xs = _dispatch(x, pos, gend, gsz, n_valid, tm_e, nt)
    ys = _experts(xs, tile_e, n_valid, lw, tm_e)
    return _combine(x, info, pos, ys, lw)


def _ones_block_diag():
    idx = np.arange(RWKV_DIM) // RWKV_HEAD
    return jnp.asarray((idx[:, None] == idx[None, :]).astype(np.float32), dtype=BF16)


def _deinterleave_matrix():
    n = 2 * D_FF
    src = np.concatenate([np.arange(0, n, 2), np.arange(1, n, 2)])
    return jnp.asarray((np.arange(n)[:, None] == src[None, :]).astype(np.float32), dtype=BF16)


def _block_diag(x):
    C, G, r, c = x.shape
    eye = jnp.eye(G, dtype=x.dtype)
    return jnp.einsum("cgij,gh->cgihj", x, eye).reshape(C, G * r, G * c)


def _prep_layer(P, l):
    f = lambda name: P[name][l]
    lw = {}
    w_in = f("w_in")
    offs = np.cumsum((0,) + IN_SIZES)
    cols = lambda j: w_in[:, offs[j]:offs[j + 1]]
    w_q, w_kv = cols(0), cols(1)
    w_ckv, w_kpe = w_kv[:, :MLA_KV_RANK], w_kv[:, MLA_KV_RANK:]
    half = MLA_ROPE // 2
    zpad = jnp.zeros((D_MODEL, LANES - MLA_ROPE), F32)
    w_kpe_rot = jnp.concatenate([-w_kpe[:, half:], w_kpe[:, :half]], 1)
    lw["w_mla"] = jnp.concatenate([w_q, w_ckv, w_kpe, zpad, w_kpe_rot, zpad], 1).astype(BF16)
    lw["w_rw"] = cols(2).astype(BF16)
    lw["w_su"] = cols(3).astype(BF16)
    lw["w_gates"] = jnp.concatenate([cols(4), cols(5), cols(6)], 1).astype(BF16)
    lw["q_norm"] = f("mla_q_a_norm").reshape(1, -1)
    lw["kv_norm"] = f("mla_kv_a_norm").reshape(1, -1)
    wqb = f("mla_w_q_b").reshape(MLA_Q_RANK, MLA_HEADS, MLA_NOPE + MLA_ROPE)
    nope, x1, x2 = wqb[..., :MLA_NOPE], wqb[..., MLA_NOPE:MLA_NOPE + half], wqb[..., MLA_NOPE + half:]
    z32 = jnp.zeros_like(wqb[..., :HEAD_PAD - MLA_NOPE - MLA_ROPE])
    plain = jnp.concatenate([x1, x2, nope, z32], -1).reshape(MLA_Q_RANK, -1)
    rot = jnp.concatenate([-x2, x1, jnp.zeros_like(nope), z32], -1).reshape(MLA_Q_RANK, -1)
    lw["w_qb"] = jnp.concatenate([plain, rot], 1).astype(BF16)
    wkvb = f("mla_w_kv_b").reshape(MLA_KV_RANK, MLA_HEADS, MLA_NOPE + MLA_V)
    k_nope, v = wkvb[..., :MLA_NOPE], wkvb[..., MLA_NOPE:]
    zk = jnp.zeros_like(k_nope[..., :MLA_ROPE])
    lw["w_k"] = jnp.concatenate([zk, k_nope, zk], -1).reshape(MLA_KV_RANK, -1).astype(BF16)
    lw["w_v"] = v.reshape(MLA_KV_RANK, -1).astype(BF16)
    row = lambda name: f(name).reshape(1, -1)
    lw["mu"] = row("rwkv_mu")
    lw["w0"] = row("rwkv_w0")
    z64 = jnp.zeros((64, RWKV_DIM), F32)
    lw["w_up"] = jnp.concatenate([f("rwkv_w_up"), z64], 0).astype(BF16)
    lw["a_up"] = jnp.concatenate([z64, f("rwkv_a_up")], 0).astype(BF16)
    lw["a0"] = row("rwkv_a0")
    lw["g_up"] = f("rwkv_g_up").astype(BF16)
    lw["k_k"] = row("rwkv_k_k")
    lw["k_a"] = row("rwkv_k_a")
    lw["r_k"] = row("rwkv_r_k")
    lw["lnx_g"] = row("rwkv_lnx_g")
    lw["lnx_b"] = row("rwkv_lnx_b")
    lw["ones_bd"] = _ones_block_diag()
    ab_re, ab_im, bb_re, bb_im = _s5_params(f("ssm_a_re"), f("ssm_a_im"), f("ssm_log_dt"),
                                            f("ssm_b_re"), f("ssm_b_im"))
    gpc = SSM_GROUPS // SSM_CHUNKS
    chunked = lambda t: t.reshape(SSM_CHUNKS, gpc, t.shape[1], t.shape[2])
    lw["s5_bre"] = _block_diag(chunked(jnp.transpose(bb_re, (1, 0, 2)))).astype(BF16)
    lw["s5_bim"] = _block_diag(chunked(jnp.transpose(bb_im, (1, 0, 2)))).astype(BF16)
    lw["s5_cre"] = _block_diag(chunked(jnp.transpose(f("ssm_c_re"), (0, 2, 1)))).astype(BF16)
    lw["s5_cim"] = _block_diag(chunked(jnp.transpose(f("ssm_c_im"), (0, 2, 1)))).astype(BF16)
    lw["s5_d"] = row("ssm_d")
    lw["s5_are"] = ab_re.reshape(1, -1)
    lw["s5_aim"] = ab_im.reshape(1, -1)
    lw["glu_w"] = f("ssm_glu_w").astype(BF16)
    lw["glu_b"] = row("ssm_glu_b")
    for name in ("w_br_a", "w_br_b", "w_br_c", "w_out"):
        lw[name] = f(name).astype(BF16)
    for name in ("ln1_g", "ln1_b", "ln2_g", "ln2_b"):
        lw[name] = row(name)
    rw_ = jnp.pad(f("router_w"), ((0, 0), (0, LANES - N_EXPERTS)))
    hi = rw_.astype(BF16)
    lw["router_hi"] = hi
    lw["router_lo"] = (rw_ - hi.astype(F32)).astype(BF16)
    lw["router_b"] = jnp.pad(f("router_b"), (0, LANES - N_EXPERTS), constant_values=NEG_INF).reshape(1, -1)
    rows_l = N_EXPERTS * D_MODEL
    wgu = _matmul(P["exp_w_gu"].reshape(-1, 2 * D_FF), _deinterleave_matrix(), 1, rows_l, out_dtype=BF16,
                  row0=l * rows_l)
    lw["w_gu"] = wgu.reshape(N_EXPERTS, D_MODEL, 2 * D_FF)
    bgu = f("exp_b_gu")
    lw["b_gu"] = jnp.concatenate([bgu[..., 0::2], bgu[..., 1::2]], -1).reshape(N_EXPERTS, 1, 2 * D_FF)
    lw["w_down"] = P["exp_w_down"].reshape(-1, D_FF, D_MODEL)
    lw["w_down_e0"] = l * N_EXPERTS
    lw["b_down"] = f("exp_b_down").reshape(N_EXPERTS, 1, D_MODEL)
    return lw


def _rope_tables(pos):
    half = MLA_ROPE // 2
    inv = ROPE_THETA ** (-jnp.arange(half, dtype=F32) / half)
    ang = pos.astype(F32)[:, None] * inv
    cos, sin = jnp.cos(ang), jnp.sin(ang)
    n = pos.shape[0]
    tc = jnp.concatenate([cos, cos, jnp.ones((n, MLA_NOPE), F32),
                          jnp.zeros((n, HEAD_PAD - MLA_NOPE - MLA_ROPE), F32)], 1)
    ts = jnp.concatenate([sin, sin, jnp.zeros((n, HEAD_PAD - MLA_ROPE), F32)], 1)
    return tc, ts


def _to_lanes(t, nb, s, lanes):
    t = t.reshape(s, nb * RWKV_HEADS, RWKV_HEAD).transpose(0, 2, 1)
    return jnp.pad(t, ((0, 0), (0, 0), (0, lanes - nb * RWKV_HEADS)))


def _layer(x, lw, past, nb, s, tables, att_tk=LANES):
    T = nb * s
    tc, ts = tables
    if past is None:
        start = 0
        shift_p = jnp.zeros((nb, RWKV_PROJ), F32)
        wkv_p = jnp.zeros((nb, RWKV_HEADS, RWKV_HEAD, RWKV_HEAD), F32)
        sre_p = jnp.zeros((nb, SSM_CH), F32)
        sim_p = jnp.zeros((nb, SSM_CH), F32)
    else:
        ckv_p, kpe_p, shift_p, wkv_p, sre_p, sim_p = past
        start = ckv_p.shape[1]
        sre_p = sre_p.reshape(nb, SSM_CH)
        sim_p = sim_p.reshape(nb, SSM_CH)

    q, ckv, kpe128 = _mla_in(x, lw, tc, ts)
    if past is None:
        sk = s
        ckv_all, kpe_all = ckv, kpe128
    else:
        sk = start + s
        ckv_all = jnp.concatenate([ckv_p, ckv.reshape(nb, s, -1)], 1)
        kpe_new = kpe128.reshape(nb, s, LANES)
        kpe_all = jnp.concatenate([jnp.pad(kpe_p, ((0, 0), (0, 0), (0, LANES - MLA_ROPE))), kpe_new], 1)
    skp = -(-sk // att_tk) * att_tk
    if skp != sk:
        ckv_all = jnp.pad(ckv_all.reshape(nb, sk, -1), ((0, 0), (0, skp - sk), (0, 0)))
        kpe_all = jnp.pad(kpe_all.reshape(nb, sk, -1), ((0, 0), (0, skp - sk), (0, 0)))
    k_pad, v_all = _kv_expand(ckv_all.reshape(nb * skp, -1), kpe_all.reshape(nb * skp, -1), lw)
    ya = _attention(q, k_pad, v_all, nb, s, skp, sk, start)

    r, wdec, kn, v, kk, kb, g, bonus, last_row = _rwkv_pre(x, shift_p, lw, nb, s)
    lanes = -(-nb * RWKV_HEADS // LANES) * LANES
    seqs = [_to_lanes(t, nb, s, lanes) for t in (r, wdec, kn, v, kk, kb)]
    s0 = jnp.transpose(wkv_p, (3, 2, 0, 1)).reshape(RWKV_HEAD, RWKV_HEAD, nb * RWKV_HEADS)
    s0 = jnp.pad(s0, ((0, 0), (0, 0), (0, lanes - nb * RWKV_HEADS)))
    o_l, sT = _rwkv_scan(seqs, s0, s)
    o_tm = o_l[:, :, :nb * RWKV_HEADS].transpose(0, 2, 1).reshape(s, nb * RWKV_DIM)
    wkv_n = (sT[:, :, :nb * RWKV_HEADS].reshape(RWKV_HEAD, RWKV_HEAD, nb, RWKV_HEADS).transpose(2, 3, 1, 0))
    shift_n = last_row.reshape(nb, RWKV_PROJ)

    su_tm = _matmul(x, lw["w_su"], nb, s, time_major_out=True).reshape(s * nb, SSM_DIM)
    yc_tm, sre_n, sim_n = _s5(su_tm, sre_p, sim_p, lw, nb, s)
    yc_tm = yc_tm.reshape(s, nb * SSM_DIM)

    x1 = _merge(x, ya, o_tm, bonus, g, yc_tm, lw, nb, s)
    x2 = _moe(x1, lw)
    new = (ckv.reshape(nb, s, MLA_KV_RANK), kpe128[:, :MLA_ROPE].reshape(nb, s, MLA_ROPE), shift_n, wkv_n,
           sre_n.reshape(nb, SSM_GROUPS, SSM_STATE), sim_n.reshape(nb, SSM_GROUPS, SSM_STATE))
    return x2, new


def _trunk(x3, weights, caches):
    nb, s, d = x3.shape
    start = 0 if caches is None else caches[0].shape[2]
    tc, ts = _rope_tables(start + jnp.arange(s))
    tables = (jnp.tile(tc, (nb, 1)), jnp.tile(ts, (nb, 1)))
    x = x3.reshape(nb * s, d)
    new = []
    for l in range(len(weights)):
        past = None if caches is None else tuple(c[l] for c in caches)
        x, st = _layer(x, weights[l], past, nb, s, tables)
        new.append(st)
    return (x.reshape(nb, s, d),) + tuple(jnp.stack([st[j] for st in new]) for j in range(6))


def kernel(x_prompt, x_sample, cache_mla_ckv, cache_mla_kpe, state_rwkv_shift, state_rwkv_wkv, state_ssm_re, state_ssm_im, w_in, mla_q_a_norm, mla_w_q_b, mla_kv_a_norm, mla_w_kv_b, rwkv_mu, rwkv_w0, rwkv_w_up, rwkv_a0, rwkv_a_up, rwkv_g_up, rwkv_k_k, rwkv_k_a, rwkv_r_k, rwkv_lnx_g, rwkv_lnx_b, ssm_a_re, ssm_a_im, ssm_b_re, ssm_b_im, ssm_c_re, ssm_c_im, ssm_d, ssm_log_dt, ssm_glu_w, ssm_glu_b, w_br_a, w_br_b, w_br_c, w_out, ln1_g, ln1_b, router_w, router_b, exp_w_gu, exp_b_gu, exp_w_down, exp_b_down, ln2_g, ln2_b):
    P = dict(w_in=w_in, mla_q_a_norm=mla_q_a_norm, mla_w_q_b=mla_w_q_b, mla_kv_a_norm=mla_kv_a_norm,
             mla_w_kv_b=mla_w_kv_b, rwkv_mu=rwkv_mu, rwkv_w0=rwkv_w0, rwkv_w_up=rwkv_w_up, rwkv_a0=rwkv_a0,
             rwkv_a_up=rwkv_a_up, rwkv_g_up=rwkv_g_up, rwkv_k_k=rwkv_k_k, rwkv_k_a=rwkv_k_a, rwkv_r_k=rwkv_r_k,
             rwkv_lnx_g=rwkv_lnx_g, rwkv_lnx_b=rwkv_lnx_b, ssm_a_re=ssm_a_re, ssm_a_im=ssm_a_im,
             ssm_b_re=ssm_b_re, ssm_b_im=ssm_b_im, ssm_c_re=ssm_c_re, ssm_c_im=ssm_c_im, ssm_d=ssm_d,
             ssm_log_dt=ssm_log_dt, ssm_glu_w=ssm_glu_w, ssm_glu_b=ssm_glu_b, w_br_a=w_br_a, w_br_b=w_br_b,
             w_br_c=w_br_c, w_out=w_out, ln1_g=ln1_g, ln1_b=ln1_b, router_w=router_w, router_b=router_b,
             exp_w_gu=exp_w_gu, exp_b_gu=exp_b_gu, exp_w_down=exp_w_down, exp_b_down=exp_b_down,
             ln2_g=ln2_g, ln2_b=ln2_b)
    depth = w_in.shape[0]
    weights = [_prep_layer(P, l) for l in range(depth)]
    outs_p = _trunk(x_prompt, weights, None)
    caches = (cache_mla_ckv, cache_mla_kpe, state_rwkv_shift, state_rwkv_wkv, state_ssm_re, state_ssm_im)
    outs_s = _trunk(x_sample, weights, caches)
    return (outs_p[0], outs_s[0]) + outs_p[1:] + outs_s[1:]
```

```python
import functools
import math

import numpy as np
import jax
import jax.numpy as jnp
from jax import lax
from jax.experimental import pallas as pl
from jax.experimental.pallas import tpu as pltpu

F32 = jnp.float32
BF16 = jnp.bfloat16

D_MODEL = 1024
CHUNK = 64
MLA_HEADS = 8
MLA_NOPE = 64
MLA_ROPE = 32
MLA_V = 64
MLA_Q_RANK = 384
MLA_KV_RANK = 256
ROPE_THETA = 10000.0
HEAD_PAD = 128
RWKV_HEADS = 8
RWKV_HEAD = 64
RWKV_DIM = RWKV_HEADS * RWKV_HEAD
RWKV_PROJ = 3 * RWKV_DIM + 64 + 64 + 128
RWKV_LN_EPS = 64e-5
RWKV_BLOCK = 16
SSM_DIM = 512
SSM_GROUP = 16
SSM_GROUPS = 32
SSM_STATE = 64
SSM_CHUNKS = 4
SSM_CH = SSM_GROUPS * SSM_STATE
IN_SIZES = (MLA_Q_RANK, MLA_KV_RANK + MLA_ROPE, RWKV_PROJ, SSM_DIM, D_MODEL, D_MODEL, D_MODEL)
N_EXPERTS = 32
TOP_K = 4
D_FF = 512
SWIGLU_LIMIT = 7.0
SWIGLU_ALPHA = 1.702
DEPTH = 2
DN_ALPHA = (2 * DEPTH) ** 0.25
LN_EPS = 1e-5
RMS_EPS = 1e-6
NEG_INF = -1e30
ATT_SCALE = (MLA_NOPE + MLA_ROPE) ** -0.5
LANES = 128
VMEM_LIMIT = 48 * 1024 * 1024


def _cparams(*sem):
    return pltpu.CompilerParams(dimension_semantics=sem, vmem_limit_bytes=VMEM_LIMIT)


def _dot(a, b):
    return jnp.dot(a, b, preferred_element_type=F32)


def _sigmoid(x):
    return 1.0 / (1.0 + jnp.exp(-x))


def _layer_norm(x, g, b):
    mu = jnp.mean(x, -1, keepdims=True)
    xc = x - mu
    var = jnp.mean(xc * xc, -1, keepdims=True)
    return xc * lax.rsqrt(var + LN_EPS) * g + b


def _full(shape):
    n = len(shape)
    return pl.BlockSpec(shape, lambda *_: (0,) * n)


def _mm_kernel(x_ref, w_ref, o_ref):
    o_ref[...] = _dot(x_ref[...].astype(BF16), w_ref[...]).astype(o_ref.dtype)


def _matmul(x, w, nb, s, *, time_major_out=False, tm=512, tn=1024, out_dtype=F32, row0=0):
    K = x.shape[1]
    N = w.shape[1]
    tm = min(tm, s)
    tn = min(tn, N)
    while N % tn:
        tn -= LANES
    nt = s // tm
    blk0 = row0 // tm
    if time_major_out:
        assert tn == N
        out_shape = jax.ShapeDtypeStruct((s, nb * N), out_dtype)
        out_spec = pl.BlockSpec((tm, N), lambda b, i, j: (i, b))
    else:
        out_shape = jax.ShapeDtypeStruct((nb * s, N), out_dtype)
        out_spec = pl.BlockSpec((tm, tn), lambda b, i, j: (b * nt + i, j))
    return pl.pallas_call(
        _mm_kernel,
        grid=(nb, nt, N // tn),
        in_specs=[pl.BlockSpec((tm, K), lambda b, i, j: (blk0 + b * nt + i, 0)),
                  pl.BlockSpec((K, tn), lambda b, i, j: (0, j))],
        out_specs=out_spec,
        out_shape=out_shape,
        compiler_params=_cparams("parallel", "parallel", "arbitrary"),
        name="matmul",
    )(x, w)


def _mla_in_kernel(x_ref, w_ref, qg_ref, kvg_ref, wqb_ref, tc_ref, ts_ref, q_ref, ckv_ref, kpe_ref):
    h = _dot(x_ref[...].astype(BF16), w_ref[...])
    tc = tc_ref[...]
    ts = ts_ref[...]
    qa = h[:, :MLA_Q_RANK]
    qn = qa * lax.rsqrt(jnp.mean(qa * qa, -1, keepdims=True) + RMS_EPS) * qg_ref[...]
    q2 = _dot(qn.astype(BF16), wqb_ref[...])
    hp = MLA_HEADS * HEAD_PAD
    for hd in range(MLA_HEADS):
        lo = hd * HEAD_PAD
        q0 = q2[:, lo:lo + HEAD_PAD]
        q1 = q2[:, hp + lo:hp + lo + HEAD_PAD]
        q_ref[:, lo:lo + HEAD_PAD] = ((q0 * tc + q1 * ts) * ATT_SCALE).astype(BF16)
    c0 = MLA_Q_RANK
    ckv = h[:, c0:c0 + MLA_KV_RANK]
    ckv_ref[...] = ckv * lax.rsqrt(jnp.mean(ckv * ckv, -1, keepdims=True) + RMS_EPS) * kvg_ref[...]
    c1 = c0 + MLA_KV_RANK
    kpe_ref[...] = h[:, c1:c1 + LANES] * tc + h[:, c1 + LANES:c1 + 2 * LANES] * ts


def _mla_in(x, lw, tc, ts, *, tm=512):
    T = x.shape[0]
    tm = min(tm, T)
    hp = MLA_HEADS * HEAD_PAD
    nw = lw["w_mla"].shape[1]
    return pl.pallas_call(
        _mla_in_kernel,
        grid=(T // tm,),
        in_specs=[pl.BlockSpec((tm, D_MODEL), lambda i: (i, 0)),
                  _full((D_MODEL, nw)), _full((1, MLA_Q_RANK)), _full((1, MLA_KV_RANK)),
                  _full((MLA_Q_RANK, 2 * hp)),
                  pl.BlockSpec((tm, LANES), lambda i: (i, 0)),
                  pl.BlockSpec((tm, LANES), lambda i: (i, 0))],
        out_specs=[pl.BlockSpec((tm, hp), lambda i: (i, 0)),
                   pl.BlockSpec((tm, MLA_KV_RANK), lambda i: (i, 0)),
                   pl.BlockSpec((tm, LANES), lambda i: (i, 0))],
        out_shape=[jax.ShapeDtypeStruct((T, hp), BF16),
                   jax.ShapeDtypeStruct((T, MLA_KV_RANK), F32),
                   jax.ShapeDtypeStruct((T, LANES), F32)],
        compiler_params=_cparams("parallel"),
        name="mla_in",
    )(x, lw["w_mla"], lw["q_norm"], lw["kv_norm"], lw["w_qb"], tc, ts)


def _kv_expand_kernel(ckv_ref, kpe_ref, wk_ref, wv_ref, k_ref, v_ref):
    c = ckv_ref[...].astype(BF16)
    k = _dot(c, wk_ref[...])
    kpe = kpe_ref[...]
    for hd in range(MLA_HEADS):
        lo = hd * HEAD_PAD
        k_ref[:, lo:lo + HEAD_PAD] = (k[:, lo:lo + HEAD_PAD] + kpe).astype(BF16)
    v_ref[...] = _dot(c, wv_ref[...]).astype(BF16)


def _kv_expand(ckv, kpe, lw, *, tm=512):
    T = ckv.shape[0]
    tm = min(tm, T)
    while T % tm:
        tm //= 2
    hp = MLA_HEADS * HEAD_PAD
    hv = MLA_HEADS * MLA_V
    return pl.pallas_call(
        _kv_expand_kernel,
        grid=(T // tm,),
        in_specs=[pl.BlockSpec((tm, MLA_KV_RANK), lambda i: (i, 0)),
                  pl.BlockSpec((tm, LANES), lambda i: (i, 0)),
                  _full((MLA_KV_RANK, hp)), _full((MLA_KV_RANK, hv))],
        out_specs=[pl.BlockSpec((tm, hp), lambda i: (i, 0)),
                   pl.BlockSpec((tm, hv), lambda i: (i, 0))],
        out_shape=[jax.ShapeDtypeStruct((T, hp), BF16), jax.ShapeDtypeStruct((T, hv), BF16)],
        compiler_params=_cparams("parallel"),
        name="kv_expand",
    )(ckv, kpe, lw["w_k"], lw["w_v"])


def _attn_kernel(q_ref, k_ref, v_ref, o_ref, *, tq, nq, q_start, sk):
    lane = lax.broadcasted_iota(jnp.int32, (tq, LANES), 1)
    nt = (((1,), (1,)), ((), ()))
    for i in range(nq):
        q_lo = q_start + i * tq
        k_end = min(((q_lo + tq - 1) // CHUNK + 1) * CHUNK, sk)
        kw = -(-k_end // LANES) * LANES
        c0 = min((q_lo // CHUNK + 1) * CHUNK, sk) // LANES * LANES
        tail = kw - c0
        if tail:
            q_chunk = (q_lo + lax.broadcasted_iota(jnp.int32, (tq, tail), 0)) // CHUNK
            k_pos = c0 + lax.broadcasted_iota(jnp.int32, (tq, tail), 1)
            visible = k_pos // CHUNK <= q_chunk
            if kw > sk:
                visible = visible & (k_pos < sk)
        outs = []
        for hh in range(2):
            hs = slice(hh * HEAD_PAD, (hh + 1) * HEAD_PAD)
            q = q_ref[i * tq:(i + 1) * tq, hs]
            parts = []
            if c0:
                parts.append(lax.dot_general(q, k_ref[0:c0, hs], nt, preferred_element_type=F32))
            if tail:
                s_t = lax.dot_general(q, k_ref[c0:kw, hs], nt, preferred_element_type=F32)
                parts.append(jnp.where(visible, s_t, NEG_INF))
            m = jnp.max(parts[0], -1, keepdims=True)
            for s_ in parts[1:]:
                m = jnp.maximum(m, jnp.max(s_, -1, keepdims=True))
            l = None
            acc = None
            lo = 0
            for s_ in parts:
                p = jnp.exp(s_ - m)
                ps = jnp.sum(p, -1, keepdims=True)
                pv = _dot(p.astype(BF16), v_ref[lo:lo + s_.shape[1], :])
                l = ps if l is None else l + ps
                acc = pv if acc is None else acc + pv
                lo += s_.shape[1]
            outs.append(acc / l)
        o_ref[i * tq:(i + 1) * tq, :] = jnp.where(lane < MLA_V, outs[0], outs[1])


def _attention(q, k, v, nb, sq, skp, sk, q_start, *, tq=512):
    tq = min(tq, sq)
    kern = functools.partial(_attn_kernel, tq=tq, nq=sq // tq, q_start=q_start, sk=sk)
    return pl.pallas_call(
        kern,
        grid=(nb, MLA_HEADS // 2),
        in_specs=[pl.BlockSpec((sq, 2 * HEAD_PAD), lambda b, j: (b, j)),
                  pl.BlockSpec((skp, 2 * HEAD_PAD), lambda b, j: (b, j)),
                  pl.BlockSpec((skp, 2 * MLA_V), lambda b, j: (b, j))],
        out_specs=pl.BlockSpec((sq, 2 * MLA_V), lambda b, j: (b, j)),
        out_shape=jax.ShapeDtypeStruct((nb * sq, MLA_HEADS * MLA_V), F32),
        compiler_params=_cparams("parallel", "parallel"),
        name="attention",
    )(q, k, v)


def _head_sum(z, ones_bd):
    hi = z.astype(BF16)
    lo = (z - hi.astype(F32)).astype(BF16)
    return _dot(hi, ones_bd) + _dot(lo, ones_bd)


def _rwkv_pre_kernel(x_ref, xprev_ref, shift_ref, wrw_ref, mu_ref, w0_ref, wup_ref, a0_ref, aup_ref, gup_ref,
                     kk_ref, ka_ref, rk_ref, ones_ref, blk_ref,
                     r_out, w_out, k_out, v_out, kk_out, b_out, g_out, bonus_out, last_out):
    p = _dot(x_ref[...].astype(BF16), wrw_ref[...])
    first = pl.program_id(1) == 0
    p_before = _dot(xprev_ref[...].astype(BF16), wrw_ref[...])
    prev_row = jnp.where(first, shift_ref[0], p_before[7:8, :])
    last_out[0] = p[p.shape[0] - 1:, :]
    row = lax.broadcasted_iota(jnp.int32, p.shape, 0)
    prev = jnp.where(row == 0, prev_row, pltpu.roll(p, 1, 0))
    ps = p + (prev - p) * mu_ref[...]
    d = RWKV_DIM
    r = ps[:, 0:d]
    k = ps[:, d:2 * d]
    v = ps[:, 2 * d:3 * d]
    wa = ps[:, 3 * d:3 * d + LANES]
    gd = ps[:, 3 * d + LANES:3 * d + 2 * LANES]
    ones_bd = ones_ref[...]
    wlin = w0_ref[...] + _dot(jnp.tanh(wa).astype(BF16), wup_ref[...])
    z = -wlin
    w = -(jnp.maximum(z, 0.0) + jnp.log1p(jnp.exp(-jnp.abs(z)))) - 0.5
    log_decay = -jnp.exp(w)
    blk = blk_ref[...]
    rows = blk.shape[0]
    pieces, rest = [], log_decay
    for _ in range(3):
        pieces.append(rest.astype(BF16))
        rest = rest - pieces[-1].astype(F32)
    cum = jnp.concatenate(
        [sum(_dot(blk, pc[g * rows:(g + 1) * rows]) for pc in pieces) for g in range(p.shape[0] // rows)], 0)
    p_incl = jnp.exp(cum)
    p_inv = jnp.exp(-cum)
    w_out[...] = p_incl
    a = _sigmoid(a0_ref[...] + _dot(wa.astype(BF16), aup_ref[...]))
    g_out[...] = _dot(_sigmoid(gd).astype(BF16), gup_ref[...])
    kk = k * kk_ref[...]
    kkn = kk * lax.rsqrt(_head_sum(kk * kk, ones_bd) + 1e-12)
    kk_out[...] = kkn * jnp.exp(cum - log_decay)
    b_out[...] = kkn * a * p_inv
    kn = k * (1.0 + (a - 1.0) * ka_ref[...])
    r_out[...] = r * p_incl
    k_out[...] = kn * p_inv
    v_out[...] = v
    bonus_out[...] = _head_sum(r * kn * rk_ref[...], ones_bd) * v


def _rwkv_pre(x, shift_prev, lw, nb, s, *, tm=512):
    tm = min(tm, s)
    tps = s // tm
    d = RWKV_DIM
    vec = lambda n: _full((1, n))
    bm = pl.BlockSpec((tm, d), lambda b, i: (b * tps + i, 0))
    tmaj = pl.BlockSpec((tm, d), lambda b, i: (i, b))
    per_seq = pl.BlockSpec((1, 1, RWKV_PROJ), lambda b, i: (b, 0, 0))
    bm_shape = jax.ShapeDtypeStruct((nb * s, d), F32)
    tm_shape = jax.ShapeDtypeStruct((s, nb * d), F32)
    br = min(tm, LANES)
    assert tm % br == 0 and br % RWKV_BLOCK == 0
    idx = np.arange(br)
    blk = (idx[:, None] // RWKV_BLOCK == idx[None, :] // RWKV_BLOCK) & (idx[None, :] <= idx[:, None])
    blk = jnp.asarray(blk.astype(np.float32), dtype=BF16)
    return pl.pallas_call(
        _rwkv_pre_kernel,
        grid=(nb, tps),
        in_specs=[pl.BlockSpec((tm, D_MODEL), lambda b, i: (b * tps + i, 0)),
                  pl.BlockSpec((8, D_MODEL), lambda b, i: (jnp.maximum((b * tps + i) * (tm // 8) - 1, 0), 0)),
                  per_seq, _full((D_MODEL, RWKV_PROJ)),
                  vec(RWKV_PROJ), vec(d), _full((LANES, d)), vec(d), _full((LANES, d)),
                  _full((LANES, d)), vec(d), vec(d), vec(d), _full((d, d)), _full((br, br))],
        out_specs=[tmaj] * 6 + [bm] * 2 + [per_seq],
        out_shape=[tm_shape] * 6 + [bm_shape] * 2 + [jax.ShapeDtypeStruct((nb, 1, RWKV_PROJ), F32)],
        compiler_params=_cparams("parallel", "arbitrary"),
        name="rwkv_pre",
    )(x, x, shift_prev.reshape(nb, 1, RWKV_PROJ), lw["w_rw"], lw["mu"], lw["w0"], lw["w_up"], lw["a0"],
      lw["a_up"], lw["g_up"], lw["k_k"], lw["k_a"], lw["r_k"], lw["ones_bd"], blk)


RWKV_VC = 32


def _rwkv_scan_kernel(r_ref, p_ref, k_ref, v_ref, kk_ref, b_ref, s0_ref, o_ref, sT_ref, st_ref, *, tt):
    n = RWKV_HEAD

    @pl.when(pl.program_id(0) == 0)
    def _():
        st_ref[...] = s0_ref[...]

    def step(t, carry):
        for c in range(n // RWKV_VC):
            vs = slice(c * RWKV_VC, (c + 1) * RWKV_VC)
            parts = [None] * 4
            for q in range(n):
                term = st_ref[q, vs, :] * kk_ref[t, q:q + 1, :]
                parts[q % 4] = term if parts[q % 4] is None else parts[q % 4] + term
            sa = -((parts[0] + parts[1]) + (parts[2] + parts[3]))
            vt = v_ref[t, vs, :]
            outs = [None] * 4
            for q in range(n):
                s_new = st_ref[q, vs, :] + (sa * b_ref[t, q:q + 1, :] + vt * k_ref[t, q:q + 1, :])
                st_ref[q, vs, :] = s_new
                term = s_new * r_ref[t, q:q + 1, :]
                outs[q % 4] = term if outs[q % 4] is None else outs[q % 4] + term
            o_ref[t, vs, :] = (outs[0] + outs[1]) + (outs[2] + outs[3])
        return carry

    lax.fori_loop(0, tt, step, 0)
    for q in range(n):
        st_ref[q] = st_ref[q] * p_ref[tt - 1, q:q + 1, :]

    @pl.when(pl.program_id(0) == pl.num_programs(0) - 1)
    def _():
        sT_ref[...] = st_ref[...]


def _rwkv_scan(seqs, s0, s):
    L = s0.shape[-1]
    n = RWKV_HEAD
    tt = RWKV_BLOCK
    assert s % tt == 0
    blk = pl.BlockSpec((tt, n, L), lambda i: (i, 0, 0))
    kern = functools.partial(_rwkv_scan_kernel, tt=tt)
    return pl.pallas_call(
        kern,
        grid=(s // tt,),
        in_specs=[blk] * 6 + [_full((n, n, L))],
        out_specs=[blk, _full((n, n, L))],
        out_shape=[jax.ShapeDtypeStruct((s, n, L), F32), jax.ShapeDtypeStruct((n, n, L), F32)],
        scratch_shapes=[pltpu.VMEM((n, n, L), F32)],
        compiler_params=_cparams("arbitrary"),
        name="rwkv_scan",
    )(*seqs, s0)


def _s5_params_kernel(are_ref, aim_ref, ldt_ref, bre_ref, bim_ref, abre_ref, abim_ref, bbre_ref, bbim_ref):
    lr = are_ref[...]
    li = aim_ref[...]
    dt = jnp.exp(ldt_ref[...])
    mag = jnp.exp(lr * dt)
    ab_re = mag * jnp.cos(li * dt)
    ab_im = mag * jnp.sin(li * dt)
    den = lr * lr + li * li
    f_re = ((ab_re - 1.0) * lr + ab_im * li) / den
    f_im = (ab_im * lr - (ab_re - 1.0) * li) / den
    abre_ref[...] = ab_re
    abim_ref[...] = ab_im
    for i in range(SSM_GROUP):
        br = bre_ref[i]
        bi = bim_ref[i]
        bbre_ref[i] = f_re * br - f_im * bi
        bbim_ref[i] = f_re * bi + f_im * br


def _s5_params(a_re, a_im, log_dt, b_re, b_im):
    g, n = a_re.shape
    gn = jax.ShapeDtypeStruct((g, n), F32)
    ign = jax.ShapeDtypeStruct((SSM_GROUP, g, n), F32)
    return pl.pallas_call(
        _s5_params_kernel,
        out_shape=[gn, gn, ign, ign],
        name="s5_params",
    )(a_re, a_im, log_dt.reshape(g, 1), jnp.transpose(b_re, (2, 0, 1)), jnp.transpose(b_im, (2, 0, 1)))


def _gelu_tanh(x):
    return 0.5 * x * (1.0 + jnp.tanh(math.sqrt(2.0 / math.pi) * (x + 0.044715 * (x * x * x))))


def _s5_kernel(u_ref, bre_ref, bim_ref, cre_ref, cim_ref, d_ref, are_ref, aim_ref, h0re_ref, h0im_ref,
               gw_ref, gb_ref, y_ref, hTre_ref, hTim_ref, hre_s, him_s, sre_s, sim_s, yy_s, *, tt, nb):
    cw = SSM_CH // SSM_CHUNKS

    @pl.when(pl.program_id(0) == 0)
    def _():
        hre_s[...] = h0re_ref[...]
        him_s[...] = h0im_ref[...]

    for c in range(SSM_CHUNKS):
        uc = u_ref[:, c * LANES:(c + 1) * LANES]
        ub = uc.astype(BF16)
        sre_s[...] = _dot(ub, bre_ref[c])
        sim_s[...] = _dot(ub, bim_ref[c])
        a_re = jnp.broadcast_to(are_ref[:, c * cw:(c + 1) * cw], (nb, cw))
        a_im = jnp.broadcast_to(aim_ref[:, c * cw:(c + 1) * cw], (nb, cw))

        def step(t, carry, a_re=a_re, a_im=a_im):
            hr, hi = carry
            rows = pl.ds(pl.multiple_of(t * nb, nb), nb)
            nr = a_re * hr - a_im * hi + sre_s[rows, :]
            ni = a_re * hi + a_im * hr + sim_s[rows, :]
            sre_s[rows, :] = nr
            sim_s[rows, :] = ni
            return nr, ni

        hr, hi = lax.fori_loop(0, tt, step, (hre_s[:, c * cw:(c + 1) * cw], him_s[:, c * cw:(c + 1) * cw]))
        hre_s[:, c * cw:(c + 1) * cw] = hr
        him_s[:, c * cw:(c + 1) * cw] = hi
        yc = _dot(sre_s[...].astype(BF16), cre_ref[c]) - _dot(sim_s[...].astype(BF16), cim_ref[c])
        yy_s[:, c * LANES:(c + 1) * LANES] = yc + d_ref[:, c * LANES:(c + 1) * LANES] * uc

    y = _gelu_tanh(yy_s[...])
    z = _dot(y.astype(BF16), gw_ref[...]) + gb_ref[...]
    y_ref[...] = z[:, :SSM_DIM] * _sigmoid(z[:, SSM_DIM:])

    @pl.when(pl.program_id(0) == pl.num_programs(0) - 1)
    def _():
        hTre_ref[...] = hre_s[...]
        hTim_ref[...] = him_s[...]


def _s5(u_tm, h0_re, h0_im, lw, nb, s, *, rows=512):
    tt = max(min(rows // nb, s), 1)
    R = tt * nb
    cw = SSM_CH // SSM_CHUNKS
    kern = functools.partial(_s5_kernel, tt=tt, nb=nb)
    return pl.pallas_call(
        kern,
        grid=(s // tt,),
        in_specs=[pl.BlockSpec((R, SSM_DIM), lambda i: (i, 0)),
                  _full((SSM_CHUNKS, LANES, cw)), _full((SSM_CHUNKS, LANES, cw)),
                  _full((SSM_CHUNKS, cw, LANES)), _full((SSM_CHUNKS, cw, LANES)),
                  _full((1, SSM_DIM)), _full((1, SSM_CH)), _full((1, SSM_CH)),
                  _full((nb, SSM_CH)), _full((nb, SSM_CH)),
                  _full((SSM_DIM, 2 * SSM_DIM)), _full((1, 2 * SSM_DIM))],
        out_specs=[pl.BlockSpec((R, SSM_DIM), lambda i: (i, 0)), _full((nb, SSM_CH)), _full((nb, SSM_CH))],
        out_shape=[jax.ShapeDtypeStruct((s * nb, SSM_DIM), F32),
                   jax.ShapeDtypeStruct((nb, SSM_CH), F32), jax.ShapeDtypeStruct((nb, SSM_CH), F32)],
        scratch_shapes=[pltpu.VMEM((nb, SSM_CH), F32), pltpu.VMEM((nb, SSM_CH), F32),
                        pltpu.VMEM((R, cw), F32), pltpu.VMEM((R, cw), F32), pltpu.VMEM((R, SSM_DIM), F32)],
        compiler_params=_cparams("arbitrary"),
        name="s5",
    )(u_tm, lw["s5_bre"], lw["s5_bim"], lw["s5_cre"], lw["s5_cim"], lw["s5_d"], lw["s5_are"], lw["s5_aim"],
      h0_re, h0_im, lw["glu_w"], lw["glu_b"])


def _merge_kernel(x_ref, ya_ref, o_ref, bonus_ref, g_ref, yc_ref, wg_ref, lng_ref, lnb_ref, ones_ref,
                  wa_ref, wb_ref, wc_ref, wo_ref, ln1g_ref, ln1b_ref, out_ref):
    ones_bd = ones_ref[...]
    o = o_ref[...]
    inv_n = 1.0 / RWKV_HEAD
    mean = _head_sum(o, ones_bd) * inv_n
    oc = o - mean
    var = _head_sum(oc * oc, ones_bd) * inv_n
    yb = (oc * lax.rsqrt(var + RWKV_LN_EPS) * lng_ref[...] + lnb_ref[...] + bonus_ref[...]) * g_ref[...]
    d = D_MODEL
    x = x_ref[...]
    xb = x.astype(BF16)
    merged = None
    for j, (y, w_ref) in enumerate(((ya_ref[...], wa_ref), (yb, wb_ref), (yc_ref[...], wc_ref))):
        gate = _sigmoid(_dot(xb, wg_ref[:, j * d:(j + 1) * d]))
        term = gate * _dot(y.astype(BF16), w_ref[...])
        merged = term if merged is None else merged + term
    y = DN_ALPHA * x + _dot(merged.astype(BF16), wo_ref[...])
    out_ref[...] = _layer_norm(y, ln1g_ref[...], ln1b_ref[...])


def _merge(x, ya, o_tm, bonus, g, yc_tm, lw, nb, s, *, tm=512):
    tm = min(tm, s)
    nt = s // tm
    d = D_MODEL
    h = RWKV_DIM
    row = lambda w: pl.BlockSpec((tm, w), lambda b, i: (b * nt + i, 0))
    tmaj = lambda w: pl.BlockSpec((tm, w), lambda b, i: (i, b))
    vec = lambda n: _full((1, n))
    once = lambda shape: pl.BlockSpec(shape, lambda b, i: (0, 0), pipeline_mode=pl.Buffered(1))
    return pl.pallas_call(
        _merge_kernel,
        grid=(nb, nt),
        in_specs=[row(d), row(h), tmaj(h), row(h), row(h), tmaj(SSM_DIM),
                  once((d, 3 * d)), vec(h), vec(h), once((h, h)),
                  once((h, d)), once((h, d)), once((h, d)), once((d, d)), vec(d), vec(d)],
        out_specs=row(d),
        out_shape=jax.ShapeDtypeStruct((nb * s, d), F32),
        compiler_params=_cparams("parallel", "parallel"),
        name="merge",
    )(x, ya, o_tm, bonus, g, yc_tm, lw["w_gates"], lw["lnx_g"], lw["lnx_b"], lw["ones_bd"],
      lw["w_br_a"], lw["w_br_b"], lw["w_br_c"], lw["w_out"], lw["ln1_g"], lw["ln1_b"])


INFO_GATE, INFO_EID, INFO_RANK = 0, 4, 8


def _router_kernel(x_ref, whi_ref, wlo_ref, b_ref, tri_ref, info_ref, cnt_ref, base_s):
    @pl.when(pl.program_id(0) == 0)
    def _():
        base_s[...] = jnp.zeros_like(base_s)

    x = x_ref[...]
    xh = x.astype(BF16)
    xl = (x - xh.astype(F32)).astype(BF16)
    logits = _dot(xh, whi_ref[...]) + _dot(xl, whi_ref[...]) + _dot(xh, wlo_ref[...]) + b_ref[...]
    lane_i = lax.broadcasted_iota(jnp.int32, logits.shape, 1)
    lane = lane_i.astype(F32)
    vals, sels, ids = [], [], []
    for _ in range(TOP_K):
        m = jnp.max(logits, -1, keepdims=True)
        idx = jnp.min(jnp.where(logits == m, lane, float(LANES)), -1, keepdims=True)
        sel = lane == idx
        vals.append(m)
        sels.append(sel)
        ids.append(idx)
        logits = jnp.where(sel, -3e38, logits)
    es = [jnp.exp(v - vals[0]) for v in vals]
    den = es[0] + es[1] + es[2] + es[3]
    chosen = jnp.zeros_like(logits)
    for sel in sels:
        chosen = chosen + jnp.where(sel, 1.0, 0.0)
    before = _dot(tri_ref[...], chosen.astype(BF16)) + base_s[...]
    info = jnp.zeros_like(logits)
    for k in range(TOP_K):
        rank = jnp.sum(jnp.where(sels[k], before, 0.0), -1, keepdims=True)
        info = (info + jnp.where(lane_i == INFO_GATE + k, es[k] / den, 0.0)
                + jnp.where(lane_i == INFO_EID + k, ids[k], 0.0)
                + jnp.where(lane_i == INFO_RANK + k, rank, 0.0))
    info_ref[...] = info
    base_s[...] += jnp.sum(chosen, 0, keepdims=True)
    cnt_ref[...] = base_s[...]


def _router(x, lw, *, tm=512):
    T = x.shape[0]
    tm = min(tm, T)
    tri = jnp.asarray(np.tril(np.ones((tm, tm), np.float32), -1), dtype=BF16)
    return pl.pallas_call(
        _router_kernel,
        grid=(T // tm,),
        in_specs=[pl.BlockSpec((tm, D_MODEL), lambda i: (i, 0)),
                  _full((D_MODEL, LANES)), _full((D_MODEL, LANES)), _full((1, LANES)), _full((tm, tm))],
        out_specs=[pl.BlockSpec((tm, LANES), lambda i: (i, 0)), _full((1, LANES))],
        out_shape=[jax.ShapeDtypeStruct((T, LANES), F32), jax.ShapeDtypeStruct((1, LANES), F32)],
        scratch_shapes=[pltpu.VMEM((1, LANES), F32)],
        compiler_params=_cparams("arbitrary"),
        name="router",
    )(x, lw["router_hi"], lw["router_lo"], lw["router_b"], tri)


def _moe_schedule(info, cnt, tm_e, nt):
    eid = info[:, INFO_EID:INFO_EID + TOP_K].astype(jnp.int32)
    rank = info[:, INFO_RANK:INFO_RANK + TOP_K].astype(jnp.int32)
    counts = cnt[0, :N_EXPERTS].astype(jnp.int32)
    gsz = (counts + tm_e - 1) // tm_e * tm_e
    gend = jnp.cumsum(gsz)
    goff = gend - gsz
    onehot = eid[..., None] == jnp.arange(N_EXPERTS, dtype=jnp.int32)
    pos = jnp.sum(jnp.where(onehot, goff, 0), -1) + rank
    tile_start = jnp.arange(nt, dtype=jnp.int32) * tm_e
    tile_e = jnp.minimum(jnp.sum((gend[None, :] <= tile_start[:, None]).astype(jnp.int32), -1), N_EXPERTS - 1)
    n_valid = (gend[-1] // tm_e).reshape(1)
    i32 = lambda a: a.astype(jnp.int32)
    return i32(pos.reshape(-1)), i32(tile_e), i32(n_valid), i32(gend), i32(gsz)


def _dispatch_kernel(gend_ref, gsz_ref, nv_ref, pos_ref, x_ref, xs_ref, zbuf, sem, zsem, *, tm, tm_e, nt):
    @pl.when(pl.program_id(0) == 0)
    def _():
        zbuf[...] = jnp.zeros_like(zbuf)
        fill = lambda row0: pltpu.make_async_copy(zbuf, xs_ref.at[pl.ds(pl.multiple_of(row0, tm_e), tm_e)], zsem)
        for phase in ("start", "wait"):
            for e in range(N_EXPERTS):
                @pl.when(gsz_ref[e] > 0)
                def _(e=e, phase=phase):
                    getattr(fill(gend_ref[e] - tm_e), phase)()

            def tail(j, carry, phase=phase):
                getattr(fill(j * tm_e), phase)()
                return carry

            lax.fori_loop(nv_ref[0], nt, tail, 0)

    def issue(t, carry):
        for k in range(TOP_K):
            p = pos_ref[t * TOP_K + k]
            pltpu.make_async_copy(x_ref.at[pl.ds(t, 1)], xs_ref.at[pl.ds(p, 1)], sem).start(priority=k % 2)
        return carry

    lax.fori_loop(0, tm, issue, 0)
    for k in range(TOP_K):
        pltpu.make_async_copy(x_ref, xs_ref.at[pl.ds(0, tm)], sem).wait()


def _dispatch(x, pos, gend, gsz, n_valid, tm_e, nt, *, tm=512):
    T, d = x.shape
    tm = min(tm, T)
    kern = functools.partial(_dispatch_kernel, tm=tm, tm_e=tm_e, nt=nt)
    return pl.pallas_call(
        kern,
        grid_spec=pltpu.PrefetchScalarGridSpec(
            num_scalar_prefetch=3,
            grid=(T // tm,),
            in_specs=[pl.BlockSpec((tm * TOP_K,), lambda i, *_: (i,), memory_space=pltpu.SMEM),
                      pl.BlockSpec((tm, d), lambda i, *_: (i, 0))],
            out_specs=pl.BlockSpec(memory_space=pl.ANY),
            scratch_shapes=[pltpu.VMEM((tm_e, d), F32), pltpu.SemaphoreType.DMA, pltpu.SemaphoreType.DMA]),
        out_shape=jax.ShapeDtypeStruct((nt * tm_e, d), F32),
        compiler_params=_cparams("arbitrary"),
        name="moe_dispatch",
    )(gend, gsz, n_valid, pos, x)


def _expert_kernel(te_ref, nv_ref, xs_ref, wgu_ref, bgu_ref, wd_ref, bd_ref, ys_ref):
    del te_ref

    @pl.when(pl.program_id(0) < nv_ref[0])
    def _():
        h = _dot(xs_ref[...].astype(BF16), wgu_ref[0]) + bgu_ref[0]
        hg = jnp.minimum(h[:, :D_FF], SWIGLU_LIMIT)
        hl = jnp.clip(h[:, D_FF:], -SWIGLU_LIMIT, SWIGLU_LIMIT)
        act = hg * _sigmoid(SWIGLU_ALPHA * hg) * (hl + 1.0)
        ys_ref[...] = _dot(act.astype(BF16), wd_ref[0].astype(BF16)) + bd_ref[0]

    @pl.when(pl.program_id(0) >= nv_ref[0])
    def _():
        ys_ref[...] = jnp.zeros_like(ys_ref)


def _experts(xs, tile_e, n_valid, lw, tm_e):
    rows, d = xs.shape
    nt = rows // tm_e
    tile = lambda j, te, nv: (jnp.minimum(j, nv[0] - 1), 0)
    wsel = lambda j, te, nv: (te[jnp.minimum(j, nv[0] - 1)], 0, 0)
    e0 = lw["w_down_e0"]
    wsel_down = lambda j, te, nv: (e0 + te[jnp.minimum(j, nv[0] - 1)], 0, 0)
    return pl.pallas_call(
        _expert_kernel,
        grid_spec=pltpu.PrefetchScalarGridSpec(
            num_scalar_prefetch=2,
            grid=(nt,),
            in_specs=[pl.BlockSpec((tm_e, d), tile),
                      pl.BlockSpec((1, d, 2 * D_FF), wsel), pl.BlockSpec((1, 1, 2 * D_FF), wsel),
                      pl.BlockSpec((1, D_FF, d), wsel_down), pl.BlockSpec((1, 1, d), wsel)],
            out_specs=pl.BlockSpec((tm_e, d), lambda j, te, nv: (j, 0))),
        out_shape=jax.ShapeDtypeStruct((rows, d), F32),
        compiler_params=_cparams("arbitrary"),
        name="moe_experts",
    )(tile_e, n_valid, xs, lw["w_gu"], lw["b_gu"], lw["w_down"], lw["b_down"])


def _combine_kernel(pos_ref, x_ref, info_ref, ys_ref, ln2g_ref, ln2b_ref, out_ref, buf, sem, *, tm):
    def issue(t, carry):
        for k in range(TOP_K):
            p = pos_ref[t * TOP_K + k]
            pltpu.make_async_copy(ys_ref.at[pl.ds(p, 1)], buf.at[k, pl.ds(t, 1)], sem).start(priority=k % 2)
        return carry

    lax.fori_loop(0, tm, issue, 0)
    for k in range(TOP_K):
        pltpu.make_async_copy(ys_ref.at[pl.ds(0, tm)], buf.at[k], sem).wait()
    rc = min(128, tm)
    lane = lax.broadcasted_iota(jnp.int32, (rc, LANES), 1)

    def rows_pass(c, carry):
        rows = pl.ds(pl.multiple_of(c * rc, rc), rc)
        info = info_ref[rows, :]
        acc = DN_ALPHA * x_ref[rows, :]
        for k in range(TOP_K):
            gate = jnp.sum(jnp.where(lane == INFO_GATE + k, info, 0.0), -1, keepdims=True)
            acc = acc + gate * buf[k, rows, :]
        out_ref[rows, :] = _layer_norm(acc, ln2g_ref[...], ln2b_ref[...])
        return carry

    lax.fori_loop(0, tm // rc, rows_pass, 0)


def _combine(x, info, pos, ys, lw, *, tm=512):
    T, d = x.shape
    tm = min(tm, T)
    kern = functools.partial(_combine_kernel, tm=tm)
    return pl.pallas_call(
        kern,
        grid=(T // tm,),
        in_specs=[pl.BlockSpec((tm * TOP_K,), lambda i: (i,), memory_space=pltpu.SMEM),
                  pl.BlockSpec((tm, d), lambda i: (i, 0)),
                  pl.BlockSpec((tm, LANES), lambda i: (i, 0)),
                  pl.BlockSpec(memory_space=pl.ANY),
                  _full((1, d)), _full((1, d))],
        out_specs=pl.BlockSpec((tm, d), lambda i: (i, 0)),
        out_shape=jax.ShapeDtypeStruct((T, d), F32),
        scratch_shapes=[pltpu.VMEM((TOP_K, tm, d), F32), pltpu.SemaphoreType.DMA],
        compiler_params=_cparams("arbitrary"),
        name="moe_combine",
    )(pos, x, info, ys, lw["ln2_g"], lw["ln2_b"])


def _moe(x, lw):
    T = x.shape[0]
    tm_e = 512 if T >= 4096 else 128
    nt = T * TOP_K // tm_e + N_EXPERTS
    info, cnt = _router(x, lw)
    pos, tile_e, n_valid, gend, gsz = _moe_schedule(info, cnt, tm_e, nt)
    xs = _dispatch(x, pos, gend, gsz, n_valid, tm_e, nt)
    ys = _experts(xs, tile_e, n_valid, lw, tm_e)
    return _combine(x, info, pos, ys, lw)


def _ones_block_diag():
    idx = np.arange(RWKV_DIM) // RWKV_HEAD
    return jnp.asarray((idx[:, None] == idx[None, :]).astype(np.float32), dtype=BF16)


def _deinterleave_matrix():
    n = 2 * D_FF
    src = np.concatenate([np.arange(0, n, 2), np.arange(1, n, 2)])
    return jnp.asarray((np.arange(n)[:, None] == src[None, :]).astype(np.float32), dtype=BF16)


def _block_diag(x):
    C, G, r, c = x.shape
    eye = jnp.eye(G, dtype=x.dtype)
    return jnp.einsum("cgij,gh->cgihj", x, eye).reshape(C, G * r, G * c)


def _prep_layer(P, l):
    f = lambda name: P[name][l]
    lw = {}
    w_in = f("w_in")
    offs = np.cumsum((0,) + IN_SIZES)
    cols = lambda j: w_in[:, offs[j]:offs[j + 1]]
    w_q, w_kv = cols(0), cols(1)
    w_ckv, w_kpe = w_kv[:, :MLA_KV_RANK], w_kv[:, MLA_KV_RANK:]
    half = MLA_ROPE // 2
    zpad = jnp.zeros((D_MODEL, LANES - MLA_ROPE), F32)
    w_kpe_rot = jnp.concatenate([-w_kpe[:, half:], w_kpe[:, :half]], 1)
    lw["w_mla"] = jnp.concatenate([w_q, w_ckv, w_kpe, zpad, w_kpe_rot, zpad], 1).astype(BF16)
    lw["w_rw"] = cols(2).astype(BF16)
    lw["w_su"] = cols(3).astype(BF16)
    lw["w_gates"] = jnp.concatenate([cols(4), cols(5), cols(6)], 1).astype(BF16)
    lw["q_norm"] = f("mla_q_a_norm").reshape(1, -1)
    lw["kv_norm"] = f("mla_kv_a_norm").reshape(1, -1)
    wqb = f("mla_w_q_b").reshape(MLA_Q_RANK, MLA_HEADS, MLA_NOPE + MLA_ROPE)
    nope, x1, x2 = wqb[..., :MLA_NOPE], wqb[..., MLA_NOPE:MLA_NOPE + half], wqb[..., MLA_NOPE + half:]
    z32 = jnp.zeros_like(wqb[..., :HEAD_PAD - MLA_NOPE - MLA_ROPE])
    plain = jnp.concatenate([x1, x2, nope, z32], -1).reshape(MLA_Q_RANK, -1)
    rot = jnp.concatenate([-x2, x1, jnp.zeros_like(nope), z32], -1).reshape(MLA_Q_RANK, -1)
    lw["w_qb"] = jnp.concatenate([plain, rot], 1).astype(BF16)
    wkvb = f("mla_w_kv_b").reshape(MLA_KV_RANK, MLA_HEADS, MLA_NOPE + MLA_V)
    k_nope, v = wkvb[..., :MLA_NOPE], wkvb[..., MLA_NOPE:]
    zk = jnp.zeros_like(k_nope[..., :MLA_ROPE])
    lw["w_k"] = jnp.concatenate([zk, k_nope, zk], -1).reshape(MLA_KV_RANK, -1).astype(BF16)
    lw["w_v"] = v.reshape(MLA_KV_RANK, -1).astype(BF16)
    row = lambda name: f(name).reshape(1, -1)
    lw["mu"] = row("rwkv_mu")
    lw["w0"] = row("rwkv_w0")
    z64 = jnp.zeros((64, RWKV_DIM), F32)
    lw["w_up"] = jnp.concatenate([f("rwkv_w_up"), z64], 0).astype(BF16)
    lw["a_up"] = jnp.concatenate([z64, f("rwkv_a_up")], 0).astype(BF16)
    lw["a0"] = row("rwkv_a0")
    lw["g_up"] = f("rwkv_g_up").astype(BF16)
    lw["k_k"] = row("rwkv_k_k")
    lw["k_a"] = row("rwkv_k_a")
    lw["r_k"] = row("rwkv_r_k")
    lw["lnx_g"] = row("rwkv_lnx_g")
    lw["lnx_b"] = row("rwkv_lnx_b")
    lw["ones_bd"] = _ones_block_diag()
    ab_re, ab_im, bb_re, bb_im = _s5_params(f("ssm_a_re"), f("ssm_a_im"), f("ssm_log_dt"),
                                            f("ssm_b_re"), f("ssm_b_im"))
    gpc = SSM_GROUPS // SSM_CHUNKS
    chunked = lambda t: t.reshape(SSM_CHUNKS, gpc, t.shape[1], t.shape[2])
    lw["s5_bre"] = _block_diag(chunked(jnp.transpose(bb_re, (1, 0, 2)))).astype(BF16)
    lw["s5_bim"] = _block_diag(chunked(jnp.transpose(bb_im, (1, 0, 2)))).astype(BF16)
    lw["s5_cre"] = _block_diag(chunked(jnp.transpose(f("ssm_c_re"), (0, 2, 1)))).astype(BF16)
    lw["s5_cim"] = _block_diag(chunked(jnp.transpose(f("ssm_c_im"), (0, 2, 1)))).astype(BF16)
    lw["s5_d"] = row("ssm_d")
    lw["s5_are"] = ab_re.reshape(1, -1)
    lw["s5_aim"] = ab_im.reshape(1, -1)
    lw["glu_w"] = f("ssm_glu_w").astype(BF16)
    lw["glu_b"] = row("ssm_glu_b")
    for name in ("w_br_a", "w_br_b", "w_br_c", "w_out"):
        lw[name] = f(name).astype(BF16)
    for name in ("ln1_g", "ln1_b", "ln2_g", "ln2_b"):
        lw[name] = row(name)
    rw_ = jnp.pad(f("router_w"), ((0, 0), (0, LANES - N_EXPERTS)))
    hi = rw_.astype(BF16)
    lw["router_hi"] = hi
    lw["router_lo"] = (rw_ - hi.astype(F32)).astype(BF16)
    lw["router_b"] = jnp.pad(f("router_b"), (0, LANES - N_EXPERTS), constant_values=NEG_INF).reshape(1, -1)
    rows_l = N_EXPERTS * D_MODEL
    wgu = _matmul(P["exp_w_gu"].reshape(-1, 2 * D_FF), _deinterleave_matrix(), 1, rows_l, out_dtype=BF16,
                  row0=l * rows_l)
    lw["w_gu"] = wgu.reshape(N_EXPERTS, D_MODEL, 2 * D_FF)
    bgu = f("exp_b_gu")
    lw["b_gu"] = jnp.concatenate([bgu[..., 0::2], bgu[..., 1::2]], -1).reshape(N_EXPERTS, 1, 2 * D_FF)
    lw["w_down"] = P["exp_w_down"].reshape(-1, D_FF, D_MODEL)
    lw["w_down_e0"] = l * N_EXPERTS
    lw["b_down"] = f("exp_b_down").reshape(N_EXPERTS, 1, D_MODEL)
    return lw


def _rope_tables(pos):
    half = MLA_ROPE // 2
    inv = ROPE_THETA ** (-jnp.arange(half, dtype=F32) / half)
    ang = pos.astype(F32)[:, None] * inv
    cos, sin = jnp.cos(ang), jnp.sin(ang)
    n = pos.shape[0]
    tc = jnp.concatenate([cos, cos, jnp.ones((n, MLA_NOPE), F32),
                          jnp.zeros((n, HEAD_PAD - MLA_NOPE - MLA_ROPE), F32)], 1)
    ts = jnp.concatenate([sin, sin, jnp.zeros((n, HEAD_PAD - MLA_ROPE), F32)], 1)
    return tc, ts


def _to_lanes(t, nb, s, lanes):
    t = t.reshape(s, nb * RWKV_HEADS, RWKV_HEAD).transpose(0, 2, 1)
    return jnp.pad(t, ((0, 0), (0, 0), (0, lanes - nb * RWKV_HEADS)))


def _layer(x, lw, past, nb, s, tables, att_tk=LANES):
    T = nb * s
    tc, ts = tables
    if past is None:
        start = 0
        shift_p = jnp.zeros((nb, RWKV_PROJ), F32)
        wkv_p = jnp.zeros((nb, RWKV_HEADS, RWKV_HEAD, RWKV_HEAD), F32)
        sre_p = jnp.zeros((nb, SSM_CH), F32)
        sim_p = jnp.zeros((nb, SSM_CH), F32)
    else:
        ckv_p, kpe_p, shift_p, wkv_p, sre_p, sim_p = past
        start = ckv_p.shape[1]
        sre_p = sre_p.reshape(nb, SSM_CH)
        sim_p = sim_p.reshape(nb, SSM_CH)

    q, ckv, kpe128 = _mla_in(x, lw, tc, ts)
    if past is None:
        sk = s
        ckv_all, kpe_all = ckv, kpe128
    else:
        sk = start + s
        ckv_all = jnp.concatenate([ckv_p, ckv.reshape(nb, s, -1)], 1)
        kpe_new = kpe128.reshape(nb, s, LANES)
        kpe_all = jnp.concatenate([jnp.pad(kpe_p, ((0, 0), (0, 0), (0, LANES - MLA_ROPE))), kpe_new], 1)
    skp = -(-sk // att_tk) * att_tk
    if skp != sk:
        ckv_all = jnp.pad(ckv_all.reshape(nb, sk, -1), ((0, 0), (0, skp - sk), (0, 0)))
        kpe_all = jnp.pad(kpe_all.reshape(nb, sk, -1), ((0, 0), (0, skp - sk), (0, 0)))
    k_pad, v_all = _kv_expand(ckv_all.reshape(nb * skp, -1), kpe_all.reshape(nb * skp, -1), lw)
    ya = _attention(q, k_pad, v_all, nb, s, skp, sk, start)

    r, wdec, kn, v, kk, kb, g, bonus, last_row = _rwkv_pre(x, shift_p, lw, nb, s)
    lanes = -(-nb * RWKV_HEADS // LANES) * LANES
    seqs = [_to_lanes(t, nb, s, lanes) for t in (r, wdec, kn, v, kk, kb)]
    s0 = jnp.transpose(wkv_p, (3, 2, 0, 1)).reshape(RWKV_HEAD, RWKV_HEAD, nb * RWKV_HEADS)
    s0 = jnp.pad(s0, ((0, 0), (0, 0), (0, lanes - nb * RWKV_HEADS)))
    o_l, sT = _rwkv_scan(seqs, s0, s)
    o_tm = o_l[:, :, :nb * RWKV_HEADS].transpose(0, 2, 1).reshape(s, nb * RWKV_DIM)
    wkv_n = (sT[:, :, :nb * RWKV_HEADS].reshape(RWKV_HEAD, RWKV_HEAD, nb, RWKV_HEADS).transpose(2, 3, 1, 0))
    shift_n = last_row.reshape(nb, RWKV_PROJ)

    su_tm = _matmul(x, lw["w_su"], nb, s, time_major_out=True).reshape(s * nb, SSM_DIM)
    yc_tm, sre_n, sim_n = _s5(su_tm, sre_p, sim_p, lw, nb, s)
    yc_tm = yc_tm.reshape(s, nb * SSM_DIM)

    x1 = _merge(x, ya, o_tm, bonus, g, yc_tm, lw, nb, s)
    x2 = _moe(x1, lw)
    new = (ckv.reshape(nb, s, MLA_KV_RANK), kpe128[:, :MLA_ROPE].reshape(nb, s, MLA_ROPE), shift_n, wkv_n,
           sre_n.reshape(nb, SSM_GROUPS, SSM_STATE), sim_n.reshape(nb, SSM_GROUPS, SSM_STATE))
    return x2, new


def _trunk(x3, weights, caches):
    nb, s, d = x3.shape
    start = 0 if caches is None else caches[0].shape[2]
    tc, ts = _rope_tables(start + jnp.arange(s))
    tables = (jnp.tile(tc, (nb, 1)), jnp.tile(ts, (nb, 1)))
    x = x3.reshape(nb * s, d)
    new = []
    for l in range(len(weights)):
        past = None if caches is None else tuple(c[l] for c in caches)
        x, st = _layer(x, weights[l], past, nb, s, tables)
        new.append(st)
    return (x.reshape(nb, s, d),) + tuple(jnp.stack([st[j] for st in new]) for j in range(6))


def kernel(x_prompt, x_sample, cache_mla_ckv, cache_mla_kpe, state_rwkv_shift, state_rwkv_wkv, state_ssm_re, state_ssm_im, w_in, mla_q_a_norm, mla_w_q_b, mla_kv_a_norm, mla_w_kv_b, rwkv_mu, rwkv_w0, rwkv_w_up, rwkv_a0, rwkv_a_up, rwkv_g_up, rwkv_k_k, rwkv_k_a, rwkv_r_k, rwkv_lnx_g, rwkv_lnx_b, ssm_a_re, ssm_a_im, ssm_b_re, ssm_b_im, ssm_c_re, ssm_c_im, ssm_d, ssm_log_dt, ssm_glu_w, ssm_glu_b, w_br_a, w_br_b, w_br_c, w_out, ln1_g, ln1_b, router_w, router_b, exp_w_gu, exp_b_gu, exp_w_down, exp_b_down, ln2_g, ln2_b):
    P = dict(w_in=w_in, mla_q_a_norm=mla_q_a_norm, mla_w_q_b=mla_w_q_b, mla_kv_a_norm=mla_kv_a_norm,
             mla_w_kv_b=mla_w_kv_b, rwkv_mu=rwkv_mu, rwkv_w0=rwkv_w0, rwkv_w_up=rwkv_w_up, rwkv_a0=rwkv_a0,
             rwkv_a_up=rwkv_a_up, rwkv_g_up=rwkv_g_up, rwkv_k_k=rwkv_k_k, rwkv_k_a=rwkv_k_a, rwkv_r_k=rwkv_r_k,
             rwkv_lnx_g=rwkv_lnx_g, rwkv_lnx_b=rwkv_lnx_b, ssm_a_re=ssm_a_re, ssm_a_im=ssm_a_im,
             ssm_b_re=ssm_b_re, ssm_b_im=ssm_b_im, ssm_c_re=ssm_c_re, ssm_c_im=ssm_c_im, ssm_d=ssm_d,
             ssm_log_dt=ssm_log_dt, ssm_glu_w=ssm_glu_w, ssm_glu_b=ssm_glu_b, w_br_a=w_br_a, w_br_b=w_br_b,
             w_br_c=w_br_c, w_out=w_out, ln1_g=ln1_g, ln1_b=ln1_b, router_w=router_w, router_b=router_b,
             exp_w_gu=exp_w_gu, exp_b_gu=exp_b_gu, exp_w_down=exp_w_down, exp_b_down=exp_b_down,
             ln2_g=ln2_g, ln2_b=ln2_b)
    depth = w_in.shape[0]
    weights = [_prep_layer(P, l) for l in range(depth)]
    outs_p = _trunk(x_prompt, weights, None)
    caches = (cache_mla_ckv, cache_mla_kpe, state_rwkv_shift, state_rwkv_wkv, state_ssm_re, state_ssm_im)
    outs_s = _trunk(x_sample, weights, caches)
    return (outs_p[0], outs_s[0]) + outs_p[1:] + outs_s[1:]
```

```python
import functools
import math

import numpy as np
import jax
import jax.numpy as jnp
from jax import lax
from jax.experimental import pallas as pl
from jax.experimental.pallas import tpu as pltpu

F32 = jnp.float32
BF16 = jnp.bfloat16

D_MODEL = 1024
CHUNK = 64
MLA_HEADS = 8
MLA_NOPE = 64
MLA_ROPE = 32
MLA_V = 64
MLA_Q_RANK = 384
MLA_KV_RANK = 256
ROPE_THETA = 10000.0
HEAD_PAD = 128
RWKV_HEADS = 8
RWKV_HEAD = 64
RWKV_DIM = RWKV_HEADS * RWKV_HEAD
RWKV_PROJ = 3 * RWKV_DIM + 64 + 64 + 128
RWKV_LN_EPS = 64e-5
RWKV_BLOCK = 16
SSM_DIM = 512
SSM_GROUP = 16
SSM_GROUPS = 32
SSM_STATE = 64
SSM_CHUNKS = 4
SSM_CH = SSM_GROUPS * SSM_STATE
IN_SIZES = (MLA_Q_RANK, MLA_KV_RANK + MLA_ROPE, RWKV_PROJ, SSM_DIM, D_MODEL, D_MODEL, D_MODEL)
N_EXPERTS = 32
TOP_K = 4
D_FF = 512
SWIGLU_LIMIT = 7.0
SWIGLU_ALPHA = 1.702
DEPTH = 2
DN_ALPHA = (2 * DEPTH) ** 0.25
LN_EPS = 1e-5
RMS_EPS = 1e-6
NEG_INF = -1e30
ATT_SCALE = (MLA_NOPE + MLA_ROPE) ** -0.5
LANES = 128
VMEM_LIMIT = 48 * 1024 * 1024


def _cparams(*sem):
    return pltpu.CompilerParams(dimension_semantics=sem, vmem_limit_bytes=VMEM_LIMIT)


def _dot(a, b):
    return jnp.dot(a, b, preferred_element_type=F32)


def _sigmoid(x):
    return 1.0 / (1.0 + jnp.exp(-x))


def _layer_norm(x, g, b):
    mu = jnp.mean(x, -1, keepdims=True)
    xc = x - mu
    var = jnp.mean(xc * xc, -1, keepdims=True)
    return xc * lax.rsqrt(var + LN_EPS) * g + b


def _full(shape):
    n = len(shape)
    return pl.BlockSpec(shape, lambda *_: (0,) * n)


def _mm_kernel(x_ref, w_ref, o_ref):
    o_ref[...] = _dot(x_ref[...].astype(BF16), w_ref[...]).astype(o_ref.dtype)


def _matmul(x, w, nb, s, *, time_major_out=False, tm=512, tn=1024, out_dtype=F32, row0=0):
    K = x.shape[1]
    N = w.shape[1]
    tm = min(tm, s)
    tn = min(tn, N)
    while N % tn:
        tn -= LANES
    nt = s // tm
    blk0 = row0 // tm
    if time_major_out:
        assert tn == N
        out_shape = jax.ShapeDtypeStruct((s, nb * N), out_dtype)
        out_spec = pl.BlockSpec((tm, N), lambda b, i, j: (i, b))
    else:
        out_shape = jax.ShapeDtypeStruct((nb * s, N), out_dtype)
        out_spec = pl.BlockSpec((tm, tn), lambda b, i, j: (b * nt + i, j))
    return pl.pallas_call(
        _mm_kernel,
        grid=(nb, nt, N // tn),
        in_specs=[pl.BlockSpec((tm, K), lambda b, i, j: (blk0 + b * nt + i, 0)),
                  pl.BlockSpec((K, tn), lambda b, i, j: (0, j))],
        out_specs=out_spec,
        out_shape=out_shape,
        compiler_params=_cparams("parallel", "parallel", "arbitrary"),
        name="matmul",
    )(x, w)


def _mla_in_kernel(x_ref, w_ref, qg_ref, kvg_ref, wqb_ref, tc_ref, ts_ref, q_ref, ckv_ref, kpe_ref):
    h = _dot(x_ref[...].astype(BF16), w_ref[...])
    tc = tc_ref[...]
    ts = ts_ref[...]
    qa = h[:, :MLA_Q_RANK]
    qn = qa * lax.rsqrt(jnp.mean(qa * qa, -1, keepdims=True) + RMS_EPS) * qg_ref[...]
    q2 = _dot(qn.astype(BF16), wqb_ref[...])
    hp = MLA_HEADS * HEAD_PAD
    for hd in range(MLA_HEADS):
        lo = hd * HEAD_PAD
        q0 = q2[:, lo:lo + HEAD_PAD]
        q1 = q2[:, hp + lo:hp + lo + HEAD_PAD]
        q_ref[:, lo:lo + HEAD_PAD] = ((q0 * tc + q1 * ts) * ATT_SCALE).astype(BF16)
    c0 = MLA_Q_RANK
    ckv = h[:, c0:c0 + MLA_KV_RANK]
    ckv_ref[...] = ckv * lax.rsqrt(jnp.mean(ckv * ckv, -1, keepdims=True) + RMS_EPS) * kvg_ref[...]
    c1 = c0 + MLA_KV_RANK
    kpe_ref[...] = h[:, c1:c1 + LANES] * tc + h[:, c1 + LANES:c1 + 2 * LANES] * ts


def _mla_in(x, lw, tc, ts, *, tm=512):
    T = x.shape[0]
    tm = min(tm, T)
    hp = MLA_HEADS * HEAD_PAD
    nw = lw["w_mla"].shape[1]
    return pl.pallas_call(
        _mla_in_kernel,
        grid=(T // tm,),
        in_specs=[pl.BlockSpec((tm, D_MODEL), lambda i: (i, 0)),
                  _full((D_MODEL, nw)), _full((1, MLA_Q_RANK)), _full((1, MLA_KV_RANK)),
                  _full((MLA_Q_RANK, 2 * hp)),
                  pl.BlockSpec((tm, LANES), lambda i: (i, 0)),
                  pl.BlockSpec((tm, LANES), lambda i: (i, 0))],
        out_specs=[pl.BlockSpec((tm, hp), lambda i: (i, 0)),
                   pl.BlockSpec((tm, MLA_KV_RANK), lambda i: (i, 0)),
                   pl.BlockSpec((tm, LANES), lambda i: (i, 0))],
        out_shape=[jax.ShapeDtypeStruct((T, hp), BF16),
                   jax.ShapeDtypeStruct((T, MLA_KV_RANK), F32),
                   jax.ShapeDtypeStruct((T, LANES), F32)],
        compiler_params=_cparams("parallel"),
        name="mla_in",
    )(x, lw["w_mla"], lw["q_norm"], lw["kv_norm"], lw["w_qb"], tc, ts)


def _kv_expand_kernel(ckv_ref, kpe_ref, wk_ref, wv_ref, k_ref, v_ref):
    c = ckv_ref[...].astype(BF16)
    k = _dot(c, wk_ref[...])
    kpe = kpe_ref[...]
    for hd in range(MLA_HEADS):
        lo = hd * HEAD_PAD
        k_ref[:, lo:lo + HEAD_PAD] = (k[:, lo:lo + HEAD_PAD] + kpe).astype(BF16)
    v_ref[...] = _dot(c, wv_ref[...]).astype(BF16)


def _kv_expand(ckv, kpe, lw, *, tm=512):
    T = ckv.shape[0]
    tm = min(tm, T)
    while T % tm:
        tm //= 2
    hp = MLA_HEADS * HEAD_PAD
    hv = MLA_HEADS * MLA_V
    return pl.pallas_call(
        _kv_expand_kernel,
        grid=(T // tm,),
        in_specs=[pl.BlockSpec((tm, MLA_KV_RANK), lambda i: (i, 0)),
                  pl.BlockSpec((tm, LANES), lambda i: (i, 0)),
                  _full((MLA_KV_RANK, hp)), _full((MLA_KV_RANK, hv))],
        out_specs=[pl.BlockSpec((tm, hp), lambda i: (i, 0)),
                   pl.BlockSpec((tm, hv), lambda i: (i, 0))],
        out_shape=[jax.ShapeDtypeStruct((T, hp), BF16), jax.ShapeDtypeStruct((T, hv), BF16)],
        compiler_params=_cparams("parallel"),
        name="kv_expand",
    )(ckv, kpe, lw["w_k"], lw["w_v"])


def _attn_kernel(q_ref, k_ref, v_ref, o_ref, *, tq, nq, q_start, sk):
    lane = lax.broadcasted_iota(jnp.int32, (tq, LANES), 1)
    nt = (((1,), (1,)), ((), ()))
    for i in range(nq):
        q_lo = q_start + i * tq
        k_end = min(((q_lo + tq - 1) // CHUNK + 1) * CHUNK, sk)
        kw = -(-k_end // LANES) * LANES
        c0 = min((q_lo // CHUNK + 1) * CHUNK, sk) // LANES * LANES
        tail = kw - c0
        if tail:
            q_chunk = (q_lo + lax.broadcasted_iota(jnp.int32, (tq, tail), 0)) // CHUNK
            k_pos = c0 + lax.broadcasted_iota(jnp.int32, (tq, tail), 1)
            visible = k_pos // CHUNK <= q_chunk
            if kw > sk:
                visible = visible & (k_pos < sk)
        outs = []
        for hh in range(2):
            hs = slice(hh * HEAD_PAD, (hh + 1) * HEAD_PAD)
            q = q_ref[i * tq:(i + 1) * tq, hs]
            parts = []
            if c0:
                parts.append(lax.dot_general(q, k_ref[0:c0, hs], nt, preferred_element_type=F32))
            if tail:
                s_t = lax.dot_general(q, k_ref[c0:kw, hs], nt, preferred_element_type=F32)
                parts.append(jnp.where(visible, s_t, NEG_INF))
            m = jnp.max(parts[0], -1, keepdims=True)
            for s_ in parts[1:]:
                m = jnp.maximum(m, jnp.max(s_, -1, keepdims=True))
            l = None
            acc = None
            lo = 0
            for s_ in parts:
                p = jnp.exp(s_ - m)
                ps = jnp.sum(p, -1, keepdims=True)
                pv = _dot(p.astype(BF16), v_ref[lo:lo + s_.shape[1], :])
                l = ps if l is None else l + ps
                acc = pv if acc is None else acc + pv
                lo += s_.shape[1]
            outs.append(acc / l)
        o_ref[i * tq:(i + 1) * tq, :] = jnp.where(lane < MLA_V, outs[0], outs[1])


def _attention(q, k, v, nb, sq, skp, sk, q_start, *, tq=512):
    tq = min(tq, sq)
    kern = functools.partial(_attn_kernel, tq=tq, nq=sq // tq, q_start=q_start, sk=sk)
    return pl.pallas_call(
        kern,
        grid=(nb, MLA_HEADS // 2),
        in_specs=[pl.BlockSpec((sq, 2 * HEAD_PAD), lambda b, j: (b, j)),
                  pl.BlockSpec((skp, 2 * HEAD_PAD), lambda b, j: (b, j)),
                  pl.BlockSpec((skp, 2 * MLA_V), lambda b, j: (b, j))],
        out_specs=pl.BlockSpec((sq, 2 * MLA_V), lambda b, j: (b, j)),
        out_shape=jax.ShapeDtypeStruct((nb * sq, MLA_HEADS * MLA_V), F32),
        compiler_params=_cparams("parallel", "parallel"),
        name="attention",
    )(q, k, v)


def _head_sum(z, ones_bd):
    hi = z.astype(BF16)
    lo = (z - hi.astype(F32)).astype(BF16)
    return _dot(hi, ones_bd) + _dot(lo, ones_bd)


def _rwkv_pre_kernel(x_ref, xprev_ref, shift_ref, wrw_ref, mu_ref, w0_ref, wup_ref, a0_ref, aup_ref, gup_ref,
                     kk_ref, ka_ref, rk_ref, ones_ref, blk_ref,
                     r_out, w_out, k_out, v_out, kk_out, b_out, g_out, bonus_out, last_out):
    p = _dot(x_ref[...].astype(BF16), wrw_ref[...])
    first = pl.program_id(1) == 0
    p_before = _dot(xprev_ref[...].astype(BF16), wrw_ref[...])
    prev_row = jnp.where(first, shift_ref[0], p_before[7:8, :])
    last_out[0] = p[p.shape[0] - 1:, :]
    row = lax.broadcasted_iota(jnp.int32, p.shape, 0)
    prev = jnp.where(row == 0, prev_row, pltpu.roll(p, 1, 0))
    ps = p + (prev - p) * mu_ref[...]
    d = RWKV_DIM
    r = ps[:, 0:d]
    k = ps[:, d:2 * d]
    v = ps[:, 2 * d:3 * d]
    wa = ps[:, 3 * d:3 * d + LANES]
    gd = ps[:, 3 * d + LANES:3 * d + 2 * LANES]
    ones_bd = ones_ref[...]
    wlin = w0_ref[...] + _dot(jnp.tanh(wa).astype(BF16), wup_ref[...])
    z = -wlin
    w = -(jnp.maximum(z, 0.0) + jnp.log1p(jnp.exp(-jnp.abs(z)))) - 0.5
    log_decay = -jnp.exp(w)
    blk = blk_ref[...]
    rows = blk.shape[0]
    pieces, rest = [], log_decay
    for _ in range(3):
        pieces.append(rest.astype(BF16))
        rest = rest - pieces[-1].astype(F32)
    cum = jnp.concatenate(
        [sum(_dot(blk, pc[g * rows:(g + 1) * rows]) for pc in pieces) for g in range(p.shape[0] // rows)], 0)
    p_incl = jnp.exp(cum)
    p_inv = jnp.exp(-cum)
    w_out[...] = p_incl
    a = _sigmoid(a0_ref[...] + _dot(wa.astype(BF16), aup_ref[...]))
    g_out[...] = _dot(_sigmoid(gd).astype(BF16), gup_ref[...])
    kk = k * kk_ref[...]
    kkn = kk * lax.rsqrt(_head_sum(kk * kk, ones_bd) + 1e-12)
    kk_out[...] = kkn * jnp.exp(cum - log_decay)
    b_out[...] = kkn * a * p_inv
    kn = k * (1.0 + (a - 1.0) * ka_ref[...])
    r_out[...] = r * p_incl
    k_out[...] = kn * p_inv
    v_out[...] = v
    bonus_out[...] = _head_sum(r * kn * rk_ref[...], ones_bd) * v


def _rwkv_pre(x, shift_prev, lw, nb, s, *, tm=512):
    tm = min(tm, s)
    tps = s // tm
    d = RWKV_DIM
    vec = lambda n: _full((1, n))
    bm = pl.BlockSpec((tm, d), lambda b, i: (b * tps + i, 0))
    tmaj = pl.BlockSpec((tm, d), lambda b, i: (i, b))
    per_seq = pl.BlockSpec((1, 1, RWKV_PROJ), lambda b, i: (b, 0, 0))
    bm_shape = jax.ShapeDtypeStruct((nb * s, d), F32)
    tm_shape = jax.ShapeDtypeStruct((s, nb * d), F32)
    br = min(tm, LANES)
    assert tm % br == 0 and br % RWKV_BLOCK == 0
    idx = np.arange(br)
    blk = (idx[:, None] // RWKV_BLOCK == idx[None, :] // RWKV_BLOCK) & (idx[None, :] <= idx[:, None])
    blk = jnp.asarray(blk.astype(np.float32), dtype=BF16)
    return pl.pallas_call(
        _rwkv_pre_kernel,
        grid=(nb, tps),
        in_specs=[pl.BlockSpec((tm, D_MODEL), lambda b, i: (b * tps + i, 0)),
                  pl.BlockSpec((8, D_MODEL), lambda b, i: (jnp.maximum((b * tps + i) * (tm // 8) - 1, 0), 0)),
                  per_seq, _full((D_MODEL, RWKV_PROJ)),
                  vec(RWKV_PROJ), vec(d), _full((LANES, d)), vec(d), _full((LANES, d)),
                  _full((LANES, d)), vec(d), vec(d), vec(d), _full((d, d)), _full((br, br))],
        out_specs=[tmaj] * 6 + [bm] * 2 + [per_seq],
        out_shape=[tm_shape] * 6 + [bm_shape] * 2 + [jax.ShapeDtypeStruct((nb, 1, RWKV_PROJ), F32)],
        compiler_params=_cparams("parallel", "arbitrary"),
        name="rwkv_pre",
    )(x, x, shift_prev.reshape(nb, 1, RWKV_PROJ), lw["w_rw"], lw["mu"], lw["w0"], lw["w_up"], lw["a0"],
      lw["a_up"], lw["g_up"], lw["k_k"], lw["k_a"], lw["r_k"], lw["ones_bd"], blk)


RWKV_VC = 32


def _rwkv_scan_kernel(r_ref, p_ref, k_ref, v_ref, kk_ref, b_ref, s0_ref, o_ref, sT_ref, st_ref, *, tt):
    n = RWKV_HEAD

    @pl.when(pl.program_id(0) == 0)
    def _():
        st_ref[...] = s0_ref[...]

    def step(t, carry):
        for c in range(n // RWKV_VC):
            vs = slice(c * RWKV_VC, (c + 1) * RWKV_VC)
            parts = [None] * 4
            for q in range(n):
                term = st_ref[q, vs, :] * kk_ref[t, q:q + 1, :]
                parts[q % 4] = term if parts[q % 4] is None else parts[q % 4] + term
            sa = -((parts[0] + parts[1]) + (parts[2] + parts[3]))
            vt = v_ref[t, vs, :]
            outs = [None] * 4
            for q in range(n):
                s_new = st_ref[q, vs, :] + (sa * b_ref[t, q:q + 1, :] + vt * k_ref[t, q:q + 1, :])
                st_ref[q, vs, :] = s_new
                term = s_new * r_ref[t, q:q + 1, :]
                outs[q % 4] = term if outs[q % 4] is None else outs[q % 4] + term
            o_ref[t, vs, :] = (outs[0] + outs[1]) + (outs[2] + outs[3])
        return carry

    lax.fori_loop(0, tt, step, 0)
    for q in range(n):
        st_ref[q] = st_ref[q] * p_ref[tt - 1, q:q + 1, :]

    @pl.when(pl.program_id(0) == pl.num_programs(0) - 1)
    def _():
        sT_ref[...] = st_ref[...]


def _rwkv_scan(seqs, s0, s):
    L = s0.shape[-1]
    n = RWKV_HEAD
    tt = RWKV_BLOCK
    assert s % tt == 0
    blk = pl.BlockSpec((tt, n, L), lambda i: (i, 0, 0))
    kern = functools.partial(_rwkv_scan_kernel, tt=tt)
    return pl.pallas_call(
        kern,
        grid=(s // tt,),
        in_specs=[blk] * 6 + [_full((n, n, L))],
        out_specs=[blk, _full((n, n, L))],
        out_shape=[jax.ShapeDtypeStruct((s, n, L), F32), jax.ShapeDtypeStruct((n, n, L), F32)],
        scratch_shapes=[pltpu.VMEM((n, n, L), F32)],
        compiler_params=_cparams("arbitrary"),
        name="rwkv_scan",
    )(*seqs, s0)


def _s5_params_kernel(are_ref, aim_ref, ldt_ref, bre_ref, bim_ref, abre_ref, abim_ref, bbre_ref, bbim_ref):
    lr = are_ref[...]
    li = aim_ref[...]
    dt = jnp.exp(ldt_ref[...])
    mag = jnp.exp(lr * dt)
    ab_re = mag * jnp.cos(li * dt)
    ab_im = mag * jnp.sin(li * dt)
    den = lr * lr + li * li
    f_re = ((ab_re - 1.0) * lr + ab_im * li) / den
    f_im = (ab_im * lr - (ab_re - 1.0) * li) / den
    abre_ref[...] = ab_re
    abim_ref[...] = ab_im
    for i in range(SSM_GROUP):
        br = bre_ref[i]
        bi = bim_ref[i]
        bbre_ref[i] = f_re * br - f_im * bi
        bbim_ref[i] = f_re * bi + f_im * br


def _s5_params(a_re, a_im, log_dt, b_re, b_im):
    g, n = a_re.shape
    gn = jax.ShapeDtypeStruct((g, n), F32)
    ign = jax.ShapeDtypeStruct((SSM_GROUP, g, n), F32)
    return pl.pallas_call(
        _s5_params_kernel,
        out_shape=[gn, gn, ign, ign],
        name="s5_params",
    )(a_re, a_im, log_dt.reshape(g, 1), jnp.transpose(b_re, (2, 0, 1)), jnp.transpose(b_im, (2, 0, 1)))


def _gelu_tanh(x):
    return 0.5 * x * (1.0 + jnp.tanh(math.sqrt(2.0 / math.pi) * (x + 0.044715 * (x * x * x))))


def _s5_kernel(u_ref, bre_ref, bim_ref, cre_ref, cim_ref, d_ref, are_ref, aim_ref, h0re_ref, h0im_ref,
               gw_ref, gb_ref, y_ref, hTre_ref, hTim_ref, hre_s, him_s, sre_s, sim_s, yy_s, *, tt, nb):
    cw = SSM_CH // SSM_CHUNKS

    @pl.when(pl.program_id(0) == 0)
    def _():
        hre_s[...] = h0re_ref[...]
        him_s[...] = h0im_ref[...]

    for c in range(SSM_CHUNKS):
        uc = u_ref[:, c * LANES:(c + 1) * LANES]
        ub = uc.astype(BF16)
        sre_s[...] = _dot(ub, bre_ref[c])
        sim_s[...] = _dot(ub, bim_ref[c])
        a_re = jnp.broadcast_to(are_ref[:, c * cw:(c + 1) * cw], (nb, cw))
        a_im = jnp.broadcast_to(aim_ref[:, c * cw:(c + 1) * cw], (nb, cw))

        def step(t, carry, a_re=a_re, a_im=a_im):
            hr, hi = carry
            rows = pl.ds(pl.multiple_of(t * nb, nb), nb)
            nr = a_re * hr - a_im * hi + sre_s[rows, :]
            ni = a_re * hi + a_im * hr + sim_s[rows, :]
            sre_s[rows, :] = nr
            sim_s[rows, :] = ni
            return nr, ni

        hr, hi = lax.fori_loop(0, tt, step, (hre_s[:, c * cw:(c + 1) * cw], him_s[:, c * cw:(c + 1) * cw]))
        hre_s[:, c * cw:(c + 1) * cw] = hr
        him_s[:, c * cw:(c + 1) * cw] = hi
        yc = _dot(sre_s[...].astype(BF16), cre_ref[c]) - _dot(sim_s[...].astype(BF16), cim_ref[c])
        yy_s[:, c * LANES:(c + 1) * LANES] = yc + d_ref[:, c * LANES:(c + 1) * LANES] * uc

    y = _gelu_tanh(yy_s[...])
    z = _dot(y.astype(BF16), gw_ref[...]) + gb_ref[...]
    y_ref[...] = z[:, :SSM_DIM] * _sigmoid(z[:, SSM_DIM:])

    @pl.when(pl.program_id(0) == pl.num_programs(0) - 1)
    def _():
        hTre_ref[...] = hre_s[...]
        hTim_ref[...] = him_s[...]


def _s5(u_tm, h0_re, h0_im, lw, nb, s, *, rows=1024):
    tt = max(min(rows // nb, s), 1)
    R = tt * nb
    cw = SSM_CH // SSM_CHUNKS
    kern = functools.partial(_s5_kernel, tt=tt, nb=nb)
    return pl.pallas_call(
        kern,
        grid=(s // tt,),
        in_specs=[pl.BlockSpec((R, SSM_DIM), lambda i: (i, 0)),
                  _full((SSM_CHUNKS, LANES, cw)), _full((SSM_CHUNKS, LANES, cw)),
                  _full((SSM_CHUNKS, cw, LANES)), _full((SSM_CHUNKS, cw, LANES)),
                  _full((1, SSM_DIM)), _full((1, SSM_CH)), _full((1, SSM_CH)),
                  _full((nb, SSM_CH)), _full((nb, SSM_CH)),
                  _full((SSM_DIM, 2 * SSM_DIM)), _full((1, 2 * SSM_DIM))],
        out_specs=[pl.BlockSpec((R, SSM_DIM), lambda i: (i, 0)), _full((nb, SSM_CH)), _full((nb, SSM_CH))],
        out_shape=[jax.ShapeDtypeStruct((s * nb, SSM_DIM), F32),
                   jax.ShapeDtypeStruct((nb, SSM_CH), F32), jax.ShapeDtypeStruct((nb, SSM_CH), F32)],
        scratch_shapes=[pltpu.VMEM((nb, SSM_CH), F32), pltpu.VMEM((nb, SSM_CH), F32),
                        pltpu.VMEM((R, cw), F32), pltpu.VMEM((R, cw), F32), pltpu.VMEM((R, SSM_DIM), F32)],
        compiler_params=_cparams("arbitrary"),
        name="s5",
    )(u_tm, lw["s5_bre"], lw["s5_bim"], lw["s5_cre"], lw["s5_cim"], lw["s5_d"], lw["s5_are"], lw["s5_aim"],
      h0_re, h0_im, lw["glu_w"], lw["glu_b"])


def _merge_kernel(x_ref, ya_ref, o_ref, bonus_ref, g_ref, yc_ref, wg_ref, lng_ref, lnb_ref, ones_ref,
                  wa_ref, wb_ref, wc_ref, wo_ref, ln1g_ref, ln1b_ref, out_ref):
    ones_bd = ones_ref[...]
    o = o_ref[...]
    inv_n = 1.0 / RWKV_HEAD
    mean = _head_sum(o, ones_bd) * inv_n
    oc = o - mean
    var = _head_sum(oc * oc, ones_bd) * inv_n
    yb = (oc * lax.rsqrt(var + RWKV_LN_EPS) * lng_ref[...] + lnb_ref[...] + bonus_ref[...]) * g_ref[...]
    d = D_MODEL
    x = x_ref[...]
    xb = x.astype(BF16)
    merged = None
    for j, (y, w_ref) in enumerate(((ya_ref[...], wa_ref), (yb, wb_ref), (yc_ref[...], wc_ref))):
        gate = _sigmoid(_dot(xb, wg_ref[:, j * d:(j + 1) * d]))
        term = gate * _dot(y.astype(BF16), w_ref[...])
        merged = term if merged is None else merged + term
    y = DN_ALPHA * x + _dot(merged.astype(BF16), wo_ref[...])
    out_ref[...] = _layer_norm(y, ln1g_ref[...], ln1b_ref[...])


def _merge(x, ya, o_tm, bonus, g, yc_tm, lw, nb, s, *, tm=512):
    tm = min(tm, s)
    nt = s // tm
    d = D_MODEL
    h = RWKV_DIM
    row = lambda w: pl.BlockSpec((tm, w), lambda b, i: (b * nt + i, 0))
    tmaj = lambda w: pl.BlockSpec((tm, w), lambda b, i: (i, b))
    vec = lambda n: _full((1, n))
    once = lambda shape: pl.BlockSpec(shape, lambda b, i: (0, 0), pipeline_mode=pl.Buffered(1))
    return pl.pallas_call(
        _merge_kernel,
        grid=(nb, nt),
        in_specs=[row(d), row(h), tmaj(h), row(h), row(h), tmaj(SSM_DIM),
                  once((d, 3 * d)), vec(h), vec(h), once((h, h)),
                  once((h, d)), once((h, d)), once((h, d)), once((d, d)), vec(d), vec(d)],
        out_specs=row(d),
        out_shape=jax.ShapeDtypeStruct((nb * s, d), F32),
        compiler_params=_cparams("parallel", "parallel"),
        name="merge",
    )(x, ya, o_tm, bonus, g, yc_tm, lw["w_gates"], lw["lnx_g"], lw["lnx_b"], lw["ones_bd"],
      lw["w_br_a"], lw["w_br_b"], lw["w_br_c"], lw["w_out"], lw["ln1_g"], lw["ln1_b"])


INFO_GATE, INFO_EID, INFO_RANK = 0, 4, 8


def _router_kernel(x_ref, whi_ref, wlo_ref, b_ref, tri_ref, info_ref, cnt_ref, base_s):
    @pl.when(pl.program_id(0) == 0)
    def _():
        base_s[...] = jnp.zeros_like(base_s)

    x = x_ref[...]
    xh = x.astype(BF16)
    xl = (x - xh.astype(F32)).astype(BF16)
    logits = _dot(xh, whi_ref[...]) + _dot(xl, whi_ref[...]) + _dot(xh, wlo_ref[...]) + b_ref[...]
    lane_i = lax.broadcasted_iota(jnp.int32, logits.shape, 1)
    lane = lane_i.astype(F32)
    vals, sels, ids = [], [], []
    for _ in range(TOP_K):
        m = jnp.max(logits, -1, keepdims=True)
        idx = jnp.min(jnp.where(logits == m, lane, float(LANES)), -1, keepdims=True)
        sel = lane == idx
        vals.append(m)
        sels.append(sel)
        ids.append(idx)
        logits = jnp.where(sel, -3e38, logits)
    es = [jnp.exp(v - vals[0]) for v in vals]
    den = es[0] + es[1] + es[2] + es[3]
    chosen = jnp.zeros_like(logits)
    for sel in sels:
        chosen = chosen + jnp.where(sel, 1.0, 0.0)
    before = _dot(tri_ref[...], chosen.astype(BF16)) + base_s[...]
    info = jnp.zeros_like(logits)
    for k in range(TOP_K):
        rank = jnp.sum(jnp.where(sels[k], before, 0.0), -1, keepdims=True)
        info = (info + jnp.where(lane_i == INFO_GATE + k, es[k] / den, 0.0)
                + jnp.where(lane_i == INFO_EID + k, ids[k], 0.0)
                + jnp.where(lane_i == INFO_RANK + k, rank, 0.0))
    info_ref[...] = info
    base_s[...] += jnp.sum(chosen, 0, keepdims=True)
    cnt_ref[...] = base_s[...]


def _router(x, lw, *, tm=512):
    T = x.shape[0]
    tm = min(tm, T)
    tri = jnp.asarray(np.tril(np.ones((tm, tm), np.float32), -1), dtype=BF16)
    return pl.pallas_call(
        _router_kernel,
        grid=(T // tm,),
        in_specs=[pl.BlockSpec((tm, D_MODEL), lambda i: (i, 0)),
                  _full((D_MODEL, LANES)), _full((D_MODEL, LANES)), _full((1, LANES)), _full((tm, tm))],
        out_specs=[pl.BlockSpec((tm, LANES), lambda i: (i, 0)), _full((1, LANES))],
        out_shape=[jax.ShapeDtypeStruct((T, LANES), F32), jax.ShapeDtypeStruct((1, LANES), F32)],
        scratch_shapes=[pltpu.VMEM((1, LANES), F32)],
        compiler_params=_cparams("arbitrary"),
        name="router",
    )(x, lw["router_hi"], lw["router_lo"], lw["router_b"], tri)


def _moe_schedule(info, cnt, tm_e, nt):
    eid = info[:, INFO_EID:INFO_EID + TOP_K].astype(jnp.int32)
    rank = info[:, INFO_RANK:INFO_RANK + TOP_K].astype(jnp.int32)
    counts = cnt[0, :N_EXPERTS].astype(jnp.int32)
    gsz = (counts + tm_e - 1) // tm_e * tm_e
    gend = jnp.cumsum(gsz)
    goff = gend - gsz
    onehot = eid[..., None] == jnp.arange(N_EXPERTS, dtype=jnp.int32)
    pos = jnp.sum(jnp.where(onehot, goff, 0), -1) + rank
    tile_start = jnp.arange(nt, dtype=jnp.int32) * tm_e
    tile_e = jnp.minimum(jnp.sum((gend[None, :] <= tile_start[:, None]).astype(jnp.int32), -1), N_EXPERTS - 1)
    n_valid = (gend[-1] // tm_e).reshape(1)
    i32 = lambda a: a.astype(jnp.int32)
    return i32(pos.reshape(-1)), i32(tile_e), i32(n_valid), i32(gend), i32(gsz)


def _dispatch_kernel(gend_ref, gsz_ref, nv_ref, pos_ref, x_ref, xs_ref, zbuf, sem, zsem, *, tm, tm_e, nt):
    @pl.when(pl.program_id(0) == 0)
    def _():
        zbuf[...] = jnp.zeros_like(zbuf)
        fill = lambda row0: pltpu.make_async_copy(zbuf, xs_ref.at[pl.ds(pl.multiple_of(row0, tm_e), tm_e)], zsem)
        for phase in ("start", "wait"):
            for e in range(N_EXPERTS):
                @pl.when(gsz_ref[e] > 0)
                def _(e=e, phase=phase):
                    getattr(fill(gend_ref[e] - tm_e), phase)()

            def tail(j, carry, phase=phase):
                getattr(fill(j * tm_e), phase)()
                return carry

            lax.fori_loop(nv_ref[0], nt, tail, 0)

    def issue(t, carry):
        for k in range(TOP_K):
            p = pos_ref[t * TOP_K + k]
            pltpu.make_async_copy(x_ref.at[pl.ds(t, 1)], xs_ref.at[pl.ds(p, 1)], sem).start(priority=k % 2)
        return carry

    lax.fori_loop(0, tm, issue, 0, unroll=8)
    for k in range(TOP_K):
        pltpu.make_async_copy(x_ref, xs_ref.at[pl.ds(0, tm)], sem).wait()


def _dispatch(x, pos, gend, gsz, n_valid, tm_e, nt, *, tm=512):
    T, d = x.shape
    tm = min(tm, T)
    kern = functools.partial(_dispatch_kernel, tm=tm, tm_e=tm_e, nt=nt)
    return pl.pallas_call(
        kern,
        grid_spec=pltpu.PrefetchScalarGridSpec(
            num_scalar_prefetch=3,
            grid=(T // tm,),
            in_specs=[pl.BlockSpec((tm * TOP_K,), lambda i, *_: (i,), memory_space=pltpu.SMEM),
                      pl.BlockSpec((tm, d), lambda i, *_: (i, 0))],
            out_specs=pl.BlockSpec(memory_space=pl.ANY),
            scratch_shapes=[pltpu.VMEM((tm_e, d), F32), pltpu.SemaphoreType.DMA, pltpu.SemaphoreType.DMA]),
        out_shape=jax.ShapeDtypeStruct((nt * tm_e, d), F32),
        compiler_params=_cparams("arbitrary"),
        name="moe_dispatch",
    )(gend, gsz, n_valid, pos, x)


def _expert_kernel(te_ref, nv_ref, xs_ref, wgu_ref, bgu_ref, wd_ref, bd_ref, ys_ref):
    del te_ref

    @pl.when(pl.program_id(0) < nv_ref[0])
    def _():
        h = _dot(xs_ref[...].astype(BF16), wgu_ref[0]) + bgu_ref[0]
        hg = jnp.minimum(h[:, :D_FF], SWIGLU_LIMIT)
        hl = jnp.clip(h[:, D_FF:], -SWIGLU_LIMIT, SWIGLU_LIMIT)
        act = hg * _sigmoid(SWIGLU_ALPHA * hg) * (hl + 1.0)
        ys_ref[...] = _dot(act.astype(BF16), wd_ref[0].astype(BF16)) + bd_ref[0]

    @pl.when(pl.program_id(0) >= nv_ref[0])
    def _():
        ys_ref[...] = jnp.zeros_like(ys_ref)


def _experts(xs, tile_e, n_valid, lw, tm_e):
    rows, d = xs.shape
    nt = rows // tm_e
    tile = lambda j, te, nv: (jnp.minimum(j, nv[0] - 1), 0)
    wsel = lambda j, te, nv: (te[jnp.minimum(j, nv[0] - 1)], 0, 0)
    e0 = lw["w_down_e0"]
    wsel_down = lambda j, te, nv: (e0 + te[jnp.minimum(j, nv[0] - 1)], 0, 0)
    return pl.pallas_call(
        _expert_kernel,
        grid_spec=pltpu.PrefetchScalarGridSpec(
            num_scalar_prefetch=2,
            grid=(nt,),
            in_specs=[pl.BlockSpec((tm_e, d), tile),
                      pl.BlockSpec((1, d, 2 * D_FF), wsel), pl.BlockSpec((1, 1, 2 * D_FF), wsel),
                      pl.BlockSpec((1, D_FF, d), wsel_down), pl.BlockSpec((1, 1, d), wsel)],
            out_specs=pl.BlockSpec((tm_e, d), lambda j, te, nv: (j, 0))),
        out_shape=jax.ShapeDtypeStruct((rows, d), F32),
        compiler_params=_cparams("arbitrary"),
        name="moe_experts",
    )(tile_e, n_valid, xs, lw["w_gu"], lw["b_gu"], lw["w_down"], lw["b_down"])


def _combine_kernel(pos_ref, x_ref, info_ref, ys_ref, ln2g_ref, ln2b_ref, out_ref, buf, sem, *, tm):
    def issue(t, carry):
        for k in range(TOP_K):
            p = pos_ref[t * TOP_K + k]
            pltpu.make_async_copy(ys_ref.at[pl.ds(p, 1)], buf.at[k, pl.ds(t, 1)], sem).start(priority=k % 2)
        return carry

    lax.fori_loop(0, tm, issue, 0, unroll=8)
    for k in range(TOP_K):
        pltpu.make_async_copy(ys_ref.at[pl.ds(0, tm)], buf.at[k], sem).wait()
    rc = min(128, tm)
    lane = lax.broadcasted_iota(jnp.int32, (rc, LANES), 1)

    def rows_pass(c, carry):
        rows = pl.ds(pl.multiple_of(c * rc, rc), rc)
        info = info_ref[rows, :]
        acc = DN_ALPHA * x_ref[rows, :]
        for k in range(TOP_K):
            gate = jnp.sum(jnp.where(lane == INFO_GATE + k, info, 0.0), -1, keepdims=True)
            acc = acc + gate * buf[k, rows, :]
        out_ref[rows, :] = _layer_norm(acc, ln2g_ref[...], ln2b_ref[...])
        return carry

    lax.fori_loop(0, tm // rc, rows_pass, 0)


def _combine(x, info, pos, ys, lw, *, tm=512):
    T, d = x.shape
    tm = min(tm, T)
    kern = functools.partial(_combine_kernel, tm=tm)
    return pl.pallas_call(
        kern,
        grid=(T // tm,),
        in_specs=[pl.BlockSpec((tm * TOP_K,), lambda i: (i,), memory_space=pltpu.SMEM),
                  pl.BlockSpec((tm, d), lambda i: (i, 0)),
                  pl.BlockSpec((tm, LANES), lambda i: (i, 0)),
                  pl.BlockSpec(memory_space=pl.ANY),
                  _full((1, d)), _full((1, d))],
        out_specs=pl.BlockSpec((tm, d), lambda i: (i, 0)),
        out_shape=jax.ShapeDtypeStruct((T, d), F32),
        scratch_shapes=[pltpu.VMEM((TOP_K, tm, d), F32), pltpu.SemaphoreType.DMA],
        compiler_params=_cparams("arbitrary"),
        name="moe_combine",
    )(pos, x, info, ys, lw["ln2_g"], lw["ln2_b"])


def _moe(x, lw):
    T = x.shape[0]
    tm_e = 512 if T >= 4096 else 128
    nt = T * TOP_K // tm_e + N_EXPERTS
    info, cnt = _router(x, lw)
    pos, tile_e, n_valid, gend, gsz = _moe_schedule(info, cnt, tm_e, nt)
    xs = _dispatch(x, pos, gend, gsz, n_valid, tm_e, nt)
    ys = _experts(xs, tile_e, n_valid, lw, tm_e)
    return _combine(x, info, pos, ys, lw)


def _ones_block_diag():
    idx = np.arange(RWKV_DIM) // RWKV_HEAD
    return jnp.asarray((idx[:, None] == idx[None, :]).astype(np.float32), dtype=BF16)


def _deinterleave_matrix():
    n = 2 * D_FF
    src = np.concatenate([np.arange(0, n, 2), np.arange(1, n, 2)])
    return jnp.asarray((np.arange(n)[:, None] == src[None, :]).astype(np.float32), dtype=BF16)


def _block_diag(x):
    C, G, r, c = x.shape
    eye = jnp.eye(G, dtype=x.dtype)
    return jnp.einsum("cgij,gh->cgihj", x, eye).reshape(C, G * r, G * c)


def _prep_layer(P, l):
    f = lambda name: P[name][l]
    lw = {}
    w_in = f("w_in")
    offs = np.cumsum((0,) + IN_SIZES)
    cols = lambda j: w_in[:, offs[j]:offs[j + 1]]
    w_q, w_kv = cols(0), cols(1)
    w_ckv, w_kpe = w_kv[:, :MLA_KV_RANK], w_kv[:, MLA_KV_RANK:]
    half = MLA_ROPE // 2
    zpad = jnp.zeros((D_MODEL, LANES - MLA_ROPE), F32)
    w_kpe_rot = jnp.concatenate([-w_kpe[:, half:], w_kpe[:, :half]], 1)
    lw["w_mla"] = jnp.concatenate([w_q, w_ckv, w_kpe, zpad, w_kpe_rot, zpad], 1).astype(BF16)
    lw["w_rw"] = cols(2).astype(BF16)
    lw["w_su"] = cols(3).astype(BF16)
    lw["w_gates"] = jnp.concatenate([cols(4), cols(5), cols(6)], 1).astype(BF16)
    lw["q_norm"] = f("mla_q_a_norm").reshape(1, -1)
    lw["kv_norm"] = f("mla_kv_a_norm").reshape(1, -1)
    wqb = f("mla_w_q_b").reshape(MLA_Q_RANK, MLA_HEADS, MLA_NOPE + MLA_ROPE)
    nope, x1, x2 = wqb[..., :MLA_NOPE], wqb[..., MLA_NOPE:MLA_NOPE + half], wqb[..., MLA_NOPE + half:]
    z32 = jnp.zeros_like(wqb[..., :HEAD_PAD - MLA_NOPE - MLA_ROPE])
    plain = jnp.concatenate([x1, x2, nope, z32], -1).reshape(MLA_Q_RANK, -1)
    rot = jnp.concatenate([-x2, x1, jnp.zeros_like(nope), z32], -1).reshape(MLA_Q_RANK, -1)
    lw["w_qb"] = jnp.concatenate([plain, rot], 1).astype(BF16)
    wkvb = f("mla_w_kv_b").reshape(MLA_KV_RANK, MLA_HEADS, MLA_NOPE + MLA_V)
    k_nope, v = wkvb[..., :MLA_NOPE], wkvb[..., MLA_NOPE:]
    zk = jnp.zeros_like(k_nope[..., :MLA_ROPE])
    lw["w_k"] = jnp.concatenate([zk, k_nope, zk], -1).reshape(MLA_KV_RANK, -1).astype(BF16)
    lw["w_v"] = v.reshape(MLA_KV_RANK, -1).astype(BF16)
    row = lambda name: f(name).reshape(1, -1)
    lw["mu"] = row("rwkv_mu")
    lw["w0"] = row("rwkv_w0")
    z64 = jnp.zeros((64, RWKV_DIM), F32)
    lw["w_up"] = jnp.concatenate([f("rwkv_w_up"), z64], 0).astype(BF16)
    lw["a_up"] = jnp.concatenate([z64, f("rwkv_a_up")], 0).astype(BF16)
    lw["a0"] = row("rwkv_a0")
    lw["g_up"] = f("rwkv_g_up").astype(BF16)
    lw["k_k"] = row("rwkv_k_k")
    lw["k_a"] = row("rwkv_k_a")
    lw["r_k"] = row("rwkv_r_k")
    lw["lnx_g"] = row("rwkv_lnx_g")
    lw["lnx_b"] = row("rwkv_lnx_b")
    lw["ones_bd"] = _ones_block_diag()
    ab_re, ab_im, bb_re, bb_im = _s5_params(f("ssm_a_re"), f("ssm_a_im"), f("ssm_log_dt"),
                                            f("ssm_b_re"), f("ssm_b_im"))
    gpc = SSM_GROUPS // SSM_CHUNKS
    chunked = lambda t: t.reshape(SSM_CHUNKS, gpc, t.shape[1], t.shape[2])
    lw["s5_bre"] = _block_diag(chunked(jnp.transpose(bb_re, (1, 0, 2)))).astype(BF16)
    lw["s5_bim"] = _block_diag(chunked(jnp.transpose(bb_im, (1, 0, 2)))).astype(BF16)
    lw["s5_cre"] = _block_diag(chunked(jnp.transpose(f("ssm_c_re"), (0, 2, 1)))).astype(BF16)
    lw["s5_cim"] = _block_diag(chunked(jnp.transpose(f("ssm_c_im"), (0, 2, 1)))).astype(BF16)
    lw["s5_d"] = row("ssm_d")
    lw["s5_are"] = ab_re.reshape(1, -1)
    lw["s5_aim"] = ab_im.reshape(1, -1)
    lw["glu_w"] = f("ssm_glu_w").astype(BF16)
    lw["glu_b"] = row("ssm_glu_b")
    for name in ("w_br_a", "w_br_b", "w_br_c", "w_out"):
        lw[name] = f(name).astype(BF16)
    for name in ("ln1_g", "ln1_b", "ln2_g", "ln2_b"):
        lw[name] = row(name)
    rw_ = jnp.pad(f("router_w"), ((0, 0), (0, LANES - N_EXPERTS)))
    hi = rw_.astype(BF16)
    lw["router_hi"] = hi
    lw["router_lo"] = (rw_ - hi.astype(F32)).astype(BF16)
    lw["router_b"] = jnp.pad(f("router_b"), (0, LANES - N_EXPERTS), constant_values=NEG_INF).reshape(1, -1)
    rows_l = N_EXPERTS * D_MODEL
    wgu = _matmul(P["exp_w_gu"].reshape(-1, 2 * D_FF), _deinterleave_matrix(), 1, rows_l, out_dtype=BF16,
                  row0=l * rows_l)
    lw["w_gu"] = wgu.reshape(N_EXPERTS, D_MODEL, 2 * D_FF)
    bgu = f("exp_b_gu")
    lw["b_gu"] = jnp.concatenate([bgu[..., 0::2], bgu[..., 1::2]], -1).reshape(N_EXPERTS, 1, 2 * D_FF)
    lw["w_down"] = P["exp_w_down"].reshape(-1, D_FF, D_MODEL)
    lw["w_down_e0"] = l * N_EXPERTS
    lw["b_down"] = f("exp_b_down").reshape(N_EXPERTS, 1, D_MODEL)
    return lw


def _rope_tables(pos):
    half = MLA_ROPE // 2
    inv = ROPE_THETA ** (-jnp.arange(half, dtype=F32) / half)
    ang = pos.astype(F32)[:, None] * inv
    cos, sin = jnp.cos(ang), jnp.sin(ang)
    n = pos.shape[0]
    tc = jnp.concatenate([cos, cos, jnp.ones((n, MLA_NOPE), F32),
                          jnp.zeros((n, HEAD_PAD - MLA_NOPE - MLA_ROPE), F32)], 1)
    ts = jnp.concatenate([sin, sin, jnp.zeros((n, HEAD_PAD - MLA_ROPE), F32)], 1)
    return tc, ts


def _to_lanes(t, nb, s, lanes):
    t = t.reshape(s, nb * RWKV_HEADS, RWKV_HEAD).transpose(0, 2, 1)
    return jnp.pad(t, ((0, 0), (0, 0), (0, lanes - nb * RWKV_HEADS)))


def _layer(x, lw, past, nb, s, tables, att_tk=LANES):
    T = nb * s
    tc, ts = tables
    if past is None:
        start = 0
        shift_p = jnp.zeros((nb, RWKV_PROJ), F32)
        wkv_p = jnp.zeros((nb, RWKV_HEADS, RWKV_HEAD, RWKV_HEAD), F32)
        sre_p = jnp.zeros((nb, SSM_CH), F32)
        sim_p = jnp.zeros((nb, SSM_CH), F32)
    else:
        ckv_p, kpe_p, shift_p, wkv_p, sre_p, sim_p = past
        start = ckv_p.shape[1]
        sre_p = sre_p.reshape(nb, SSM_CH)
        sim_p = sim_p.reshape(nb, SSM_CH)

    q, ckv, kpe128 = _mla_in(x, lw, tc, ts)
    if past is None:
        sk = s
        ckv_all, kpe_all = ckv, kpe128
    else:
        sk = start + s
        ckv_all = jnp.concatenate([ckv_p, ckv.reshape(nb, s, -1)], 1)
        kpe_new = kpe128.reshape(nb, s, LANES)
        kpe_all = jnp.concatenate([jnp.pad(kpe_p, ((0, 0), (0, 0), (0, LANES - MLA_ROPE))), kpe_new], 1)
    skp = -(-sk // att_tk) * att_tk
    if skp != sk:
        ckv_all = jnp.pad(ckv_all.reshape(nb, sk, -1), ((0, 0), (0, skp - sk), (0, 0)))
        kpe_all = jnp.pad(kpe_all.reshape(nb, sk, -1), ((0, 0), (0, skp - sk), (0, 0)))
    k_pad, v_all = _kv_expand(ckv_all.reshape(nb * skp, -1), kpe_all.reshape(nb * skp, -1), lw)
    ya = _attention(q, k_pad, v_all, nb, s, skp, sk, start)

    r, wdec, kn, v, kk, kb, g, bonus, last_row = _rwkv_pre(x, shift_p, lw, nb, s)
    lanes = -(-nb * RWKV_HEADS // LANES) * LANES
    seqs = [_to_lanes(t, nb, s, lanes) for t in (r, wdec, kn, v, kk, kb)]
    s0 = jnp.transpose(wkv_p, (3, 2, 0, 1)).reshape(RWKV_HEAD, RWKV_HEAD, nb * RWKV_HEADS)
    s0 = jnp.pad(s0, ((0, 0), (0, 0), (0, lanes - nb * RWKV_HEADS)))
    o_l, sT = _rwkv_scan(seqs, s0, s)
    o_tm = o_l[:, :, :nb * RWKV_HEADS].transpose(0, 2, 1).reshape(s, nb * RWKV_DIM)
    wkv_n = (sT[:, :, :nb * RWKV_HEADS].reshape(RWKV_HEAD, RWKV_HEAD, nb, RWKV_HEADS).transpose(2, 3, 1, 0))
    shift_n = last_row.reshape(nb, RWKV_PROJ)

    su_tm = _matmul(x, lw["w_su"], nb, s, time_major_out=True).reshape(s * nb, SSM_DIM)
    yc_tm, sre_n, sim_n = _s5(su_tm, sre_p, sim_p, lw, nb, s)
    yc_tm = yc_tm.reshape(s, nb * SSM_DIM)

    x1 = _merge(x, ya, o_tm, bonus, g, yc_tm, lw, nb, s)
    x2 = _moe(x1, lw)
    new = (ckv.reshape(nb, s, MLA_KV_RANK), kpe128[:, :MLA_ROPE].reshape(nb, s, MLA_ROPE), shift_n, wkv_n,
           sre_n.reshape(nb, SSM_GROUPS, SSM_STATE), sim_n.reshape(nb, SSM_GROUPS, SSM_STATE))
    return x2, new


def _trunk(x3, weights, caches):
    nb, s, d = x3.shape
    start = 0 if caches is None else caches[0].shape[2]
    tc, ts = _rope_tables(start + jnp.arange(s))
    tables = (jnp.tile(tc, (nb, 1)), jnp.tile(ts, (nb, 1)))
    x = x3.reshape(nb * s, d)
    new = []
    for l in range(len(weights)):
        past = None if caches is None else tuple(c[l] for c in caches)
        x, st = _layer(x, weights[l], past, nb, s, tables)
        new.append(st)
    return (x.reshape(nb, s, d),) + tuple(jnp.stack([st[j] for st in new]) for j in range(6))


def kernel(x_prompt, x_sample, cache_mla_ckv, cache_mla_kpe, state_rwkv_shift, state_rwkv_wkv, state_ssm_re, state_ssm_im, w_in, mla_q_a_norm, mla_w_q_b, mla_kv_a_norm, mla_w_kv_b, rwkv_mu, rwkv_w0, rwkv_w_up, rwkv_a0, rwkv_a_up, rwkv_g_up, rwkv_k_k, rwkv_k_a, rwkv_r_k, rwkv_lnx_g, rwkv_lnx_b, ssm_a_re, ssm_a_im, ssm_b_re, ssm_b_im, ssm_c_re, ssm_c_im, ssm_d, ssm_log_dt, ssm_glu_w, ssm_glu_b, w_br_a, w_br_b, w_br_c, w_out, ln1_g, ln1_b, router_w, router_b, exp_w_gu, exp_b_gu, exp_w_down, exp_b_down, ln2_g, ln2_b):
    P = dict(w_in=w_in, mla_q_a_norm=mla_q_a_norm, mla_w_q_b=mla_w_q_b, mla_kv_a_norm=mla_kv_a_norm,
             mla_w_kv_b=mla_w_kv_b, rwkv_mu=rwkv_mu, rwkv_w0=rwkv_w0, rwkv_w_up=rwkv_w_up, rwkv_a0=rwkv_a0,
             rwkv_a_up=rwkv_a_up, rwkv_g_up=rwkv_g_up, rwkv_k_k=rwkv_k_k, rwkv_k_a=rwkv_k_a, rwkv_r_k=rwkv_r_k,
             rwkv_lnx_g=rwkv_lnx_g, rwkv_lnx_b=rwkv_lnx_b, ssm_a_re=ssm_a_re, ssm_a_im=ssm_a_im,
             ssm_b_re=ssm_b_re, ssm_b_im=ssm_b_im, ssm_c_re=ssm_c_re, ssm_c_im=ssm_c_im, ssm_d=ssm_d,
             ssm_log_dt=ssm_log_dt, ssm_glu_w=ssm_glu_w, ssm_glu_b=ssm_glu_b, w_br_a=w_br_a, w_br_b=w_br_b,
             w_br_c=w_br_c, w_out=w_out, ln1_g=ln1_g, ln1_b=ln1_b, router_w=router_w, router_b=router_b,
             exp_w_gu=exp_w_gu, exp_b_gu=exp_b_gu, exp_w_down=exp_w_down, exp_b_down=exp_b_down,
             ln2_g=ln2_g, ln2_b=ln2_b)
    depth = w_in.shape[0]
    weights = [_prep_layer(P, l) for l in range(depth)]
    outs_p = _trunk(x_prompt, weights, None)
    caches = (cache_mla_ckv, cache_mla_kpe, state_rwkv_shift, state_rwkv_wkv, state_ssm_re, state_ssm_im)
    outs_s = _trunk(x_sample, weights, caches)
    return (outs_p[0], outs_s[0]) + outs_p[1:] + outs_s[1:]
```

```python
import functools
import math

import numpy as np
import jax
import jax.numpy as jnp
from jax import lax
from jax.experimental import pallas as pl
from jax.experimental.pallas import tpu as pltpu

F32 = jnp.float32
BF16 = jnp.bfloat16

D_MODEL = 1024
CHUNK = 64
MLA_HEADS = 8
MLA_NOPE = 64
MLA_ROPE = 32
MLA_V = 64
MLA_Q_RANK = 384
MLA_KV_RANK = 256
ROPE_THETA = 10000.0
HEAD_PAD = 128
RWKV_HEADS = 8
RWKV_HEAD = 64
RWKV_DIM = RWKV_HEADS * RWKV_HEAD
RWKV_PROJ = 3 * RWKV_DIM + 64 + 64 + 128
RWKV_LN_EPS = 64e-5
RWKV_BLOCK = 16
SSM_DIM = 512
SSM_GROUP = 16
SSM_GROUPS = 32
SSM_STATE = 64
SSM_CHUNKS = 4
SSM_CH = SSM_GROUPS * SSM_STATE
IN_SIZES = (MLA_Q_RANK, MLA_KV_RANK + MLA_ROPE, RWKV_PROJ, SSM_DIM, D_MODEL, D_MODEL, D_MODEL)
N_EXPERTS = 32
TOP_K = 4
D_FF = 512
SWIGLU_LIMIT = 7.0
SWIGLU_ALPHA = 1.702
DEPTH = 2
DN_ALPHA = (2 * DEPTH) ** 0.25
LN_EPS = 1e-5
RMS_EPS = 1e-6
NEG_INF = -1e30
ATT_SCALE = (MLA_NOPE + MLA_ROPE) ** -0.5
LANES = 128
VMEM_LIMIT = 48 * 1024 * 1024


def _cparams(*sem):
    return pltpu.CompilerParams(dimension_semantics=sem, vmem_limit_bytes=VMEM_LIMIT)


def _dot(a, b):
    return jnp.dot(a, b, preferred_element_type=F32)


def _sigmoid(x):
    return 1.0 / (1.0 + jnp.exp(-x))


def _layer_norm(x, g, b):
    mu = jnp.mean(x, -1, keepdims=True)
    xc = x - mu
    var = jnp.mean(xc * xc, -1, keepdims=True)
    return xc * lax.rsqrt(var + LN_EPS) * g + b


def _full(shape):
    n = len(shape)
    return pl.BlockSpec(shape, lambda *_: (0,) * n)


def _mm_kernel(x_ref, w_ref, o_ref):
    o_ref[...] = _dot(x_ref[...].astype(BF16), w_ref[...]).astype(o_ref.dtype)


def _matmul(x, w, nb, s, *, time_major_out=False, tm=512, tn=1024, out_dtype=F32, row0=0):
    K = x.shape[1]
    N = w.shape[1]
    tm = min(tm, s)
    tn = min(tn, N)
    while N % tn:
        tn -= LANES
    nt = s // tm
    blk0 = row0 // tm
    if time_major_out:
        assert tn == N
        out_shape = jax.ShapeDtypeStruct((s, nb * N), out_dtype)
        out_spec = pl.BlockSpec((tm, N), lambda b, i, j: (i, b))
    else:
        out_shape = jax.ShapeDtypeStruct((nb * s, N), out_dtype)
        out_spec = pl.BlockSpec((tm, tn), lambda b, i, j: (b * nt + i, j))
    return pl.pallas_call(
        _mm_kernel,
        grid=(nb, nt, N // tn),
        in_specs=[pl.BlockSpec((tm, K), lambda b, i, j: (blk0 + b * nt + i, 0)),
                  pl.BlockSpec((K, tn), lambda b, i, j: (0, j))],
        out_specs=out_spec,
        out_shape=out_shape,
        compiler_params=_cparams("parallel", "parallel", "arbitrary"),
        name="matmul",
    )(x, w)


def _mla_in_kernel(x_ref, w_ref, qg_ref, kvg_ref, wqb_ref, tc_ref, ts_ref, q_ref, ckv_ref, kpe_ref):
    h = _dot(x_ref[...].astype(BF16), w_ref[...])
    tc = tc_ref[...]
    ts = ts_ref[...]
    qa = h[:, :MLA_Q_RANK]
    qn = qa * lax.rsqrt(jnp.mean(qa * qa, -1, keepdims=True) + RMS_EPS) * qg_ref[...]
    q2 = _dot(qn.astype(BF16), wqb_ref[...])
    hp = MLA_HEADS * HEAD_PAD
    for hd in range(MLA_HEADS):
        lo = hd * HEAD_PAD
        q0 = q2[:, lo:lo + HEAD_PAD]
        q1 = q2[:, hp + lo:hp + lo + HEAD_PAD]
        q_ref[:, lo:lo + HEAD_PAD] = ((q0 * tc + q1 * ts) * ATT_SCALE).astype(BF16)
    c0 = MLA_Q_RANK
    ckv = h[:, c0:c0 + MLA_KV_RANK]
    ckv_ref[...] = ckv * lax.rsqrt(jnp.mean(ckv * ckv, -1, keepdims=True) + RMS_EPS) * kvg_ref[...]
    c1 = c0 + MLA_KV_RANK
    kpe_ref[...] = h[:, c1:c1 + LANES] * tc + h[:, c1 + LANES:c1 + 2 * LANES] * ts


def _mla_in(x, lw, tc, ts, *, tm=512):
    T = x.shape[0]
    tm = min(tm, T)
    hp = MLA_HEADS * HEAD_PAD
    nw = lw["w_mla"].shape[1]
    return pl.pallas_call(
        _mla_in_kernel,
        grid=(T // tm,),
        in_specs=[pl.BlockSpec((tm, D_MODEL), lambda i: (i, 0)),
                  _full((D_MODEL, nw)), _full((1, MLA_Q_RANK)), _full((1, MLA_KV_RANK)),
                  _full((MLA_Q_RANK, 2 * hp)),
                  pl.BlockSpec((tm, LANES), lambda i: (i, 0)),
                  pl.BlockSpec((tm, LANES), lambda i: (i, 0))],
        out_specs=[pl.BlockSpec((tm, hp), lambda i: (i, 0)),
                   pl.BlockSpec((tm, MLA_KV_RANK), lambda i: (i, 0)),
                   pl.BlockSpec((tm, LANES), lambda i: (i, 0))],
        out_shape=[jax.ShapeDtypeStruct((T, hp), BF16),
                   jax.ShapeDtypeStruct((T, MLA_KV_RANK), F32),
                   jax.ShapeDtypeStruct((T, LANES), F32)],
        compiler_params=_cparams("parallel"),
        name="mla_in",
    )(x, lw["w_mla"], lw["q_norm"], lw["kv_norm"], lw["w_qb"], tc, ts)


def _kv_expand_kernel(ckv_ref, kpe_ref, wk_ref, wv_ref, k_ref, v_ref):
    c = ckv_ref[...].astype(BF16)
    k = _dot(c, wk_ref[...])
    kpe = kpe_ref[...]
    for hd in range(MLA_HEADS):
        lo = hd * HEAD_PAD
        k_ref[:, lo:lo + HEAD_PAD] = (k[:, lo:lo + HEAD_PAD] + kpe).astype(BF16)
    v_ref[...] = _dot(c, wv_ref[...]).astype(BF16)


def _kv_expand(ckv, kpe, lw, *, tm=512):
    T = ckv.shape[0]
    tm = min(tm, T)
    while T % tm:
        tm //= 2
    hp = MLA_HEADS * HEAD_PAD
    hv = MLA_HEADS * MLA_V
    return pl.pallas_call(
        _kv_expand_kernel,
        grid=(T // tm,),
        in_specs=[pl.BlockSpec((tm, MLA_KV_RANK), lambda i: (i, 0)),
                  pl.BlockSpec((tm, LANES), lambda i: (i, 0)),
                  _full((MLA_KV_RANK, hp)), _full((MLA_KV_RANK, hv))],
        out_specs=[pl.BlockSpec((tm, hp), lambda i: (i, 0)),
                   pl.BlockSpec((tm, hv), lambda i: (i, 0))],
        out_shape=[jax.ShapeDtypeStruct((T, hp), BF16), jax.ShapeDtypeStruct((T, hv), BF16)],
        compiler_params=_cparams("parallel"),
        name="kv_expand",
    )(ckv, kpe, lw["w_k"], lw["w_v"])


def _attn_kernel(q_ref, k_ref, v_ref, o_ref, *, tq, nq, q_start, sk):
    lane = lax.broadcasted_iota(jnp.int32, (tq, LANES), 1)
    nt = (((1,), (1,)), ((), ()))
    for i in range(nq):
        q_lo = q_start + i * tq
        k_end = min(((q_lo + tq - 1) // CHUNK + 1) * CHUNK, sk)
        kw = -(-k_end // LANES) * LANES
        c0 = min((q_lo // CHUNK + 1) * CHUNK, sk) // LANES * LANES
        tail = kw - c0
        if tail:
            q_chunk = (q_lo + lax.broadcasted_iota(jnp.int32, (tq, tail), 0)) // CHUNK
            k_pos = c0 + lax.broadcasted_iota(jnp.int32, (tq, tail), 1)
            visible = k_pos // CHUNK <= q_chunk
            if kw > sk:
                visible = visible & (k_pos < sk)
        outs = []
        for hh in range(2):
            hs = slice(hh * HEAD_PAD, (hh + 1) * HEAD_PAD)
            q = q_ref[i * tq:(i + 1) * tq, hs]
            parts = []
            if c0:
                parts.append(lax.dot_general(q, k_ref[0:c0, hs], nt, preferred_element_type=F32))
            if tail:
                s_t = lax.dot_general(q, k_ref[c0:kw, hs], nt, preferred_element_type=F32)
                parts.append(jnp.where(visible, s_t, NEG_INF))
            m = jnp.max(parts[0], -1, keepdims=True)
            for s_ in parts[1:]:
                m = jnp.maximum(m, jnp.max(s_, -1, keepdims=True))
            l = None
            acc = None
            lo = 0
            for s_ in parts:
                p = jnp.exp(s_ - m)
                ps = jnp.sum(p, -1, keepdims=True)
                pv = _dot(p.astype(BF16), v_ref[lo:lo + s_.shape[1], :])
                l = ps if l is None else l + ps
                acc = pv if acc is None else acc + pv
                lo += s_.shape[1]
            outs.append(acc / l)
        o_ref[i * tq:(i + 1) * tq, :] = jnp.where(lane < MLA_V, outs[0], outs[1])


def _attention(q, k, v, nb, sq, skp, sk, q_start, *, tq=512):
    tq = min(tq, sq)
    kern = functools.partial(_attn_kernel, tq=tq, nq=sq // tq, q_start=q_start, sk=sk)
    return pl.pallas_call(
        kern,
        grid=(nb, MLA_HEADS // 2),
        in_specs=[pl.BlockSpec((sq, 2 * HEAD_PAD), lambda b, j: (b, j)),
                  pl.BlockSpec((skp, 2 * HEAD_PAD), lambda b, j: (b, j)),
                  pl.BlockSpec((skp, 2 * MLA_V), lambda b, j: (b, j))],
        out_specs=pl.BlockSpec((sq, 2 * MLA_V), lambda b, j: (b, j)),
        out_shape=jax.ShapeDtypeStruct((nb * sq, MLA_HEADS * MLA_V), F32),
        compiler_params=_cparams("parallel", "parallel"),
        name="attention",
    )(q, k, v)


def _head_sum(z, ones_bd):
    hi = z.astype(BF16)
    lo = (z - hi.astype(F32)).astype(BF16)
    return _dot(hi, ones_bd) + _dot(lo, ones_bd)


def _rwkv_pre_kernel(x_ref, xprev_ref, shift_ref, wrw_ref, mu_ref, w0_ref, wup_ref, a0_ref, aup_ref, gup_ref,
                     kk_ref, ka_ref, rk_ref, ones_ref, blk_ref,
                     r_out, w_out, k_out, v_out, kk_out, b_out, g_out, bonus_out, last_out):
    p = _dot(x_ref[...].astype(BF16), wrw_ref[...])
    first = pl.program_id(1) == 0
    p_before = _dot(xprev_ref[...].astype(BF16), wrw_ref[...])
    prev_row = jnp.where(first, shift_ref[0], p_before[7:8, :])
    last_out[0] = p[p.shape[0] - 1:, :]
    row = lax.broadcasted_iota(jnp.int32, p.shape, 0)
    prev = jnp.where(row == 0, prev_row, pltpu.roll(p, 1, 0))
    ps = p + (prev - p) * mu_ref[...]
    d = RWKV_DIM
    r = ps[:, 0:d]
    k = ps[:, d:2 * d]
    v = ps[:, 2 * d:3 * d]
    wa = ps[:, 3 * d:3 * d + LANES]
    gd = ps[:, 3 * d + LANES:3 * d + 2 * LANES]
    ones_bd = ones_ref[...]
    wlin = w0_ref[...] + _dot(jnp.tanh(wa).astype(BF16), wup_ref[...])
    z = -wlin
    w = -(jnp.maximum(z, 0.0) + jnp.log1p(jnp.exp(-jnp.abs(z)))) - 0.5
    log_decay = -jnp.exp(w)
    blk = blk_ref[...]
    rows = blk.shape[0]
    pieces, rest = [], log_decay
    for _ in range(3):
        pieces.append(rest.astype(BF16))
        rest = rest - pieces[-1].astype(F32)
    cum = jnp.concatenate(
        [sum(_dot(blk, pc[g * rows:(g + 1) * rows]) for pc in pieces) for g in range(p.shape[0] // rows)], 0)
    p_incl = jnp.exp(cum)
    p_inv = jnp.exp(-cum)
    w_out[...] = p_incl
    a = _sigmoid(a0_ref[...] + _dot(wa.astype(BF16), aup_ref[...]))
    g_out[...] = _dot(_sigmoid(gd).astype(BF16), gup_ref[...])
    kk = k * kk_ref[...]
    kkn = kk * lax.rsqrt(_head_sum(kk * kk, ones_bd) + 1e-12)
    kk_out[...] = kkn * jnp.exp(cum - log_decay)
    b_out[...] = kkn * a * p_inv
    kn = k * (1.0 + (a - 1.0) * ka_ref[...])
    r_out[...] = r * p_incl
    k_out[...] = kn * p_inv
    v_out[...] = v
    bonus_out[...] = _head_sum(r * kn * rk_ref[...], ones_bd) * v


def _rwkv_pre(x, shift_prev, lw, nb, s, *, tm=512):
    tm = min(tm, s)
    tps = s // tm
    d = RWKV_DIM
    vec = lambda n: _full((1, n))
    bm = pl.BlockSpec((tm, d), lambda b, i: (b * tps + i, 0))
    tmaj = pl.BlockSpec((tm, d), lambda b, i: (i, b))
    per_seq = pl.BlockSpec((1, 1, RWKV_PROJ), lambda b, i: (b, 0, 0))
    bm_shape = jax.ShapeDtypeStruct((nb * s, d), F32)
    tm_shape = jax.ShapeDtypeStruct((s, nb * d), F32)
    br = min(tm, LANES)
    assert tm % br == 0 and br % RWKV_BLOCK == 0
    idx = np.arange(br)
    blk = (idx[:, None] // RWKV_BLOCK == idx[None, :] // RWKV_BLOCK) & (idx[None, :] <= idx[:, None])
    blk = jnp.asarray(blk.astype(np.float32), dtype=BF16)
    return pl.pallas_call(
        _rwkv_pre_kernel,
        grid=(nb, tps),
        in_specs=[pl.BlockSpec((tm, D_MODEL), lambda b, i: (b * tps + i, 0)),
                  pl.BlockSpec((8, D_MODEL), lambda b, i: (jnp.maximum((b * tps + i) * (tm // 8) - 1, 0), 0)),
                  per_seq, _full((D_MODEL, RWKV_PROJ)),
                  vec(RWKV_PROJ), vec(d), _full((LANES, d)), vec(d), _full((LANES, d)),
                  _full((LANES, d)), vec(d), vec(d), vec(d), _full((d, d)), _full((br, br))],
        out_specs=[tmaj] * 6 + [bm] * 2 + [per_seq],
        out_shape=[tm_shape] * 6 + [bm_shape] * 2 + [jax.ShapeDtypeStruct((nb, 1, RWKV_PROJ), F32)],
        compiler_params=_cparams("parallel", "arbitrary"),
        name="rwkv_pre",
    )(x, x, shift_prev.reshape(nb, 1, RWKV_PROJ), lw["w_rw"], lw["mu"], lw["w0"], lw["w_up"], lw["a0"],
      lw["a_up"], lw["g_up"], lw["k_k"], lw["k_a"], lw["r_k"], lw["ones_bd"], blk)


RWKV_VC = 32


def _rwkv_scan_kernel(r_ref, p_ref, k_ref, v_ref, kk_ref, b_ref, s0_ref, o_ref, sT_ref, st_ref, *, tt):
    n = RWKV_HEAD

    @pl.when(pl.program_id(0) == 0)
    def _():
        st_ref[...] = s0_ref[...]

    def step(t, carry):
        for c in range(n // RWKV_VC):
            vs = slice(c * RWKV_VC, (c + 1) * RWKV_VC)
            parts = [None] * 4
            for q in range(n):
                term = st_ref[q, vs, :] * kk_ref[t, q:q + 1, :]
                parts[q % 4] = term if parts[q % 4] is None else parts[q % 4] + term
            sa = -((parts[0] + parts[1]) + (parts[2] + parts[3]))
            vt = v_ref[t, vs, :]
            outs = [None] * 4
            for q in range(n):
                s_new = st_ref[q, vs, :] + (sa * b_ref[t, q:q + 1, :] + vt * k_ref[t, q:q + 1, :])
                st_ref[q, vs, :] = s_new
                term = s_new * r_ref[t, q:q + 1, :]
                outs[q % 4] = term if outs[q % 4] is None else outs[q % 4] + term
            o_ref[t, vs, :] = (outs[0] + outs[1]) + (outs[2] + outs[3])
        return carry

    lax.fori_loop(0, tt, step, 0)
    for q in range(n):
        st_ref[q] = st_ref[q] * p_ref[tt - 1, q:q + 1, :]

    @pl.when(pl.program_id(0) == pl.num_programs(0) - 1)
    def _():
        sT_ref[...] = st_ref[...]


def _rwkv_scan(seqs, s0, s):
    L = s0.shape[-1]
    n = RWKV_HEAD
    tt = RWKV_BLOCK
    assert s % tt == 0
    blk = pl.BlockSpec((tt, n, L), lambda i: (i, 0, 0))
    kern = functools.partial(_rwkv_scan_kernel, tt=tt)
    return pl.pallas_call(
        kern,
        grid=(s // tt,),
        in_specs=[blk] * 6 + [_full((n, n, L))],
        out_specs=[blk, _full((n, n, L))],
        out_shape=[jax.ShapeDtypeStruct((s, n, L), F32), jax.ShapeDtypeStruct((n, n, L), F32)],
        scratch_shapes=[pltpu.VMEM((n, n, L), F32)],
        compiler_params=_cparams("arbitrary"),
        name="rwkv_scan",
    )(*seqs, s0)


def _s5_params_kernel(are_ref, aim_ref, ldt_ref, bre_ref, bim_ref, abre_ref, abim_ref, bbre_ref, bbim_ref):
    lr = are_ref[...]
    li = aim_ref[...]
    dt = jnp.exp(ldt_ref[...])
    mag = jnp.exp(lr * dt)
    ab_re = mag * jnp.cos(li * dt)
    ab_im = mag * jnp.sin(li * dt)
    den = lr * lr + li * li
    f_re = ((ab_re - 1.0) * lr + ab_im * li) / den
    f_im = (ab_im * lr - (ab_re - 1.0) * li) / den
    abre_ref[...] = ab_re
    abim_ref[...] = ab_im
    for i in range(SSM_GROUP):
        br = bre_ref[i]
        bi = bim_ref[i]
        bbre_ref[i] = f_re * br - f_im * bi
        bbim_ref[i] = f_re * bi + f_im * br


def _s5_params(a_re, a_im, log_dt, b_re, b_im):
    g, n = a_re.shape
    gn = jax.ShapeDtypeStruct((g, n), F32)
    ign = jax.ShapeDtypeStruct((SSM_GROUP, g, n), F32)
    return pl.pallas_call(
        _s5_params_kernel,
        out_shape=[gn, gn, ign, ign],
        name="s5_params",
    )(a_re, a_im, log_dt.reshape(g, 1), jnp.transpose(b_re, (2, 0, 1)), jnp.transpose(b_im, (2, 0, 1)))


def _gelu_tanh(x):
    return 0.5 * x * (1.0 + jnp.tanh(math.sqrt(2.0 / math.pi) * (x + 0.044715 * (x * x * x))))


def _s5_kernel(u_ref, bre_ref, bim_ref, cre_ref, cim_ref, d_ref, are_ref, aim_ref, h0re_ref, h0im_ref,
               gw_ref, gb_ref, y_ref, hTre_ref, hTim_ref, hre_s, him_s, sre_s, sim_s, yy_s, *, tt, nb):
    cw = SSM_CH // SSM_CHUNKS

    @pl.when(pl.program_id(0) == 0)
    def _():
        hre_s[...] = h0re_ref[...]
        him_s[...] = h0im_ref[...]

    for c in range(SSM_CHUNKS):
        uc = u_ref[:, c * LANES:(c + 1) * LANES]
        ub = uc.astype(BF16)
        sre_s[...] = _dot(ub, bre_ref[c])
        sim_s[...] = _dot(ub, bim_ref[c])
        a_re = jnp.broadcast_to(are_ref[:, c * cw:(c + 1) * cw], (nb, cw))
        a_im = jnp.broadcast_to(aim_ref[:, c * cw:(c + 1) * cw], (nb, cw))

        def step(t, carry, a_re=a_re, a_im=a_im):
            hr, hi = carry
            rows = pl.ds(pl.multiple_of(t * nb, nb), nb)
            nr = a_re * hr - a_im * hi + sre_s[rows, :]
            ni = a_re * hi + a_im * hr + sim_s[rows, :]
            sre_s[rows, :] = nr
            sim_s[rows, :] = ni
            return nr, ni

        hr, hi = lax.fori_loop(0, tt, step, (hre_s[:, c * cw:(c + 1) * cw], him_s[:, c * cw:(c + 1) * cw]))
        hre_s[:, c * cw:(c + 1) * cw] = hr
        him_s[:, c * cw:(c + 1) * cw] = hi
        yc = _dot(sre_s[...].astype(BF16), cre_ref[c]) - _dot(sim_s[...].astype(BF16), cim_ref[c])
        yy_s[:, c * LANES:(c + 1) * LANES] = yc + d_ref[:, c * LANES:(c + 1) * LANES] * uc

    y = _gelu_tanh(yy_s[...])
    z = _dot(y.astype(BF16), gw_ref[...]) + gb_ref[...]
    y_ref[...] = z[:, :SSM_DIM] * _sigmoid(z[:, SSM_DIM:])

    @pl.when(pl.program_id(0) == pl.num_programs(0) - 1)
    def _():
        hTre_ref[...] = hre_s[...]
        hTim_ref[...] = him_s[...]


def _s5(u_tm, h0_re, h0_im, lw, nb, s, *, rows=1024):
    tt = max(min(rows // nb, s), 1)
    R = tt * nb
    cw = SSM_CH // SSM_CHUNKS
    kern = functools.partial(_s5_kernel, tt=tt, nb=nb)
    return pl.pallas_call(
        kern,
        grid=(s // tt,),
        in_specs=[pl.BlockSpec((R, SSM_DIM), lambda i: (i, 0)),
                  _full((SSM_CHUNKS, LANES, cw)), _full((SSM_CHUNKS, LANES, cw)),
                  _full((SSM_CHUNKS, cw, LANES)), _full((SSM_CHUNKS, cw, LANES)),
                  _full((1, SSM_DIM)), _full((1, SSM_CH)), _full((1, SSM_CH)),
                  _full((nb, SSM_CH)), _full((nb, SSM_CH)),
                  _full((SSM_DIM, 2 * SSM_DIM)), _full((1, 2 * SSM_DIM))],
        out_specs=[pl.BlockSpec((R, SSM_DIM), lambda i: (i, 0)), _full((nb, SSM_CH)), _full((nb, SSM_CH))],
        out_shape=[jax.ShapeDtypeStruct((s * nb, SSM_DIM), F32),
                   jax.ShapeDtypeStruct((nb, SSM_CH), F32), jax.ShapeDtypeStruct((nb, SSM_CH), F32)],
        scratch_shapes=[pltpu.VMEM((nb, SSM_CH), F32), pltpu.VMEM((nb, SSM_CH), F32),
                        pltpu.VMEM((R, cw), F32), pltpu.VMEM((R, cw), F32), pltpu.VMEM((R, SSM_DIM), F32)],
        compiler_params=_cparams("arbitrary"),
        name="s5",
    )(u_tm, lw["s5_bre"], lw["s5_bim"], lw["s5_cre"], lw["s5_cim"], lw["s5_d"], lw["s5_are"], lw["s5_aim"],
      h0_re, h0_im, lw["glu_w"], lw["glu_b"])


def _merge_kernel(x_ref, ya_ref, o_ref, bonus_ref, g_ref, yc_ref, wg_ref, lng_ref, lnb_ref, ones_ref,
                  wa_ref, wb_ref, wc_ref, wo_ref, ln1g_ref, ln1b_ref, out_ref):
    ones_bd = ones_ref[...]
    o = o_ref[...]
    inv_n = 1.0 / RWKV_HEAD
    mean = _head_sum(o, ones_bd) * inv_n
    oc = o - mean
    var = _head_sum(oc * oc, ones_bd) * inv_n
    yb = (oc * lax.rsqrt(var + RWKV_LN_EPS) * lng_ref[...] + lnb_ref[...] + bonus_ref[...]) * g_ref[...]
    d = D_MODEL
    x = x_ref[...]
    xb = x.astype(BF16)
    merged = None
    for j, (y, w_ref) in enumerate(((ya_ref[...], wa_ref), (yb, wb_ref), (yc_ref[...], wc_ref))):
        gate = _sigmoid(_dot(xb, wg_ref[:, j * d:(j + 1) * d]))
        term = gate * _dot(y.astype(BF16), w_ref[...])
        merged = term if merged is None else merged + term
    y = DN_ALPHA * x + _dot(merged.astype(BF16), wo_ref[...])
    out_ref[...] = _layer_norm(y, ln1g_ref[...], ln1b_ref[...])


def _merge(x, ya, o_tm, bonus, g, yc_tm, lw, nb, s, *, tm=512):
    tm = min(tm, s)
    nt = s // tm
    d = D_MODEL
    h = RWKV_DIM
    row = lambda w: pl.BlockSpec((tm, w), lambda b, i: (b * nt + i, 0))
    tmaj = lambda w: pl.BlockSpec((tm, w), lambda b, i: (i, b))
    vec = lambda n: _full((1, n))
    once = lambda shape: pl.BlockSpec(shape, lambda b, i: (0, 0), pipeline_mode=pl.Buffered(1))
    return pl.pallas_call(
        _merge_kernel,
        grid=(nb, nt),
        in_specs=[row(d), row(h), tmaj(h), row(h), row(h), tmaj(SSM_DIM),
                  once((d, 3 * d)), vec(h), vec(h), once((h, h)),
                  once((h, d)), once((h, d)), once((h, d)), once((d, d)), vec(d), vec(d)],
        out_specs=row(d),
        out_shape=jax.ShapeDtypeStruct((nb * s, d), F32),
        compiler_params=_cparams("parallel", "parallel"),
        name="merge",
    )(x, ya, o_tm, bonus, g, yc_tm, lw["w_gates"], lw["lnx_g"], lw["lnx_b"], lw["ones_bd"],
      lw["w_br_a"], lw["w_br_b"], lw["w_br_c"], lw["w_out"], lw["ln1_g"], lw["ln1_b"])


INFO_GATE, INFO_EID, INFO_RANK = 0, 4, 8


def _router_kernel(x_ref, whi_ref, wlo_ref, b_ref, tri_ref, info_ref, cnt_ref, base_s):
    @pl.when(pl.program_id(0) == 0)
    def _():
        base_s[...] = jnp.zeros_like(base_s)

    x = x_ref[...]
    xh = x.astype(BF16)
    xl = (x - xh.astype(F32)).astype(BF16)
    logits = _dot(xh, whi_ref[...]) + _dot(xl, whi_ref[...]) + _dot(xh, wlo_ref[...]) + b_ref[...]
    lane_i = lax.broadcasted_iota(jnp.int32, logits.shape, 1)
    lane = lane_i.astype(F32)
    vals, sels, ids = [], [], []
    for _ in range(TOP_K):
        m = jnp.max(logits, -1, keepdims=True)
        idx = jnp.min(jnp.where(logits == m, lane, float(LANES)), -1, keepdims=True)
        sel = lane == idx
        vals.append(m)
        sels.append(sel)
        ids.append(idx)
        logits = jnp.where(sel, -3e38, logits)
    es = [jnp.exp(v - vals[0]) for v in vals]
    den = es[0] + es[1] + es[2] + es[3]
    chosen = jnp.zeros_like(logits)
    for sel in sels:
        chosen = chosen + jnp.where(sel, 1.0, 0.0)
    before = _dot(tri_ref[...], chosen.astype(BF16)) + base_s[...]
    info = jnp.zeros_like(logits)
    for k in range(TOP_K):
        rank = jnp.sum(jnp.where(sels[k], before, 0.0), -1, keepdims=True)
        info = (info + jnp.where(lane_i == INFO_GATE + k, es[k] / den, 0.0)
                + jnp.where(lane_i == INFO_EID + k, ids[k], 0.0)
                + jnp.where(lane_i == INFO_RANK + k, rank, 0.0))
    info_ref[...] = info
    base_s[...] += jnp.sum(chosen, 0, keepdims=True)
    cnt_ref[...] = base_s[...]


def _router(x, lw, *, tm=512):
    T = x.shape[0]
    tm = min(tm, T)
    tri = jnp.asarray(np.tril(np.ones((tm, tm), np.float32), -1), dtype=BF16)
    return pl.pallas_call(
        _router_kernel,
        grid=(T // tm,),
        in_specs=[pl.BlockSpec((tm, D_MODEL), lambda i: (i, 0)),
                  _full((D_MODEL, LANES)), _full((D_MODEL, LANES)), _full((1, LANES)), _full((tm, tm))],
        out_specs=[pl.BlockSpec((tm, LANES), lambda i: (i, 0)), _full((1, LANES))],
        out_shape=[jax.ShapeDtypeStruct((T, LANES), F32), jax.ShapeDtypeStruct((1, LANES), F32)],
        scratch_shapes=[pltpu.VMEM((1, LANES), F32)],
        compiler_params=_cparams("arbitrary"),
        name="router",
    )(x, lw["router_hi"], lw["router_lo"], lw["router_b"], tri)


def _moe_schedule(info, cnt, tm_e, nt):
    eid = info[:, INFO_EID:INFO_EID + TOP_K].astype(jnp.int32)
    rank = info[:, INFO_RANK:INFO_RANK + TOP_K].astype(jnp.int32)
    counts = cnt[0, :N_EXPERTS].astype(jnp.int32)
    gsz = (counts + tm_e - 1) // tm_e * tm_e
    gend = jnp.cumsum(gsz)
    goff = gend - gsz
    onehot = eid[..., None] == jnp.arange(N_EXPERTS, dtype=jnp.int32)
    pos = jnp.sum(jnp.where(onehot, goff, 0), -1) + rank
    tile_start = jnp.arange(nt, dtype=jnp.int32) * tm_e
    tile_e = jnp.minimum(jnp.sum((gend[None, :] <= tile_start[:, None]).astype(jnp.int32), -1), N_EXPERTS - 1)
    n_valid = (gend[-1] // tm_e).reshape(1)
    i32 = lambda a: a.astype(jnp.int32)
    return i32(pos.reshape(-1)), i32(tile_e), i32(n_valid), i32(gend), i32(gsz)


def _dispatch_kernel(gend_ref, gsz_ref, nv_ref, pos_ref, x_ref, xs_ref, zbuf, sem, zsem, *, tm, tm_e, nt):
    @pl.when(pl.program_id(0) == 0)
    def _():
        zbuf[...] = jnp.zeros_like(zbuf)
        fill = lambda row0: pltpu.make_async_copy(zbuf, xs_ref.at[pl.ds(pl.multiple_of(row0, tm_e), tm_e)], zsem)
        for phase in ("start", "wait"):
            for e in range(N_EXPERTS):
                @pl.when(gsz_ref[e] > 0)
                def _(e=e, phase=phase):
                    getattr(fill(gend_ref[e] - tm_e), phase)()

            def tail(j, carry, phase=phase):
                getattr(fill(j * tm_e), phase)()
                return carry

            lax.fori_loop(nv_ref[0], nt, tail, 0)

    def issue(t, carry):
        for k in range(TOP_K):
            p = pos_ref[t * TOP_K + k]
            pltpu.make_async_copy(x_ref.at[pl.ds(t, 1)], xs_ref.at[pl.ds(p, 1)], sem).start(priority=k % 2)
        return carry

    lax.fori_loop(0, tm, issue, 0, unroll=8)
    for k in range(TOP_K):
        pltpu.make_async_copy(x_ref, xs_ref.at[pl.ds(0, tm)], sem).wait()


def _dispatch(x, pos, gend, gsz, n_valid, tm_e, nt, *, tm=1024):
    T, d = x.shape
    tm = min(tm, T)
    kern = functools.partial(_dispatch_kernel, tm=tm, tm_e=tm_e, nt=nt)
    return pl.pallas_call(
        kern,
        grid_spec=pltpu.PrefetchScalarGridSpec(
            num_scalar_prefetch=3,
            grid=(T // tm,),
            in_specs=[pl.BlockSpec((tm * TOP_K,), lambda i, *_: (i,), memory_space=pltpu.SMEM),
                      pl.BlockSpec((tm, d), lambda i, *_: (i, 0))],
            out_specs=pl.BlockSpec(memory_space=pl.ANY),
            scratch_shapes=[pltpu.VMEM((tm_e, d), F32), pltpu.SemaphoreType.DMA, pltpu.SemaphoreType.DMA]),
        out_shape=jax.ShapeDtypeStruct((nt * tm_e, d), F32),
        compiler_params=_cparams("arbitrary"),
        name="moe_dispatch",
    )(gend, gsz, n_valid, pos, x)


def _expert_kernel(te_ref, nv_ref, xs_ref, wgu_ref, bgu_ref, wd_ref, bd_ref, ys_ref):
    del te_ref

    @pl.when(pl.program_id(0) < nv_ref[0])
    def _():
        h = _dot(xs_ref[...].astype(BF16), wgu_ref[0]) + bgu_ref[0]
        hg = jnp.minimum(h[:, :D_FF], SWIGLU_LIMIT)
        hl = jnp.clip(h[:, D_FF:], -SWIGLU_LIMIT, SWIGLU_LIMIT)
        act = hg * _sigmoid(SWIGLU_ALPHA * hg) * (hl + 1.0)
        ys_ref[...] = _dot(act.astype(BF16), wd_ref[0].astype(BF16)) + bd_ref[0]

    @pl.when(pl.program_id(0) >= nv_ref[0])
    def _():
        ys_ref[...] = jnp.zeros_like(ys_ref)


def _experts(xs, tile_e, n_valid, lw, tm_e):
    rows, d = xs.shape
    nt = rows // tm_e
    tile = lambda j, te, nv: (jnp.minimum(j, nv[0] - 1), 0)
    wsel = lambda j, te, nv: (te[jnp.minimum(j, nv[0] - 1)], 0, 0)
    e0 = lw["w_down_e0"]
    wsel_down = lambda j, te, nv: (e0 + te[jnp.minimum(j, nv[0] - 1)], 0, 0)
    return pl.pallas_call(
        _expert_kernel,
        grid_spec=pltpu.PrefetchScalarGridSpec(
            num_scalar_prefetch=2,
            grid=(nt,),
            in_specs=[pl.BlockSpec((tm_e, d), tile),
                      pl.BlockSpec((1, d, 2 * D_FF), wsel), pl.BlockSpec((1, 1, 2 * D_FF), wsel),
                      pl.BlockSpec((1, D_FF, d), wsel_down), pl.BlockSpec((1, 1, d), wsel)],
            out_specs=pl.BlockSpec((tm_e, d), lambda j, te, nv: (j, 0))),
        out_shape=jax.ShapeDtypeStruct((rows, d), F32),
        compiler_params=_cparams("arbitrary"),
        name="moe_experts",
    )(tile_e, n_valid, xs, lw["w_gu"], lw["b_gu"], lw["w_down"], lw["b_down"])


def _combine_kernel(pos_ref, x_ref, info_ref, ys_ref, ln2g_ref, ln2b_ref, out_ref, buf, sem, *, tm):
    def issue(t, carry):
        for k in range(TOP_K):
            p = pos_ref[t * TOP_K + k]
            pltpu.make_async_copy(ys_ref.at[pl.ds(p, 1)], buf.at[k, pl.ds(t, 1)], sem).start(priority=k % 2)
        return carry

    lax.fori_loop(0, tm, issue, 0, unroll=8)
    for k in range(TOP_K):
        pltpu.make_async_copy(ys_ref.at[pl.ds(0, tm)], buf.at[k], sem).wait()
    rc = min(128, tm)
    lane = lax.broadcasted_iota(jnp.int32, (rc, LANES), 1)

    def rows_pass(c, carry):
        rows = pl.ds(pl.multiple_of(c * rc, rc), rc)
        info = info_ref[rows, :]
        acc = DN_ALPHA * x_ref[rows, :]
        for k in range(TOP_K):
            gate = jnp.sum(jnp.where(lane == INFO_GATE + k, info, 0.0), -1, keepdims=True)
            acc = acc + gate * buf[k, rows, :]
        out_ref[rows, :] = _layer_norm(acc, ln2g_ref[...], ln2b_ref[...])
        return carry

    lax.fori_loop(0, tm // rc, rows_pass, 0)


def _combine(x, info, pos, ys, lw, *, tm=1024):
    T, d = x.shape
    tm = min(tm, T)
    kern = functools.partial(_combine_kernel, tm=tm)
    return pl.pallas_call(
        kern,
        grid=(T // tm,),
        in_specs=[pl.BlockSpec((tm * TOP_K,), lambda i: (i,), memory_space=pltpu.SMEM),
                  pl.BlockSpec((tm, d), lambda i: (i, 0)),
                  pl.BlockSpec((tm, LANES), lambda i: (i, 0)),
                  pl.BlockSpec(memory_space=pl.ANY),
                  _full((1, d)), _full((1, d))],
        out_specs=pl.BlockSpec((tm, d), lambda i: (i, 0)),
        out_shape=jax.ShapeDtypeStruct((T, d), F32),
        scratch_shapes=[pltpu.VMEM((TOP_K, tm, d), F32), pltpu.SemaphoreType.DMA],
        compiler_params=_cparams("arbitrary"),
        name="moe_combine",
    )(pos, x, info, ys, lw["ln2_g"], lw["ln2_b"])


def _moe(x, lw):
    T = x.shape[0]
    tm_e = 512 if T >= 4096 else 128
    nt = T * TOP_K // tm_e + N_EXPERTS
    info, cnt = _router(x, lw)
    pos, tile_e, n_valid, gend, gsz = _moe_schedule(info, cnt, tm_e, nt)
    xs = _dispatch(x, pos, gend, gsz, n_valid, tm_e, nt)
    ys = _experts(xs, tile_e, n_valid, lw, tm_e)
    return _combine(x, info, pos, ys, lw)


def _ones_block_diag():
    idx = np.arange(RWKV_DIM) // RWKV_HEAD
    return jnp.asarray((idx[:, None] == idx[None, :]).astype(np.float32), dtype=BF16)


def _deinterleave_matrix():
    n = 2 * D_FF
    src = np.concatenate([np.arange(0, n, 2), np.arange(1, n, 2)])
    return jnp.asarray((np.arange(n)[:, None] == src[None, :]).astype(np.float32), dtype=BF16)


def _block_diag(x):
    C, G, r, c = x.shape
    eye = jnp.eye(G, dtype=x.dtype)
    return jnp.einsum("cgij,gh->cgihj", x, eye).reshape(C, G * r, G * c)


def _prep_layer(P, l):
    f = lambda name: P[name][l]
    lw = {}
    w_in = f("w_in")
    offs = np.cumsum((0,) + IN_SIZES)
    cols = lambda j: w_in[:, offs[j]:offs[j + 1]]
    w_q, w_kv = cols(0), cols(1)
    w_ckv, w_kpe = w_kv[:, :MLA_KV_RANK], w_kv[:, MLA_KV_RANK:]
    half = MLA_ROPE // 2
    zpad = jnp.zeros((D_MODEL, LANES - MLA_ROPE), F32)
    w_kpe_rot = jnp.concatenate([-w_kpe[:, half:], w_kpe[:, :half]], 1)
    lw["w_mla"] = jnp.concatenate([w_q, w_ckv, w_kpe, zpad, w_kpe_rot, zpad], 1).astype(BF16)
    lw["w_rw"] = cols(2).astype(BF16)
    lw["w_su"] = cols(3).astype(BF16)
    lw["w_gates"] = jnp.concatenate([cols(4), cols(5), cols(6)], 1).astype(BF16)
    lw["q_norm"] = f("mla_q_a_norm").reshape(1, -1)
    lw["kv_norm"] = f("mla_kv_a_norm").reshape(1, -1)
    wqb = f("mla_w_q_b").reshape(MLA_Q_RANK, MLA_HEADS, MLA_NOPE + MLA_ROPE)
    nope, x1, x2 = wqb[..., :MLA_NOPE], wqb[..., MLA_NOPE:MLA_NOPE + half], wqb[..., MLA_NOPE + half:]
    z32 = jnp.zeros_like(wqb[..., :HEAD_PAD - MLA_NOPE - MLA_ROPE])
    plain = jnp.concatenate([x1, x2, nope, z32], -1).reshape(MLA_Q_RANK, -1)
    rot = jnp.concatenate([-x2, x1, jnp.zeros_like(nope), z32], -1).reshape(MLA_Q_RANK, -1)
    lw["w_qb"] = jnp.concatenate([plain, rot], 1).astype(BF16)
    wkvb = f("mla_w_kv_b").reshape(MLA_KV_RANK, MLA_HEADS, MLA_NOPE + MLA_V)
    k_nope, v = wkvb[..., :MLA_NOPE], wkvb[..., MLA_NOPE:]
    zk = jnp.zeros_like(k_nope[..., :MLA_ROPE])
    lw["w_k"] = jnp.concatenate([zk, k_nope, zk], -1).reshape(MLA_KV_RANK, -1).astype(BF16)
    lw["w_v"] = v.reshape(MLA_KV_RANK, -1).astype(BF16)
    row = lambda name: f(name).reshape(1, -1)
    lw["mu"] = row("rwkv_mu")
    lw["w0"] = row("rwkv_w0")
    z64 = jnp.zeros((64, RWKV_DIM), F32)
    lw["w_up"] = jnp.concatenate([f("rwkv_w_up"), z64], 0).astype(BF16)
    lw["a_up"] = jnp.concatenate([z64, f("rwkv_a_up")], 0).astype(BF16)
    lw["a0"] = row("rwkv_a0")
    lw["g_up"] = f("rwkv_g_up").astype(BF16)
    lw["k_k"] = row("rwkv_k_k")
    lw["k_a"] = row("rwkv_k_a")
    lw["r_k"] = row("rwkv_r_k")
    lw["lnx_g"] = row("rwkv_lnx_g")
    lw["lnx_b"] = row("rwkv_lnx_b")
    lw["ones_bd"] = _ones_block_diag()
    ab_re, ab_im, bb_re, bb_im = _s5_params(f("ssm_a_re"), f("ssm_a_im"), f("ssm_log_dt"),
                                            f("ssm_b_re"), f("ssm_b_im"))
    gpc = SSM_GROUPS // SSM_CHUNKS
    chunked = lambda t: t.reshape(SSM_CHUNKS, gpc, t.shape[1], t.shape[2])
    lw["s5_bre"] = _block_diag(chunked(jnp.transpose(bb_re, (1, 0, 2)))).astype(BF16)
    lw["s5_bim"] = _block_diag(chunked(jnp.transpose(bb_im, (1, 0, 2)))).astype(BF16)
    lw["s5_cre"] = _block_diag(chunked(jnp.transpose(f("ssm_c_re"), (0, 2, 1)))).astype(BF16)
    lw["s5_cim"] = _block_diag(chunked(jnp.transpose(f("ssm_c_im"), (0, 2, 1)))).astype(BF16)
    lw["s5_d"] = row("ssm_d")
    lw["s5_are"] = ab_re.reshape(1, -1)
    lw["s5_aim"] = ab_im.reshape(1, -1)
    lw["glu_w"] = f("ssm_glu_w").astype(BF16)
    lw["glu_b"] = row("ssm_glu_b")
    for name in ("w_br_a", "w_br_b", "w_br_c", "w_out"):
        lw[name] = f(name).astype(BF16)
    for name in ("ln1_g", "ln1_b", "ln2_g", "ln2_b"):
        lw[name] = row(name)
    rw_ = jnp.pad(f("router_w"), ((0, 0), (0, LANES - N_EXPERTS)))
    hi = rw_.astype(BF16)
    lw["router_hi"] = hi
    lw["router_lo"] = (rw_ - hi.astype(F32)).astype(BF16)
    lw["router_b"] = jnp.pad(f("router_b"), (0, LANES - N_EXPERTS), constant_values=NEG_INF).reshape(1, -1)
    rows_l = N_EXPERTS * D_MODEL
    wgu = _matmul(P["exp_w_gu"].reshape(-1, 2 * D_FF), _deinterleave_matrix(), 1, rows_l, out_dtype=BF16,
                  row0=l * rows_l)
    lw["w_gu"] = wgu.reshape(N_EXPERTS, D_MODEL, 2 * D_FF)
    bgu = f("exp_b_gu")
    lw["b_gu"] = jnp.concatenate([bgu[..., 0::2], bgu[..., 1::2]], -1).reshape(N_EXPERTS, 1, 2 * D_FF)
    lw["w_down"] = P["exp_w_down"].reshape(-1, D_FF, D_MODEL)
    lw["w_down_e0"] = l * N_EXPERTS
    lw["b_down"] = f("exp_b_down").reshape(N_EXPERTS, 1, D_MODEL)
    return lw


def _rope_tables(pos):
    half = MLA_ROPE // 2
    inv = ROPE_THETA ** (-jnp.arange(half, dtype=F32) / half)
    ang = pos.astype(F32)[:, None] * inv
    cos, sin = jnp.cos(ang), jnp.sin(ang)
    n = pos.shape[0]
    tc = jnp.concatenate([cos, cos, jnp.ones((n, MLA_NOPE), F32),
                          jnp.zeros((n, HEAD_PAD - MLA_NOPE - MLA_ROPE), F32)], 1)
    ts = jnp.concatenate([sin, sin, jnp.zeros((n, HEAD_PAD - MLA_ROPE), F32)], 1)
    return tc, ts


def _to_lanes(t, nb, s, lanes):
    t = t.reshape(s, nb * RWKV_HEADS, RWKV_HEAD).transpose(0, 2, 1)
    return jnp.pad(t, ((0, 0), (0, 0), (0, lanes - nb * RWKV_HEADS)))


def _layer(x, lw, past, nb, s, tables, att_tk=LANES):
    T = nb * s
    tc, ts = tables
    if past is None:
        start = 0
        shift_p = jnp.zeros((nb, RWKV_PROJ), F32)
        wkv_p = jnp.zeros((nb, RWKV_HEADS, RWKV_HEAD, RWKV_HEAD), F32)
        sre_p = jnp.zeros((nb, SSM_CH), F32)
        sim_p = jnp.zeros((nb, SSM_CH), F32)
    else:
        ckv_p, kpe_p, shift_p, wkv_p, sre_p, sim_p = past
        start = ckv_p.shape[1]
        sre_p = sre_p.reshape(nb, SSM_CH)
        sim_p = sim_p.reshape(nb, SSM_CH)

    q, ckv, kpe128 = _mla_in(x, lw, tc, ts)
    if past is None:
        sk = s
        ckv_all, kpe_all = ckv, kpe128
    else:
        sk = start + s
        ckv_all = jnp.concatenate([ckv_p, ckv.reshape(nb, s, -1)], 1)
        kpe_new = kpe128.reshape(nb, s, LANES)
        kpe_all = jnp.concatenate([jnp.pad(kpe_p, ((0, 0), (0, 0), (0, LANES - MLA_ROPE))), kpe_new], 1)
    skp = -(-sk // att_tk) * att_tk
    if skp != sk:
        ckv_all = jnp.pad(ckv_all.reshape(nb, sk, -1), ((0, 0), (0, skp - sk), (0, 0)))
        kpe_all = jnp.pad(kpe_all.reshape(nb, sk, -1), ((0, 0), (0, skp - sk), (0, 0)))
    k_pad, v_all = _kv_expand(ckv_all.reshape(nb * skp, -1), kpe_all.reshape(nb * skp, -1), lw)
    ya = _attention(q, k_pad, v_all, nb, s, skp, sk, start)

    r, wdec, kn, v, kk, kb, g, bonus, last_row = _rwkv_pre(x, shift_p, lw, nb, s)
    lanes = -(-nb * RWKV_HEADS // LANES) * LANES
    seqs = [_to_lanes(t, nb, s, lanes) for t in (r, wdec, kn, v, kk, kb)]
    s0 = jnp.transpose(wkv_p, (3, 2, 0, 1)).reshape(RWKV_HEAD, RWKV_HEAD, nb * RWKV_HEADS)
    s0 = jnp.pad(s0, ((0, 0), (0, 0), (0, lanes - nb * RWKV_HEADS)))
    o_l, sT = _rwkv_scan(seqs, s0, s)
    o_tm = o_l[:, :, :nb * RWKV_HEADS].transpose(0, 2, 1).reshape(s, nb * RWKV_DIM)
    wkv_n = (sT[:, :, :nb * RWKV_HEADS].reshape(RWKV_HEAD, RWKV_HEAD, nb, RWKV_HEADS).transpose(2, 3, 1, 0))
    shift_n = last_row.reshape(nb, RWKV_PROJ)

    su_tm = _matmul(x, lw["w_su"], nb, s, time_major_out=True).reshape(s * nb, SSM_DIM)
    yc_tm, sre_n, sim_n = _s5(su_tm, sre_p, sim_p, lw, nb, s)
    yc_tm = yc_tm.reshape(s, nb * SSM_DIM)

    x1 = _merge(x, ya, o_tm, bonus, g, yc_tm, lw, nb, s)
    x2 = _moe(x1, lw)
    new = (ckv.reshape(nb, s, MLA_KV_RANK), kpe128[:, :MLA_ROPE].reshape(nb, s, MLA_ROPE), shift_n, wkv_n,
           sre_n.reshape(nb, SSM_GROUPS, SSM_STATE), sim_n.reshape(nb, SSM_GROUPS, SSM_STATE))
    return x2, new


def _trunk(x3, weights, caches):
    nb, s, d = x3.shape
    start = 0 if caches is None else caches[0].shape[2]
    tc, ts = _rope_tables(start + jnp.arange(s))
    tables = (jnp.tile(tc, (nb, 1)), jnp.tile(ts, (nb, 1)))
    x = x3.reshape(nb * s, d)
    new = []
    for l in range(len(weights)):
        past = None if caches is None else tuple(c[l] for c in caches)
        x, st = _layer(x, weights[l], past, nb, s, tables)
        new.append(st)
    return (x.reshape(nb, s, d),) + tuple(jnp.stack([st[j] for st in new]) for j in range(6))


def kernel(x_prompt, x_sample, cache_mla_ckv, cache_mla_kpe, state_rwkv_shift, state_rwkv_wkv, state_ssm_re, state_ssm_im, w_in, mla_q_a_norm, mla_w_q_b, mla_kv_a_norm, mla_w_kv_b, rwkv_mu, rwkv_w0, rwkv_w_up, rwkv_a0, rwkv_a_up, rwkv_g_up, rwkv_k_k, rwkv_k_a, rwkv_r_k, rwkv_lnx_g, rwkv_lnx_b, ssm_a_re, ssm_a_im, ssm_b_re, ssm_b_im, ssm_c_re, ssm_c_im, ssm_d, ssm_log_dt, ssm_glu_w, ssm_glu_b, w_br_a, w_br_b, w_br_c, w_out, ln1_g, ln1_b, router_w, router_b, exp_w_gu, exp_b_gu, exp_w_down, exp_b_down, ln2_g, ln2_b):
    P = dict(w_in=w_in, mla_q_a_norm=mla_q_a_norm, mla_w_q_b=mla_w_q_b, mla_kv_a_norm=mla_kv_a_norm,
             mla_w_kv_b=mla_w_kv_b, rwkv_mu=rwkv_mu, rwkv_w0=rwkv_w0, rwkv_w_up=rwkv_w_up, rwkv_a0=rwkv_a0,
             rwkv_a_up=rwkv_a_up, rwkv_g_up=rwkv_g_up, rwkv_k_k=rwkv_k_k, rwkv_k_a=rwkv_k_a, rwkv_r_k=rwkv_r_k,
             rwkv_lnx_g=rwkv_lnx_g, rwkv_lnx_b=rwkv_lnx_b, ssm_a_re=ssm_a_re, ssm_a_im=ssm_a_im,
             ssm_b_re=ssm_b_re, ssm_b_im=ssm_b_im, ssm_c_re=ssm_c_re, ssm_c_im=ssm_c_im, ssm_d=ssm_d,
             ssm_log_dt=ssm_log_dt, ssm_glu_w=ssm_glu_w, ssm_glu_b=ssm_glu_b, w_br_a=w_br_a, w_br_b=w_br_b,
             w_br_c=w_br_c, w_out=w_out, ln1_g=ln1_g, ln1_b=ln1_b, router_w=router_w, router_b=router_b,
             exp_w_gu=exp_w_gu, exp_b_gu=exp_b_gu, exp_w_down=exp_w_down, exp_b_down=exp_b_down,
             ln2_g=ln2_g, ln2_b=ln2_b)
    depth = w_in.shape[0]
    weights = [_prep_layer(P, l) for l in range(depth)]
    outs_p = _trunk(x_prompt, weights, None)
    caches = (cache_mla_ckv, cache_mla_kpe, state_rwkv_shift, state_rwkv_wkv, state_ssm_re, state_ssm_im)
    outs_s = _trunk(x_sample, weights, caches)
    return (outs_p[0], outs_s[0]) + outs_p[1:] + outs_s[1:]
```
